```python
import jax
import jax.numpy as jnp
from jax import lax
import numpy as np

D_MODEL = 2048
BATCH = 4
SEQ = 4096
DEPTH = 1

GRID_W = 64
CTX_LEN = 256
HEAD_DIM = 128
MIX_WIDTH = D_MODEL
A_HEADS = MIX_WIDTH // (2 * HEAD_DIM)
A_KV_HEADS = A_HEADS // 4
A_GROUP = A_HEADS // A_KV_HEADS
A_WIDTH = A_HEADS * HEAD_DIM
WINDOW = 128
A_BLOCK = 128
ROPE_THETA = 10000.0
B_WIDTH = MIX_WIDTH - A_WIDTH
B_HEADS = 4
B_DV = B_WIDTH // B_HEADS
B_DK = B_DV // 2
B_KEY_WIDTH = B_HEADS * B_DK
GATE_RANK = 16
GATE_TAU = 16.0
GLA_CHUNK = 64
IN_SPLITS = (A_WIDTH, A_KV_HEADS * HEAD_DIM, A_KV_HEADS * HEAD_DIM, B_KEY_WIDTH, B_KEY_WIDTH, B_WIDTH, B_WIDTH, 2 * GATE_RANK)
IN_WIDTH = sum(IN_SPLITS)
N_GROUPS = 4
EXPERTS_PER_GROUP = 8
N_EXPERTS = N_GROUPS * EXPERTS_PER_GROUP
TOP_K = 2
EXPERT_HIDDEN = D_MODEL // 2
MOE_BLOCK = 128
ALPHA = (2.0 * DEPTH) ** 0.25
BETA = (8.0 * DEPTH) ** -0.25
LN_EPS = 1e-6

kernel_name = 'hymba_style_gqa_gla_hmoe_dit_layer'


def layer_norm(x, gain=None, bias=None):
    xf = x.astype(jnp.float32)
    mu = jnp.mean(xf, axis=-1, keepdims=True)
    var = jnp.mean(jnp.square(xf - mu), axis=-1, keepdims=True)
    y = (xf - mu) * lax.rsqrt(var + LN_EPS)
    if gain is not None:
        y = y * gain.astype(jnp.float32) + bias.astype(jnp.float32)
    return y.astype(x.dtype)


def modulate(x, shift, scale):
    return layer_norm(x) * (1 + scale) + shift


def heads(t, n):
    return t.reshape(t.shape[:-1] + (n, t.shape[-1] // n))


def flip(t):
    return t[:, ::-1]


def axial_rope(rows, dtype):
    n_freq = HEAD_DIM // 4
    inv_freq = ROPE_THETA ** (-jnp.arange(n_freq, dtype=jnp.float32) / n_freq)
    row = jnp.repeat(jnp.arange(rows, dtype=jnp.float32), GRID_W)
    col = jnp.tile(jnp.arange(GRID_W, dtype=jnp.float32), rows)
    ang = jnp.stack([row[:, None] * inv_freq, col[:, None] * inv_freq], axis=1)
    return jnp.cos(ang)[:, None].astype(dtype), jnp.sin(ang)[:, None].astype(dtype)


def apply_rope(t, cos, sin):
    tr = t.reshape(t.shape[:-1] + (2, 2, HEAD_DIM // 4))
    t1, t2 = tr[..., 0, :], tr[..., 1, :]
    return jnp.stack([t1 * cos - t2 * sin, t2 * cos + t1 * sin], axis=-2).reshape(t.shape)


def windowed_gqa(q, k, v, k_ctx, v_ctx, sink):
    B, S = q.shape[:2]
    L = k_ctx.shape[1]
    nb = S // A_BLOCK
    scale = HEAD_DIM ** -0.5
    qb = q.reshape(B, nb, A_BLOCK, A_KV_HEADS, A_GROUP, HEAD_DIM)
    pad = ((0, 0), (A_BLOCK, A_BLOCK), (0, 0), (0, 0))
    kp = jnp.pad(k, pad).reshape(B, nb + 2, A_BLOCK, A_KV_HEADS, HEAD_DIM)
    vp = jnp.pad(v, pad).reshape(B, nb + 2, A_BLOCK, A_KV_HEADS, HEAD_DIM)
    band = lambda t: jnp.concatenate([t[:, :-2], t[:, 1:-1], t[:, 2:]], axis=2)
    kb, vb = band(kp), band(vp)
    s_band = jnp.einsum('bnqhgd,bnkhd->bnhgqk', qb, kb).astype(jnp.float32) * scale
    blk = jnp.arange(nb)[:, None] * A_BLOCK
    qpos = blk + jnp.arange(A_BLOCK)[None, :]
    kpos = blk - A_BLOCK + jnp.arange(3 * A_BLOCK)[None, :]
    rel = kpos[:, None, :] - qpos[:, :, None]
    valid = (jnp.abs(rel) <= WINDOW) & (kpos[:, None, :] >= 0) & (kpos[:, None, :] < S)
    s_band = jnp.where(valid[None, :, None, None], s_band, -jnp.inf)
    s_ctx = jnp.einsum('bnqhgd,blhd->bnhgql', qb, k_ctx).astype(jnp.float32) * scale
    sink_l = jnp.broadcast_to(sink.astype(jnp.float32).reshape(A_KV_HEADS, A_GROUP)[None, None, :, :, None, None], (B, nb, A_KV_HEADS, A_GROUP, A_BLOCK, 1))
    p = jax.nn.softmax(jnp.concatenate([sink_l, s_ctx, s_band], axis=-1), axis=-1).astype(v.dtype)
    out = jnp.einsum('bnhgql,blhd->bnqhgd', p[..., 1:1 + L], v_ctx) + jnp.einsum('bnhgqk,bnkhd->bnqhgd', p[..., 1 + L:], vb)
    return out.reshape(B, S, A_WIDTH)


def context_attention(q_ctx, k_ctx, v_ctx, sink):
    B, L = q_ctx.shape[:2]
    qg = q_ctx.reshape(B, L, A_KV_HEADS, A_GROUP, HEAD_DIM)
    s = jnp.einsum('blhgd,bmhd->bhglm', qg, k_ctx).astype(jnp.float32) * HEAD_DIM ** -0.5
    sink_l = jnp.broadcast_to(sink.astype(jnp.float32).reshape(A_KV_HEADS, A_GROUP)[None, :, :, None, None], (B, A_KV_HEADS, A_GROUP, L, 1))
    p = jax.nn.softmax(jnp.concatenate([sink_l, s], axis=-1), axis=-1).astype(v_ctx.dtype)
    out = jnp.einsum('bhglm,bmhd->blhgd', p[..., 1:], v_ctx)
    return out.reshape(B, L, A_WIDTH)


def gla_log_decay(g_low, w_up, b_up):
    z = (g_low @ w_up + b_up).astype(jnp.float32)
    return heads(jax.nn.log_sigmoid(z) / GATE_TAU, B_HEADS)


def gla_chunked(q, k, v, log_a, state0):
    B, N, H, dk = q.shape
    nc = N // GLA_CHUNK
    chunks = lambda t: t.astype(jnp.float32).reshape(B, nc, GLA_CHUNK, H, t.shape[-1]).transpose(0, 1, 3, 2, 4)
    qc, kc, vc, gc = chunks(q) * dk ** -0.5, chunks(k), chunks(v), chunks(log_a)
    b = jnp.cumsum(gc, axis=3)
    b_last = b[:, :, :, -1:]
    b_mid = b[:, :, :, GLA_CHUNK // 2 - 1:GLA_CHUNK // 2]
    qm = qc * jnp.exp(b - b_mid)
    km = kc * jnp.exp(b_mid - b)
    tri = jnp.tril(jnp.ones((GLA_CHUNK, GLA_CHUNK), dtype=bool))
    a_intra = jnp.where(tri, jnp.einsum('bnhtd,bnhsd->bnhts', qm, km), 0.0)
    o_intra = jnp.einsum('bnhts,bnhsv->bnhtv', a_intra, vc)
    kv = jnp.einsum('bnhsd,bnhsv->bnhdv', kc * jnp.exp(b_last - b), vc)
    decay = jnp.exp(b_last[:, :, :, 0])

    def step(s, inp):
        d, kv_c = inp
        return d[..., None] * s + kv_c, s

    s_final, s_prev = lax.scan(step, state0, (decay.transpose(1, 0, 2, 3), kv.transpose(1, 0, 2, 3, 4)))
    s_prev = s_prev.transpose(1, 0, 2, 3, 4)
    o_inter = jnp.einsum('bnhtd,bnhdv->bnhtv', qc * jnp.exp(b), s_prev)
    o = (o_intra + o_inter).transpose(0, 1, 3, 2, 4).reshape(B, N, H, v.shape[-1])
    return o.astype(v.dtype), s_final


def gla_output(o, r, norm_w):
    of = o.astype(jnp.float32)
    of = of * lax.rsqrt(jnp.mean(jnp.square(of), axis=-1, keepdims=True) + LN_EPS)
    of = of.reshape(o.shape[:2] + (B_WIDTH,)) * norm_w.astype(jnp.float32)
    return of.astype(r.dtype) * jax.nn.silu(r)


def mixing_sublayer(h, h_ctx, cos, sin, w_in, w_gate_up, b_gate, sink, norm_w, w_out, need_ctx):
    split_at = [int(s) for s in np.cumsum(IN_SPLITS)[:-1]]
    qa, ka, va, qb, kb, vb, rb, gb = jnp.split(h @ w_in, split_at, axis=-1)
    qa_c, ka_c, va_c, qb_c, kb_c, vb_c, rb_c, gb_c = jnp.split(h_ctx @ w_in, split_at, axis=-1)
    qa = apply_rope(heads(qa, A_HEADS), cos, sin)
    ka = apply_rope(heads(ka, A_KV_HEADS), cos, sin)
    va = heads(va, A_KV_HEADS)
    ka_c, va_c = heads(ka_c, A_KV_HEADS), heads(va_c, A_KV_HEADS)
    out_a = windowed_gqa(qa, ka, va, ka_c, va_c, sink)
    qb, kb, vb = heads(qb, B_HEADS), heads(kb, B_HEADS), heads(vb, B_HEADS)
    qb_c, kb_c, vb_c = heads(qb_c, B_HEADS), heads(kb_c, B_HEADS), heads(vb_c, B_HEADS)
    la_f = gla_log_decay(gb[..., :GATE_RANK], w_gate_up[0], b_gate[0])
    la_b = gla_log_decay(gb[..., GATE_RANK:], w_gate_up[1], b_gate[1])
    lac_f = gla_log_decay(gb_c[..., :GATE_RANK], w_gate_up[0], b_gate[0])
    lac_b = gla_log_decay(gb_c[..., GATE_RANK:], w_gate_up[1], b_gate[1])
    s0 = jnp.zeros((h.shape[0], B_HEADS, B_DK, B_DV), jnp.float32)
    oc_f, sc_f = gla_chunked(qb_c, kb_c, vb_c, lac_f, s0)
    oc_b, sc_b = gla_chunked(flip(qb_c), flip(kb_c), flip(vb_c), flip(lac_b), s0)
    o_f, _ = gla_chunked(qb, kb, vb, la_f, sc_f)
    o_b, _ = gla_chunked(flip(qb), flip(kb), flip(vb), flip(la_b), sc_b)
    out_b = gla_output(o_f + flip(o_b), rb, norm_w)
    y = jnp.concatenate([out_a, out_b], axis=-1) @ w_out
    if not need_ctx:
        return y, None
    out_a_c = context_attention(heads(qa_c, A_HEADS), ka_c, va_c, sink)
    out_b_c = gla_output(oc_f + flip(oc_b), rb_c, norm_w)
    y_ctx = jnp.concatenate([out_a_c, out_b_c], axis=-1) @ w_out
    return y, y_ctx


def hier_moe(h, w_rg, b_rg, w_re, b_re, w_gate, w_up, w_down):
    T, D = h.shape
    pg = jax.nn.softmax((h @ w_rg).astype(jnp.float32) + b_rg.astype(jnp.float32), axis=-1)
    pg_top, grp = lax.top_k(pg, 1)
    el = ((h @ w_re).astype(jnp.float32) + b_re.astype(jnp.float32)).reshape(T, N_GROUPS, EXPERTS_PER_GROUP)
    el = jnp.take_along_axis(el, grp[:, :, None], axis=1)[:, 0]
    top_p, top_i = lax.top_k(jax.nn.softmax(el, axis=-1), TOP_K)
    weights = pg_top * top_p / jnp.sum(top_p, axis=-1, keepdims=True)
    expert_id = (grp * EXPERTS_PER_GROUP + top_i).reshape(-1)
    M = T * TOP_K
    token = jnp.arange(M) // TOP_K
    order = jnp.argsort(expert_id, stable=True)
    e_sorted = expert_id[order]
    counts = jnp.zeros((N_EXPERTS,), jnp.int32).at[expert_id].add(1)
    start = jnp.cumsum(counts) - counts
    padded = (counts + MOE_BLOCK - 1) // MOE_BLOCK * MOE_BLOCK
    pad_end = jnp.cumsum(padded)
    pad_start = pad_end - padded
    dest = pad_start[e_sorted] + jnp.arange(M) - start[e_sorted]
    n_blocks = -(-M // MOE_BLOCK) + N_EXPERTS
    buf = jnp.zeros((n_blocks * MOE_BLOCK, D), h.dtype).at[dest].set(h[token[order]])
    block_e = jnp.minimum(jnp.searchsorted(pad_end, jnp.arange(n_blocks) * MOE_BLOCK, side='right'), N_EXPERTS - 1)

    def expert_block(args):
        xb, e = args
        return (jax.nn.silu(xb @ w_gate[e]) * (xb @ w_up[e])) @ w_down[e]

    out = lax.map(expert_block, (buf.reshape(n_blocks, MOE_BLOCK, D), block_e)).reshape(-1, D)
    w_sorted = weights.reshape(-1)[order].astype(h.dtype)
    return jax.ops.segment_sum(out[dest] * w_sorted[:, None], token[order], num_segments=T)


def setup_inputs(seed: int = 0) -> dict:
    key = jax.random.key(seed)
    ks = jax.random.split(key, 24)
    n = lambda k, shape, s: jax.random.normal(k, shape, jnp.float32) * s
    D = D_MODEL
    return {
        'x': n(ks[0], (BATCH, SEQ, D), 1.0),
        'c': n(ks[1], (BATCH, D), 1.0),
        'ctx': n(ks[2], (BATCH, CTX_LEN, D), 1.0),
        'c_ctx': n(ks[3], (D,), 1.0),
        'w_ada': n(ks[4], (DEPTH, D, 6 * D), D ** -0.5),
        'b_ada': n(ks[5], (DEPTH, 6 * D), 0.02),
        'w_in': n(ks[6], (DEPTH, D, IN_WIDTH), D ** -0.5),
        'w_gate_up': n(ks[7], (DEPTH, 2, GATE_RANK, B_KEY_WIDTH), GATE_RANK ** -0.5),
        'b_gate': n(ks[8], (DEPTH, 2, B_KEY_WIDTH), 0.1),
        'attn_sink': n(ks[9], (DEPTH, A_HEADS), 1.0),
        'gla_norm_w': 1.0 + n(ks[10], (DEPTH, B_WIDTH), 0.02),
        'w_out': n(ks[11], (DEPTH, MIX_WIDTH, D), MIX_WIDTH ** -0.5 * BETA),
        'ln1_g': 1.0 + n(ks[12], (DEPTH, D), 0.02),
        'ln1_b': n(ks[13], (DEPTH, D), 0.02),
        'w_router_group': n(ks[14], (DEPTH, D, N_GROUPS), D ** -0.5),
        'b_router_group': n(ks[15], (DEPTH, N_GROUPS), 0.01),
        'w_router_expert': n(ks[16], (DEPTH, D, N_EXPERTS), D ** -0.5),
        'b_router_expert': n(ks[17], (DEPTH, N_EXPERTS), 0.01),
        'w_exp_gate': n(ks[18], (DEPTH, N_EXPERTS, D, EXPERT_HIDDEN), D ** -0.5),
        'w_exp_up': n(ks[19], (DEPTH, N_EXPERTS, D, EXPERT_HIDDEN), D ** -0.5),
        'w_exp_down': n(ks[20], (DEPTH, N_EXPERTS, EXPERT_HIDDEN, D), EXPERT_HIDDEN ** -0.5 * BETA),
        'ln2_g': 1.0 + n(ks[21], (DEPTH, D), 0.02),
        'ln2_b': n(ks[22], (DEPTH, D), 0.02),
    }


def reference(x, c, ctx, c_ctx, w_ada, b_ada, w_in, w_gate_up, b_gate, attn_sink, gla_norm_w, w_out, ln1_g, ln1_b, w_router_group, b_router_group, w_router_expert, b_router_expert, w_exp_gate, w_exp_up, w_exp_down, ln2_g, ln2_b):
    B, S, D = x.shape
    L = ctx.shape[1]
    rows = S // GRID_W
    cos, sin = axial_rope(rows, x.dtype)
    for layer in range(DEPTH):
        need_ctx = layer + 1 < DEPTH
        mod = jax.nn.silu(c) @ w_ada[layer] + b_ada[layer]
        mod_c = jax.nn.silu(c_ctx) @ w_ada[layer] + b_ada[layer]
        sh1, sc1, g1, sh2, sc2, g2 = jnp.split(mod[:, None, :], 6, axis=-1)
        csh1, csc1, cg1, csh2, csc2, cg2 = jnp.split(mod_c, 6, axis=-1)
        h = modulate(x, sh1, sc1)
        h_c = modulate(ctx, csh1, csc1)
        y, y_c = mixing_sublayer(h, h_c, cos, sin, w_in[layer], w_gate_up[layer], b_gate[layer], attn_sink[layer], gla_norm_w[layer], w_out[layer], need_ctx)
        x = layer_norm(ALPHA * x + g1 * y, ln1_g[layer], ln1_b[layer])
        h = modulate(x, sh2, sc2).reshape(B * S, D)
        moe_args = (w_router_group[layer], b_router_group[layer], w_router_expert[layer], b_router_expert[layer], w_exp_gate[layer], w_exp_up[layer], w_exp_down[layer])
        if need_ctx:
            ctx = layer_norm(ALPHA * ctx + cg1 * y_c, ln1_g[layer], ln1_b[layer])
            h_c = modulate(ctx, csh2, csc2).reshape(B * L, D)
            f = hier_moe(jnp.concatenate([h_c, h], axis=0), *moe_args)
            ctx = layer_norm(ALPHA * ctx + cg2 * f[:B * L].reshape(B, L, D), ln2_g[layer], ln2_b[layer])
            f_x = f[B * L:]
        else:
            f_x = hier_moe(h, *moe_args)
        x = layer_norm(ALPHA * x + g2 * f_x.reshape(B, S, D), ln2_g[layer], ln2_b[layer])
    return x
```

```python
import functools

import jax
import jax.numpy as jnp
from jax import lax
from jax.experimental import pallas as pl
from jax.experimental.pallas import tpu as pltpu

F32 = jnp.float32
BF16 = jnp.bfloat16

HEAD_DIM = 128
GRID_W = 64
WINDOW = 128
A_BLOCK = 128
A_GROUP = 4
ROPE_THETA = 10000.0
B_HEADS = 4
GATE_RANK = 16
GATE_TAU = 16.0
GLA_CHUNK = 64
N_GROUPS = 4
EXPERTS_PER_GROUP = 8
N_EXPERTS = N_GROUPS * EXPERTS_PER_GROUP
TOP_K = 2
DEPTH = 1
ALPHA = (2.0 * DEPTH) ** 0.25
LN_EPS = 1e-6

LANES = 128
MOD_ROWS = 8
VMEM_LIMIT = 56 * 1024 * 1024

ROW_TILE = 256
GLA_STEP = 256
MOE_BLOCK = 256
HID_CHUNK = 256


def _params(sem):
    return pltpu.CompilerParams(dimension_semantics=sem, vmem_limit_bytes=VMEM_LIMIT)


def _resident(shape):
    nd = len(shape)
    return pl.BlockSpec(shape, lambda *_: (0,) * nd, pipeline_mode=pl.Buffered(1))


def _ln(x):
    mu = jnp.mean(x, axis=-1, keepdims=True)
    xc = x - mu
    var = jnp.mean(xc * xc, axis=-1, keepdims=True)
    return xc * lax.rsqrt(var + LN_EPS)


def _silu(x):
    return x * jax.nn.sigmoid(x)


def _bdot(a, b):
    return jnp.dot(a, b, preferred_element_type=F32)


def _bdot_nt(a, b):
    return lax.dot_general(a, b, (((1,), (1,)), ((), ())), preferred_element_type=F32)


def _adaln_kernel(c_ref, w_ref, b_ref, o_ref):
    s = _silu(c_ref[...]).astype(BF16)
    o_ref[...] = _bdot(s, w_ref[...].astype(BF16)) + b_ref[...]


def _adaln(cc, w_ada, b_ada):
    d, n = w_ada.shape
    tn = 1024
    return pl.pallas_call(
        _adaln_kernel,
        grid=(n // tn,),
        in_specs=[pl.BlockSpec((MOD_ROWS, d), lambda j: (0, 0)),
                  pl.BlockSpec((d, tn), lambda j: (0, j)),
                  pl.BlockSpec((1, tn), lambda j: (0, j))],
        out_specs=pl.BlockSpec((MOD_ROWS, tn), lambda j: (0, j)),
        out_shape=jax.ShapeDtypeStruct((MOD_ROWS, n), F32),
        compiler_params=_params(("arbitrary",)),
        name="adaln",
    )(cc, w_ada, b_ada.reshape(1, n))


def _proj_kernel(x_ref, mod_ref, cos_ref, sina_ref, sinb_ref,
                 wqa, wka, wva, wqb, wkb, wvb, wrb, wg, wup, bg,
                 qa_o, ka_o, va_o, qb_o, kb_o, vb_o, rb_o, laf_o, lab_o, *, rope):
    h = _ln(x_ref[...]) * (1.0 + mod_ref[1:2, :]) + mod_ref[0:1, :]
    hb = h.astype(BF16)

    def rot(t):
        return (t * cos_ref[...] + pltpu.roll(t, 96, 1) * sina_ref[...]
                + pltpu.roll(t, 32, 1) * sinb_ref[...])

    def project(w_ref, o_ref, rotate):
        n = w_ref.shape[1]
        step = min(n, 2 * LANES)
        for c0 in range(0, n, step):
            t = _bdot(hb, w_ref[:, c0:c0 + step])
            if rotate:
                for l0 in range(0, step, LANES):
                    o_ref[:, c0 + l0:c0 + l0 + LANES] = rot(t[:, l0:l0 + LANES]).astype(o_ref.dtype)
            else:
                o_ref[:, c0:c0 + step] = t.astype(o_ref.dtype)

    project(wqa, qa_o, rope)
    project(wka, ka_o, rope)
    project(wva, va_o, False)
    project(wqb, qb_o, False)
    project(wkb, kb_o, False)
    project(wvb, vb_o, False)
    project(wrb, rb_o, False)
    gl = _bdot(hb, wg[...]).astype(BF16)
    z = _bdot(gl, wup[...]) + bg[...]
    la = (jnp.minimum(z, 0.0) - jnp.log1p(jnp.exp(-jnp.abs(z)))) / GATE_TAU
    kw = laf_o.shape[1]
    laf_o[...] = la[:, :kw]
    lab_o[...] = la[:, kw:]


def _project(xf, mod3, mod_row_of_tile, tables, weights, wup2, bg2, *, rope, tiles_per_seq):
    r, d = xf.shape
    tm = ROW_TILE
    wqa, wka, wva, wqb, wkb, wvb, wrb, wg = weights
    cos_t, sina_t, sinb_t = tables
    row = lambda n: pl.BlockSpec((tm, n), lambda i: (i, 0))
    tab = pl.BlockSpec((tm, LANES), lambda i: (i % tiles_per_seq, 0))
    outs = [(wqa.shape[1], BF16), (wka.shape[1], BF16), (wva.shape[1], BF16),
            (wqb.shape[1], F32), (wkb.shape[1], F32), (wvb.shape[1], BF16),
            (wrb.shape[1], F32), (wqb.shape[1], F32), (wqb.shape[1], F32)]
    return pl.pallas_call(
        functools.partial(_proj_kernel, rope=rope),
        grid=(r // tm,),
        in_specs=[row(d),
                  pl.BlockSpec((None, 6, d), lambda i: (mod_row_of_tile(i), 0, 0)),
                  tab, tab, tab]
                 + [_resident(w.shape) for w in weights]
                 + [_resident(wup2.shape), _resident(bg2.shape)],
        out_specs=[row(n) for n, _ in outs],
        out_shape=[jax.ShapeDtypeStruct((r, n), dt) for n, dt in outs],
        compiler_params=_params(("parallel",)),
        name="in_proj_rope" if rope else "in_proj_ctx",
    )(xf, mod3, cos_t, sina_t, sinb_t, *weights, wup2, bg2)


def _attn_kernel(sink_ref, q_ref, kp_ref, kc_ref, kn_ref, vp_ref, vc_ref, vn_ref, kx_ref, vx_ref, o_ref):
    n = pl.program_id(1)
    nb = pl.num_programs(1)
    blk = A_BLOCK
    rows = A_GROUP * blk
    scale = HEAD_DIM ** -0.5
    n_ctx = kx_ref.shape[0]
    qi = lax.broadcasted_iota(jnp.int32, (rows, blk), 0) % blk
    kj = lax.broadcasted_iota(jnp.int32, (rows, blk), 1)
    ok_prev = (kj >= qi) & (n > 0)
    ok_next = (kj <= qi) & (n < nb - 1)
    for hk in range(kp_ref.shape[1] // HEAD_DIM):
        ks = slice(hk * HEAD_DIM, (hk + 1) * HEAD_DIM)
        q4 = jnp.concatenate(
            [q_ref[:, (hk * A_GROUP + g) * HEAD_DIM:(hk * A_GROUP + g + 1) * HEAD_DIM] for g in range(A_GROUP)],
            axis=0)
        s_ctx = _bdot_nt(q4, kx_ref[:, ks]) * scale
        s_prev = jnp.where(ok_prev, _bdot_nt(q4, kp_ref[:, ks]) * scale, -jnp.inf)
        s_cur = _bdot_nt(q4, kc_ref[:, ks]) * scale
        s_next = jnp.where(ok_next, _bdot_nt(q4, kn_ref[:, ks]) * scale, -jnp.inf)
        sink = jnp.concatenate(
            [jnp.full((blk, 1), sink_ref[hk * A_GROUP + g], F32) for g in range(A_GROUP)], axis=0)
        m = jnp.maximum(jnp.max(s_ctx, axis=-1, keepdims=True), jnp.max(s_cur, axis=-1, keepdims=True))
        m = jnp.maximum(m, jnp.max(s_prev, axis=-1, keepdims=True))
        m = jnp.maximum(m, jnp.max(s_next, axis=-1, keepdims=True))
        m = jnp.maximum(m, sink)
        e_ctx = jnp.exp(s_ctx - m)
        e_prev = jnp.exp(s_prev - m)
        e_cur = jnp.exp(s_cur - m)
        e_next = jnp.exp(s_next - m)
        den = (jnp.exp(sink - m) + jnp.sum(e_ctx, axis=-1, keepdims=True)
               + jnp.sum(e_prev, axis=-1, keepdims=True) + jnp.sum(e_cur, axis=-1, keepdims=True)
               + jnp.sum(e_next, axis=-1, keepdims=True))
        inv = 1.0 / den
        o = (_bdot((e_ctx * inv).astype(BF16), vx_ref[:, ks])
             + (_bdot((e_prev * inv).astype(BF16), vp_ref[:, ks])
                + _bdot((e_cur * inv).astype(BF16), vc_ref[:, ks])
                + _bdot((e_next * inv).astype(BF16), vn_ref[:, ks])))
        for g in range(A_GROUP):
            hq = hk * A_GROUP + g
            o_ref[:, hq * HEAD_DIM:(hq + 1) * HEAD_DIM] = o[g * blk:(g + 1) * blk, :].astype(o_ref.dtype)


def _attention(sink, qa, ka, va, ka_c, va_c, batch, seq, n_ctx):
    nb = seq // A_BLOCK
    aw = qa.shape[1]
    kvw = ka.shape[1]
    blk = A_BLOCK
    prev = lambda b, n: (b * nb + jnp.maximum(n - 1, 0), 0)
    cur = lambda b, n: (b * nb + n, 0)
    nxt = lambda b, n: (b * nb + jnp.minimum(n + 1, nb - 1), 0)
    kv = lambda f: pl.BlockSpec((blk, kvw), f)
    ctx = pl.BlockSpec((n_ctx, kvw), lambda b, n: (b, 0))
    return pl.pallas_call(
        _attn_kernel,
        grid=(batch, nb),
        in_specs=[pl.BlockSpec(memory_space=pltpu.SMEM),
                  pl.BlockSpec((blk, aw), cur),
                  kv(prev), kv(cur), kv(nxt), kv(prev), kv(cur), kv(nxt), ctx, ctx],
        out_specs=pl.BlockSpec((blk, aw), cur),
        out_shape=jax.ShapeDtypeStruct((batch * seq, aw), BF16),
        compiler_params=_params(("parallel", "parallel")),
        name="window_gqa",
    )(sink, qa, ka, ka, ka, va, va, va, ka_c, va_c)


def _gla_chunk(q, k, v, g, state, *, reverse, need_o):
    c, dk = k.shape
    r = lax.broadcasted_iota(jnp.int32, (c, c), 0)
    s = lax.broadcasted_iota(jnp.int32, (c, c), 1)
    tri = (s >= r) if reverse else (s <= r)
    cum = jnp.dot(tri.astype(F32), g, precision=lax.Precision.HIGHEST, preferred_element_type=F32)
    i_last = 0 if reverse else c - 1
    i_mid = c // 2 if reverse else c // 2 - 1
    b_last = cum[i_last:i_last + 1, :]
    b_mid = cum[i_mid:i_mid + 1, :]
    o = None
    if need_o:
        qc = q * dk ** -0.5
        qm = (qc * jnp.exp(cum - b_mid)).astype(BF16)
        km = (k * jnp.exp(b_mid - cum)).astype(BF16)
        a = jnp.where(tri, _bdot_nt(qm, km), 0.0)
        o = _bdot(a.astype(BF16), v) + _bdot((qc * jnp.exp(cum)).astype(BF16), state.astype(BF16))
    kdec_t = (k * jnp.exp(b_last - cum)).T
    kv = _bdot(kdec_t.astype(BF16), v)
    decay = jnp.exp(cum.T[:, i_last:i_last + 1])
    return o, decay * state + kv


def _gla_kernel(*refs, reverse, final):
    if final:
        (kx_ref, vx_ref, gx_ref, q_ref, k_ref, v_ref, g_ref, ob_ref, r_ref, nw_ref, o_ref, st_ref) = refs
    else:
        (kx_ref, vx_ref, gx_ref, q_ref, k_ref, v_ref, g_ref, o_ref, st_ref) = refs
    t = pl.program_id(2)
    c = GLA_CHUNK

    def order(n):
        ids = range(n // c)
        return reversed(ids) if reverse else ids

    @pl.when(t == 0)
    def _context():
        state = jnp.zeros(st_ref.shape, F32)
        for i in order(kx_ref.shape[0]):
            rs = slice(i * c, (i + 1) * c)
            _, state = _gla_chunk(None, kx_ref[rs, :], vx_ref[rs, :], gx_ref[rs, :], state,
                                  reverse=reverse, need_o=False)
        st_ref[...] = state

    @pl.when(t > 0)
    def _latent():
        state = st_ref[...]
        for i in order(k_ref.shape[0]):
            rs = slice(i * c, (i + 1) * c)
            o, state = _gla_chunk(q_ref[rs, :], k_ref[rs, :], v_ref[rs, :], g_ref[rs, :], state,
                                  reverse=reverse, need_o=True)
            if final:
                o = o + ob_ref[rs, :]
                o = o * lax.rsqrt(jnp.mean(o * o, axis=-1, keepdims=True) + LN_EPS)
                o = o * nw_ref[...]
                o = o * _silu(r_ref[rs, :])
            o_ref[rs, :] = o.astype(o_ref.dtype)
        st_ref[...] = state


def _gla(kb_c, vb_c, la_c, qb, kb, vb, la, batch, seq, n_ctx, *, reverse, extra=None):
    assert n_ctx % GLA_CHUNK == 0 and GLA_STEP % GLA_CHUNK == 0
    tb = GLA_STEP
    nt = seq // tb
    dk = qb.shape[1] // B_HEADS
    dv = vb.shape[1] // B_HEADS
    final = extra is not None

    def lat(b, h, t):
        i = jnp.maximum(t - 1, 0)
        if reverse:
            i = nt - 1 - i
        return (b * nt + i, h)

    cx = lambda b, h, t: (b, h)
    in_specs = [pl.BlockSpec((n_ctx, dk), cx), pl.BlockSpec((n_ctx, dv), cx), pl.BlockSpec((n_ctx, dk), cx),
                pl.BlockSpec((tb, dk), lat), pl.BlockSpec((tb, dk), lat),
                pl.BlockSpec((tb, dv), lat), pl.BlockSpec((tb, dk), lat)]
    args = [kb_c, vb_c, la_c, qb, kb, vb, la]
    if final:
        o_other, rb, norm_w = extra
        in_specs += [pl.BlockSpec((tb, dv), lat), pl.BlockSpec((tb, dv), lat),
                     pl.BlockSpec((1, dv), lambda b, h, t: (0, h))]
        args += [o_other, rb, norm_w]
    return pl.pallas_call(
        functools.partial(_gla_kernel, reverse=reverse, final=final),
        grid=(batch, B_HEADS, nt + 1),
        in_specs=in_specs,
        out_specs=pl.BlockSpec((tb, dv), lat),
        out_shape=jax.ShapeDtypeStruct((batch * seq, vb.shape[1]), BF16 if final else F32),
        scratch_shapes=[pltpu.VMEM((dk, dv), F32)],
        compiler_params=_params(("parallel", "parallel", "arbitrary")),
        name="gla_fwd_out" if final else "gla_bwd",
    )(*args)


def _outproj_kernel(oa_ref, ob_ref, x_ref, mod_ref, wt_ref, wb_ref, lng_ref, lnb_ref, wr_ref, br_ref,
                    x1_o, h2_o, rw_o, ri_o):
    y = _bdot(oa_ref[...], wt_ref[...]) + _bdot(ob_ref[...], wb_ref[...])
    x1 = _ln(ALPHA * x_ref[...] + mod_ref[2:3, :] * y) * lng_ref[...] + lnb_ref[...]
    x1_o[...] = x1
    h2 = _ln(x1) * (1.0 + mod_ref[4:5, :]) + mod_ref[3:4, :]
    h2_o[...] = h2
    lg = _bdot(h2.astype(BF16), wr_ref[...]) + br_ref[...]
    lane = lax.broadcasted_iota(jnp.int32, lg.shape, 1)
    lanef = lane.astype(F32)
    big = float(LANES)
    is_g = (lane >= N_EXPERTS) & (lane < N_EXPERTS + N_GROUPS)
    gl = jnp.where(is_g, lg, -jnp.inf)
    gmax = jnp.max(gl, axis=-1, keepdims=True)
    pg_top = 1.0 / jnp.sum(jnp.exp(gl - gmax), axis=-1, keepdims=True)
    grp = jnp.min(jnp.where(gl == gmax, lanef, big), axis=-1, keepdims=True) - N_EXPERTS
    in_grp = (lane < N_EXPERTS) & ((lane // EXPERTS_PER_GROUP).astype(F32) == grp)
    el = jnp.where(in_grp, lg, -jnp.inf)
    m1 = jnp.max(el, axis=-1, keepdims=True)
    i1 = jnp.min(jnp.where(el == m1, lanef, big), axis=-1, keepdims=True)
    el2 = jnp.where(lanef == i1, -jnp.inf, el)
    m2 = jnp.max(el2, axis=-1, keepdims=True)
    i2 = jnp.min(jnp.where(el2 == m2, lanef, big), axis=-1, keepdims=True)
    e2 = jnp.exp(m2 - m1)
    w1 = pg_top / (1.0 + e2)
    w2 = pg_top * e2 / (1.0 + e2)
    rw_o[...] = jnp.where(lane == 0, w1, jnp.where(lane == 1, w2, 0.0))
    ri_o[...] = jnp.where(lane == 0, i1, jnp.where(lane == 1, i2, 0.0)).astype(jnp.int32)


def _outproj(out_a, out_b, xf, mod3, w_top, w_bot, ln_g, ln_b, wr, br, tiles_per_seq):
    r, d = xf.shape
    tm = ROW_TILE
    row = lambda n: pl.BlockSpec((tm, n), lambda i: (i, 0))
    return pl.pallas_call(
        _outproj_kernel,
        grid=(r // tm,),
        in_specs=[row(out_a.shape[1]), row(out_b.shape[1]), row(d),
                  pl.BlockSpec((None, 6, d), lambda i: (i // tiles_per_seq, 0, 0)),
                  _resident(w_top.shape), _resident(w_bot.shape),
                  _resident((1, d)), _resident((1, d)), _resident(wr.shape), _resident(br.shape)],
        out_specs=[row(d), row(d), row(LANES), row(LANES)],
        out_shape=[jax.ShapeDtypeStruct((r, d), F32), jax.ShapeDtypeStruct((r, d), F32),
                   jax.ShapeDtypeStruct((r, LANES), F32), jax.ShapeDtypeStruct((r, LANES), jnp.int32)],
        compiler_params=_params(("parallel",)),
        name="out_proj_router",
    )(out_a, out_b, xf, mod3, w_top, w_bot, ln_g.reshape(1, d), ln_b.reshape(1, d), wr, br)


def _row_copy(src_ref, src_row, dst_ref, dst_row, sem):
    return pltpu.make_async_copy(src_ref.at[pl.ds(src_row, 1)], dst_ref.at[pl.ds(dst_row, 1)], sem)


def _dispatch_kernel(dest_ref, h_ref, buf_in_ref, buf_ref, sem):
    del buf_in_ref
    tm = h_ref.shape[0]

    def issue(i, carry):
        for k in range(TOP_K):
            _row_copy(h_ref, i, buf_ref, dest_ref[0, 0, TOP_K * i + k], sem).start()
        return carry

    lax.fori_loop(0, tm, issue, 0)
    for _ in range(TOP_K):
        pltpu.make_async_copy(h_ref, buf_ref.at[pl.ds(0, tm)], sem).wait()


def _dispatch(dest3, h2, n_rows):
    r, d = h2.shape
    tm = ROW_TILE
    buf0 = jnp.zeros((n_rows, d), h2.dtype)
    return pl.pallas_call(
        _dispatch_kernel,
        grid=(r // tm,),
        in_specs=[pl.BlockSpec((1, 1, TOP_K * tm), lambda i: (i, 0, 0), memory_space=pltpu.SMEM),
                  pl.BlockSpec((tm, d), lambda i: (i, 0)),
                  pl.BlockSpec(memory_space=pl.ANY)],
        out_specs=pl.BlockSpec(memory_space=pl.ANY),
        out_shape=jax.ShapeDtypeStruct((n_rows, d), h2.dtype),
        scratch_shapes=[pltpu.SemaphoreType.DMA],
        input_output_aliases={2: 0},
        compiler_params=_params(("arbitrary",)),
        name="moe_dispatch",
    )(dest3, h2, buf0)


def _expert_kernel(be_ref, nu_ref, x_ref, wg_ref, wu_ref, wd_ref, o_ref):
    del be_ref

    @pl.when(pl.program_id(0) < nu_ref[0])
    def _():
        xb = x_ref[...].astype(BF16)
        for c0 in range(0, wg_ref.shape[1], HID_CHUNK):
            cs = slice(c0, c0 + HID_CHUNK)
            a = _silu(_bdot(xb, wg_ref[:, cs])) * _bdot(xb, wu_ref[:, cs])
            part = _bdot(a.astype(BF16), wd_ref[cs, :])
            if c0 == 0:
                o_ref[...] = part
            else:
                o_ref[...] += part

    @pl.when(pl.program_id(0) >= nu_ref[0])
    def _():
        o_ref[...] = jnp.zeros(o_ref.shape, F32)


def _experts(block_e, n_used, buf, wg, wu, wd):
    n_rows, d = buf.shape
    bm = MOE_BLOCK
    hid = wg.shape[2]
    return pl.pallas_call(
        _expert_kernel,
        grid_spec=pltpu.PrefetchScalarGridSpec(
            num_scalar_prefetch=2,
            grid=(n_rows // bm,),
            in_specs=[pl.BlockSpec((bm, d), lambda i, be, nu: (i, 0)),
                      pl.BlockSpec((None, d, hid), lambda i, be, nu: (be[i], 0, 0)),
                      pl.BlockSpec((None, d, hid), lambda i, be, nu: (be[i], 0, 0)),
                      pl.BlockSpec((None, hid, d), lambda i, be, nu: (be[i], 0, 0))],
            out_specs=pl.BlockSpec((bm, d), lambda i, be, nu: (i, 0)),
        ),
        out_shape=jax.ShapeDtypeStruct((n_rows, d), F32),
        compiler_params=_params(("arbitrary",)),
        name="moe_experts",
    )(block_e, n_used, buf, wg, wu, wd)


def _combine_kernel(dest_ref, y_ref, rw_ref, x1_ref, mod_ref, lng_ref, lnb_ref, o_ref, g_ref, sem):
    tm = x1_ref.shape[0]

    def issue(i, carry):
        for k in range(TOP_K):
            _row_copy(y_ref, dest_ref[0, 0, TOP_K * i + k], g_ref.at[k], i, sem).start()
        return carry

    lax.fori_loop(0, tm, issue, 0)
    for k in range(TOP_K):
        pltpu.make_async_copy(y_ref.at[pl.ds(0, tm)], g_ref.at[k], sem).wait()
    f = rw_ref[:, 0:1] * g_ref[0] + rw_ref[:, 1:2] * g_ref[1]
    o_ref[...] = _ln(ALPHA * x1_ref[...] + mod_ref[5:6, :] * f) * lng_ref[...] + lnb_ref[...]


def _combine(dest3, y, rw, x1, mod3, ln_g, ln_b, tiles_per_seq):
    r, d = x1.shape
    tm = ROW_TILE
    row = lambda n: pl.BlockSpec((tm, n), lambda i: (i, 0))
    return pl.pallas_call(
        _combine_kernel,
        grid=(r // tm,),
        in_specs=[pl.BlockSpec((1, 1, TOP_K * tm), lambda i: (i, 0, 0), memory_space=pltpu.SMEM),
                  pl.BlockSpec(memory_space=pl.ANY),
                  row(LANES), row(d),
                  pl.BlockSpec((None, 6, d), lambda i: (i // tiles_per_seq, 0, 0)),
                  _resident((1, d)), _resident((1, d))],
        out_specs=row(d),
        out_shape=jax.ShapeDtypeStruct((r, d), F32),
        scratch_shapes=[pltpu.VMEM((TOP_K, tm, d), F32), pltpu.SemaphoreType.DMA],
        compiler_params=_params(("arbitrary",)),
        name="moe_combine_ln",
    )(dest3, y, rw, x1, mod3, ln_g.reshape(1, d), ln_b.reshape(1, d))


def _routing_plan(ri, n_tokens):
    eid = ri[:, :TOP_K].reshape(-1)
    m = n_tokens * TOP_K
    onehot = (eid[:, None] == jnp.arange(N_EXPERTS, dtype=jnp.int32)[None, :]).astype(jnp.int32)
    csum = jnp.cumsum(onehot, axis=0)
    counts = csum[-1]
    rank = jnp.sum(csum * onehot, axis=1) - 1
    padded = (counts + MOE_BLOCK - 1) // MOE_BLOCK * MOE_BLOCK
    pad_end = jnp.cumsum(padded)
    pad_start = pad_end - padded
    dest = jnp.sum(onehot * pad_start[None, :], axis=1) + rank
    n_blocks = m // MOE_BLOCK + N_EXPERTS
    block_row = jnp.arange(n_blocks, dtype=jnp.int32) * MOE_BLOCK
    block_e = jnp.minimum(jnp.sum(block_row[:, None] >= pad_end[None, :], axis=1), N_EXPERTS - 1)
    n_used = (pad_end[-1] // MOE_BLOCK).reshape(1)
    return dest.astype(jnp.int32), block_e.astype(jnp.int32), n_used.astype(jnp.int32), n_blocks


def _rope_tables(seq):
    n_freq = HEAD_DIM // 4
    inv_freq = ROPE_THETA ** (-jnp.arange(n_freq, dtype=F32) / n_freq)
    rows = seq // GRID_W
    row = jnp.repeat(jnp.arange(rows, dtype=F32), GRID_W)
    col = jnp.tile(jnp.arange(GRID_W, dtype=F32), rows)
    ar, ac = row[:, None] * inv_freq, col[:, None] * inv_freq
    zero = jnp.zeros_like(ar)
    cos_t = jnp.concatenate([jnp.cos(ar), jnp.cos(ar), jnp.cos(ac), jnp.cos(ac)], axis=1)
    sin_a = jnp.concatenate([-jnp.sin(ar), zero, -jnp.sin(ac), zero], axis=1)
    sin_b = jnp.concatenate([zero, jnp.sin(ar), zero, jnp.sin(ac)], axis=1)
    return cos_t, sin_a, sin_b


def kernel(x, c, ctx, c_ctx, w_ada, b_ada, w_in, w_gate_up, b_gate, attn_sink, gla_norm_w, w_out, ln1_g, ln1_b, w_router_group, b_router_group, w_router_expert, b_router_expert, w_exp_gate, w_exp_up, w_exp_down, ln2_g, ln2_b):
    batch, seq, d = x.shape
    n_ctx = ctx.shape[1]
    assert w_ada.shape[0] == DEPTH and batch < MOD_ROWS
    assert seq % ROW_TILE == 0 and n_ctx % ROW_TILE == 0 and seq % GLA_STEP == 0 and n_ctx == GLA_STEP
    n_tok = batch * seq
    tiles_per_seq = seq // ROW_TILE
    a_width = d // 2
    kv_width = a_width // A_GROUP
    b_width = d - a_width
    key_width = b_width // 2
    layer = 0

    cc = jnp.concatenate([c, c_ctx[None, :], jnp.zeros((MOD_ROWS - batch - 1, d), F32)], axis=0)
    mod3 = _adaln(cc, w_ada[layer], b_ada[layer]).reshape(MOD_ROWS, 6, d)

    splits = (a_width, kv_width, kv_width, key_width, key_width, b_width, b_width, 2 * GATE_RANK)
    w_in_b = w_in[layer].astype(BF16)
    weights, c0 = [], 0
    for n in splits:
        weights.append(w_in_b[:, c0:c0 + n])
        c0 += n
    zero_up = jnp.zeros((GATE_RANK, key_width), F32)
    wup2 = jnp.concatenate([jnp.concatenate([w_gate_up[layer, 0], zero_up], axis=1),
                            jnp.concatenate([zero_up, w_gate_up[layer, 1]], axis=1)], axis=0).astype(BF16)
    bg2 = b_gate[layer].reshape(1, 2 * key_width)
    tables = _rope_tables(seq)

    xf = x.reshape(n_tok, d)
    qa, ka, va, qb, kb, vb, rb, la_f, la_b = _project(
        xf, mod3, lambda i: i // tiles_per_seq, tables, weights, wup2, bg2, rope=True, tiles_per_seq=tiles_per_seq)
    _, ka_c, va_c, _, kb_c, vb_c, _, lac_f, lac_b = _project(
        ctx.reshape(batch * n_ctx, d), mod3, lambda i: batch, tables, weights, wup2, bg2, rope=False,
        tiles_per_seq=tiles_per_seq)

    out_a = _attention(attn_sink[layer], qa, ka, va, ka_c, va_c, batch, seq, n_ctx)
    o_b = _gla(kb_c, vb_c, lac_b, qb, kb, vb, la_b, batch, seq, n_ctx, reverse=True)
    out_b = _gla(kb_c, vb_c, lac_f, qb, kb, vb, la_f, batch, seq, n_ctx, reverse=False,
                 extra=(o_b, rb, gla_norm_w[layer].reshape(1, b_width)))

    w_out_b = w_out[layer].astype(BF16)
    wr = jnp.concatenate([w_router_expert[layer], w_router_group[layer],
                          jnp.zeros((d, LANES - N_EXPERTS - N_GROUPS), F32)], axis=1).astype(BF16)
    br = jnp.concatenate([b_router_expert[layer], b_router_group[layer],
                          jnp.zeros((LANES - N_EXPERTS - N_GROUPS,), F32)]).reshape(1, LANES)
    x1, h2, rw, ri = _outproj(out_a, out_b, xf, mod3, w_out_b[:a_width], w_out_b[a_width:],
                              ln1_g[layer], ln1_b[layer], wr, br, tiles_per_seq)

    dest, block_e, n_used, n_blocks = _routing_plan(ri, n_tok)
    dest3 = dest.reshape(n_tok // ROW_TILE, 1, TOP_K * ROW_TILE)
    buf = _dispatch(dest3, h2, n_blocks * MOE_BLOCK)
    y = _experts(block_e, n_used, buf, w_exp_gate[layer].astype(BF16), w_exp_up[layer].astype(BF16),
                 w_exp_down[layer].astype(BF16))
    out = _combine(dest3, y, rw, x1, mod3, ln2_g[layer], ln2_b[layer], tiles_per_seq)
    return out.reshape(batch, seq, d)
```

```python
import functools

import jax
import jax.numpy as jnp
from jax import lax
from jax.experimental import pallas as pl
from jax.experimental.pallas import tpu as pltpu

F32 = jnp.float32
BF16 = jnp.bfloat16

HEAD_DIM = 128
GRID_W = 64
WINDOW = 128
A_BLOCK = 128
A_GROUP = 4
ROPE_THETA = 10000.0
B_HEADS = 4
GATE_RANK = 16
GATE_TAU = 16.0
GLA_CHUNK = 64
N_GROUPS = 4
EXPERTS_PER_GROUP = 8
N_EXPERTS = N_GROUPS * EXPERTS_PER_GROUP
TOP_K = 2
DEPTH = 1
ALPHA = (2.0 * DEPTH) ** 0.25
LN_EPS = 1e-6

LANES = 128
SUBLANES = 8
MOD_ROWS = 8
VMEM_LIMIT = 56 * 1024 * 1024

ROW_TILE = 256
GLA_STEP = 256
MOE_CHUNK = 1024
MOE_BLOCK = 256
HID_TILE = 256


def _params(sem):
    return pltpu.CompilerParams(dimension_semantics=sem, vmem_limit_bytes=VMEM_LIMIT)


def _resident(shape):
    nd = len(shape)
    return pl.BlockSpec(shape, lambda *_: (0,) * nd, pipeline_mode=pl.Buffered(1))


def _ln(x):
    mu = jnp.mean(x, axis=-1, keepdims=True)
    xc = x - mu
    var = jnp.mean(xc * xc, axis=-1, keepdims=True)
    return xc * lax.rsqrt(var + LN_EPS)


def _silu(x):
    return x * jax.nn.sigmoid(x)


def _bdot(a, b):
    return jnp.dot(a, b, preferred_element_type=F32)


def _bdot_nt(a, b):
    return lax.dot_general(a, b, (((1,), (1,)), ((), ())), preferred_element_type=F32)


def _adaln_kernel(c_ref, w_ref, b_ref, o_ref):
    s = _silu(c_ref[...]).astype(BF16)
    o_ref[...] = _bdot(s, w_ref[...].astype(BF16)) + b_ref[...]


def _adaln(cc, w_ada, b_ada):
    d, n = w_ada.shape
    tn = 1024
    return pl.pallas_call(
        _adaln_kernel,
        grid=(n // tn,),
        in_specs=[pl.BlockSpec((MOD_ROWS, d), lambda j: (0, 0)),
                  pl.BlockSpec((d, tn), lambda j: (0, j)),
                  pl.BlockSpec((1, tn), lambda j: (0, j))],
        out_specs=pl.BlockSpec((MOD_ROWS, tn), lambda j: (0, j)),
        out_shape=jax.ShapeDtypeStruct((MOD_ROWS, n), F32),
        compiler_params=_params(("arbitrary",)),
        name="adaln",
    )(cc, w_ada, b_ada.reshape(1, n))


def _proj_kernel(x_ref, mod_ref, cos_ref, sina_ref, sinb_ref,
                 wqa, wka, wva, wqb, wkb, wvb, wrb, wg, wup, bg,
                 qa_o, ka_o, va_o, qb_o, kb_o, vb_o, rb_o, laf_o, lab_o, *, rope):
    h = _ln(x_ref[...]) * (1.0 + mod_ref[1:2, :]) + mod_ref[0:1, :]
    hb = h.astype(BF16)

    def rot(t):
        return (t * cos_ref[...] + pltpu.roll(t, 96, 1) * sina_ref[...]
                + pltpu.roll(t, 32, 1) * sinb_ref[...])

    def project(w_ref, o_ref, rotate):
        n = w_ref.shape[1]
        step = min(n, 2 * LANES)
        for c0 in range(0, n, step):
            t = _bdot(hb, w_ref[:, c0:c0 + step])
            if rotate:
                for l0 in range(0, step, LANES):
                    o_ref[:, c0 + l0:c0 + l0 + LANES] = rot(t[:, l0:l0 + LANES]).astype(o_ref.dtype)
            else:
                o_ref[:, c0:c0 + step] = t.astype(o_ref.dtype)

    project(wqa, qa_o, rope)
    project(wka, ka_o, rope)
    project(wva, va_o, False)
    project(wqb, qb_o, False)
    project(wkb, kb_o, False)
    project(wvb, vb_o, False)
    project(wrb, rb_o, False)
    gl = _bdot(hb, wg[...]).astype(BF16)
    z = _bdot(gl, wup[...]) + bg[...]
    la = (jnp.minimum(z, 0.0) - jnp.log1p(jnp.exp(-jnp.abs(z)))) / GATE_TAU
    kw = laf_o.shape[1]
    laf_o[...] = la[:, :kw]
    lab_o[...] = la[:, kw:]


def _project(xf, mod3, mod_row_of_tile, tables, weights, wup2, bg2, *, rope, tiles_per_seq):
    r, d = xf.shape
    tm = ROW_TILE
    wqa, wka, wva, wqb, wkb, wvb, wrb, wg = weights
    cos_t, sina_t, sinb_t = tables
    row = lambda n: pl.BlockSpec((tm, n), lambda i: (i, 0))
    tab = pl.BlockSpec((tm, LANES), lambda i: (i % tiles_per_seq, 0))
    outs = [(wqa.shape[1], BF16), (wka.shape[1], BF16), (wva.shape[1], BF16),
            (wqb.shape[1], F32), (wkb.shape[1], F32), (wvb.shape[1], BF16),
            (wrb.shape[1], F32), (wqb.shape[1], F32), (wqb.shape[1], F32)]
    return pl.pallas_call(
        functools.partial(_proj_kernel, rope=rope),
        grid=(r // tm,),
        in_specs=[row(d),
                  pl.BlockSpec((None, 6, d), lambda i: (mod_row_of_tile(i), 0, 0)),
                  tab, tab, tab]
                 + [_resident(w.shape) for w in weights]
                 + [_resident(wup2.shape), _resident(bg2.shape)],
        out_specs=[row(n) for n, _ in outs],
        out_shape=[jax.ShapeDtypeStruct((r, n), dt) for n, dt in outs],
        compiler_params=_params(("parallel",)),
        name="in_proj_rope" if rope else "in_proj_ctx",
    )(xf, mod3, cos_t, sina_t, sinb_t, *weights, wup2, bg2)


def _attn_kernel(sink_ref, q_ref, kp_ref, kc_ref, kn_ref, vp_ref, vc_ref, vn_ref, kx_ref, vx_ref, o_ref):
    n = pl.program_id(1)
    nb = pl.num_programs(1)
    blk = A_BLOCK
    rows = A_GROUP * blk
    scale = HEAD_DIM ** -0.5
    n_ctx = kx_ref.shape[0]
    qi = lax.broadcasted_iota(jnp.int32, (rows, blk), 0) % blk
    kj = lax.broadcasted_iota(jnp.int32, (rows, blk), 1)
    ok_prev = (kj >= qi) & (n > 0)
    ok_next = (kj <= qi) & (n < nb - 1)
    for hk in range(kp_ref.shape[1] // HEAD_DIM):
        ks = slice(hk * HEAD_DIM, (hk + 1) * HEAD_DIM)
        q4 = jnp.concatenate(
            [q_ref[:, (hk * A_GROUP + g) * HEAD_DIM:(hk * A_GROUP + g + 1) * HEAD_DIM] for g in range(A_GROUP)],
            axis=0)
        s_ctx = _bdot_nt(q4, kx_ref[:, ks]) * scale
        s_prev = jnp.where(ok_prev, _bdot_nt(q4, kp_ref[:, ks]) * scale, -jnp.inf)
        s_cur = _bdot_nt(q4, kc_ref[:, ks]) * scale
        s_next = jnp.where(ok_next, _bdot_nt(q4, kn_ref[:, ks]) * scale, -jnp.inf)
        sink = jnp.concatenate(
            [jnp.full((blk, 1), sink_ref[hk * A_GROUP + g], F32) for g in range(A_GROUP)], axis=0)
        m = jnp.maximum(jnp.max(s_ctx, axis=-1, keepdims=True), jnp.max(s_cur, axis=-1, keepdims=True))
        m = jnp.maximum(m, jnp.max(s_prev, axis=-1, keepdims=True))
        m = jnp.maximum(m, jnp.max(s_next, axis=-1, keepdims=True))
        m = jnp.maximum(m, sink)
        e_ctx = jnp.exp(s_ctx - m)
        e_prev = jnp.exp(s_prev - m)
        e_cur = jnp.exp(s_cur - m)
        e_next = jnp.exp(s_next - m)
        den = (jnp.exp(sink - m) + jnp.sum(e_ctx, axis=-1, keepdims=True)
               + jnp.sum(e_prev, axis=-1, keepdims=True) + jnp.sum(e_cur, axis=-1, keepdims=True)
               + jnp.sum(e_next, axis=-1, keepdims=True))
        inv = 1.0 / den
        o = (_bdot((e_ctx * inv).astype(BF16), vx_ref[:, ks])
             + (_bdot((e_prev * inv).astype(BF16), vp_ref[:, ks])
                + _bdot((e_cur * inv).astype(BF16), vc_ref[:, ks])
                + _bdot((e_next * inv).astype(BF16), vn_ref[:, ks])))
        for g in range(A_GROUP):
            hq = hk * A_GROUP + g
            o_ref[:, hq * HEAD_DIM:(hq + 1) * HEAD_DIM] = o[g * blk:(g + 1) * blk, :].astype(o_ref.dtype)


def _attention(sink, qa, ka, va, ka_c, va_c, batch, seq, n_ctx):
    nb = seq // A_BLOCK
    aw = qa.shape[1]
    kvw = ka.shape[1]
    blk = A_BLOCK
    prev = lambda b, n: (b * nb + jnp.maximum(n - 1, 0), 0)
    cur = lambda b, n: (b * nb + n, 0)
    nxt = lambda b, n: (b * nb + jnp.minimum(n + 1, nb - 1), 0)
    kv = lambda f: pl.BlockSpec((blk, kvw), f)
    ctx = pl.BlockSpec((n_ctx, kvw), lambda b, n: (b, 0))
    return pl.pallas_call(
        _attn_kernel,
        grid=(batch, nb),
        in_specs=[pl.BlockSpec(memory_space=pltpu.SMEM),
                  pl.BlockSpec((blk, aw), cur),
                  kv(prev), kv(cur), kv(nxt), kv(prev), kv(cur), kv(nxt), ctx, ctx],
        out_specs=pl.BlockSpec((blk, aw), cur),
        out_shape=jax.ShapeDtypeStruct((batch * seq, aw), BF16),
        compiler_params=_params(("parallel", "parallel")),
        name="window_gqa",
    )(sink, qa, ka, ka, ka, va, va, va, ka_c, va_c)


def _chunk_cumsum(g, *, reverse):
    rows = g.shape[0]
    p = lax.broadcasted_iota(jnp.int32, g.shape, 0) % GLA_CHUNK
    s = 1
    while s < GLA_CHUNK:
        if reverse:
            g = g + jnp.where(p < GLA_CHUNK - s, pltpu.roll(g, rows - s, 0), 0.0)
        else:
            g = g + jnp.where(p >= s, pltpu.roll(g, s, 0), 0.0)
        s *= 2
    return g


def _per_chunk_row(x, i):
    c = GLA_CHUNK
    return jnp.concatenate(
        [jnp.broadcast_to(x[j * c + i:j * c + i + 1, :], (c, x.shape[1])) for j in range(x.shape[0] // c)], axis=0)


def _gla_block(q, k, v, cum, state_t, mask, *, reverse, need_o):
    c = GLA_CHUNK
    rows, dk = k.shape
    n_chunks = rows // c
    i_last = 0 if reverse else c - 1
    i_mid = c // 2 if reverse else c // 2 - 1
    b_last = _per_chunk_row(cum, i_last)
    kdec = (k * jnp.exp(b_last - cum)).astype(BF16)
    if need_o:
        b_mid = _per_chunk_row(cum, i_mid)
        qc = q * dk ** -0.5
        qm = (qc * jnp.exp(cum - b_mid)).astype(BF16)
        km = (k * jnp.exp(b_mid - cum)).astype(BF16)
        a = jnp.where(mask, _bdot_nt(qm, km), 0.0)
        o_intra = _bdot(a.astype(BF16), v)
        qe = (qc * jnp.exp(cum)).astype(BF16)
    o_inter = [None] * n_chunks
    for j in (reversed(range(n_chunks)) if reverse else range(n_chunks)):
        rs = slice(j * c, (j + 1) * c)
        if need_o:
            o_inter[j] = _bdot_nt(qe[rs, :], state_t.astype(BF16))
        kv_t = lax.dot_general(v[rs, :], kdec[rs, :], (((0,), (0,)), ((), ())), preferred_element_type=F32)
        state_t = state_t * jnp.exp(cum[j * c + i_last:j * c + i_last + 1, :]) + kv_t
    o = o_intra + jnp.concatenate(o_inter, axis=0) if need_o else None
    return o, state_t


def _gla_kernel(*refs, reverse, final):
    if final:
        (kx_ref, vx_ref, gx_ref, q_ref, k_ref, v_ref, g_ref, ob_ref, r_ref, nw_ref, o_ref, st_ref) = refs
    else:
        (kx_ref, vx_ref, gx_ref, q_ref, k_ref, v_ref, g_ref, o_ref, st_ref) = refs
    t = pl.program_id(1)
    n_heads = st_ref.shape[0]
    dv, dk = st_ref.shape[1:]

    @pl.when(t == 0)
    def _context():
        cum = _chunk_cumsum(gx_ref[...], reverse=reverse)
        for h in range(n_heads):
            ks, vs = slice(h * dk, (h + 1) * dk), slice(h * dv, (h + 1) * dv)
            _, st = _gla_block(None, kx_ref[:, ks], vx_ref[:, vs], cum[:, ks], jnp.zeros((dv, dk), F32), None,
                               reverse=reverse, need_o=False)
            st_ref[h] = st

    @pl.when(t > 0)
    def _latent():
        rows = k_ref.shape[0]
        cum = _chunk_cumsum(g_ref[...], reverse=reverse)
        r = lax.broadcasted_iota(jnp.int32, (rows, rows), 0)
        s = lax.broadcasted_iota(jnp.int32, (rows, rows), 1)
        causal = (s >= r) if reverse else (s <= r)
        mask = causal & ((r // GLA_CHUNK) == (s // GLA_CHUNK))
        for h in range(n_heads):
            ks, vs = slice(h * dk, (h + 1) * dk), slice(h * dv, (h + 1) * dv)
            o, st = _gla_block(q_ref[:, ks], k_ref[:, ks], v_ref[:, vs], cum[:, ks], st_ref[h], mask,
                               reverse=reverse, need_o=True)
            st_ref[h] = st
            if final:
                o = o + ob_ref[:, vs]
                o = o * lax.rsqrt(jnp.mean(o * o, axis=-1, keepdims=True) + LN_EPS)
                o = o * nw_ref[:, vs]
                o = o * _silu(r_ref[:, vs])
            o_ref[:, vs] = o.astype(o_ref.dtype)


def _gla(kb_c, vb_c, la_c, qb, kb, vb, la, batch, seq, n_ctx, *, reverse, extra=None):
    assert n_ctx % GLA_CHUNK == 0 and GLA_STEP % GLA_CHUNK == 0
    tb = GLA_STEP
    nt = seq // tb
    kw, vw = qb.shape[1], vb.shape[1]
    final = extra is not None

    def lat(b, t):
        i = jnp.maximum(t - 1, 0)
        if reverse:
            i = nt - 1 - i
        return (b * nt + i, 0)

    cx = lambda b, t: (b, 0)
    in_specs = [pl.BlockSpec((n_ctx, kw), cx), pl.BlockSpec((n_ctx, vw), cx), pl.BlockSpec((n_ctx, kw), cx),
                pl.BlockSpec((tb, kw), lat), pl.BlockSpec((tb, kw), lat),
                pl.BlockSpec((tb, vw), lat), pl.BlockSpec((tb, kw), lat)]
    args = [kb_c, vb_c, la_c, qb, kb, vb, la]
    if final:
        o_other, rb, norm_w = extra
        in_specs += [pl.BlockSpec((tb, vw), lat), pl.BlockSpec((tb, vw), lat), _resident((1, vw))]
        args += [o_other, rb, norm_w]
    return pl.pallas_call(
        functools.partial(_gla_kernel, reverse=reverse, final=final),
        grid=(batch, nt + 1),
        in_specs=in_specs,
        out_specs=pl.BlockSpec((tb, vw), lat),
        out_shape=jax.ShapeDtypeStruct((batch * seq, vw), BF16 if final else F32),
        scratch_shapes=[pltpu.VMEM((B_HEADS, vw // B_HEADS, kw // B_HEADS), F32)],
        compiler_params=_params(("parallel", "arbitrary")),
        name="gla_fwd_out" if final else "gla_bwd",
    )(*args)


def _outproj_kernel(oa_ref, ob_ref, x_ref, mod_ref, wt_ref, wb_ref, lng_ref, lnb_ref, wr_ref, br_ref,
                    x1_o, h2_o, rw_o, ri_o):
    y = _bdot(oa_ref[...], wt_ref[...]) + _bdot(ob_ref[...], wb_ref[...])
    x1 = _ln(ALPHA * x_ref[...] + mod_ref[2:3, :] * y) * lng_ref[...] + lnb_ref[...]
    x1_o[...] = x1
    h2 = _ln(x1) * (1.0 + mod_ref[4:5, :]) + mod_ref[3:4, :]
    h2_o[...] = h2
    lg = _bdot(h2.astype(BF16), wr_ref[...]) + br_ref[...]
    lane = lax.broadcasted_iota(jnp.int32, lg.shape, 1)
    lanef = lane.astype(F32)
    big = float(LANES)
    is_g = (lane >= N_EXPERTS) & (lane < N_EXPERTS + N_GROUPS)
    gl = jnp.where(is_g, lg, -jnp.inf)
    gmax = jnp.max(gl, axis=-1, keepdims=True)
    pg_top = 1.0 / jnp.sum(jnp.exp(gl - gmax), axis=-1, keepdims=True)
    grp = jnp.min(jnp.where(gl == gmax, lanef, big), axis=-1, keepdims=True) - N_EXPERTS
    in_grp = (lane < N_EXPERTS) & ((lane // EXPERTS_PER_GROUP).astype(F32) == grp)
    el = jnp.where(in_grp, lg, -jnp.inf)
    m1 = jnp.max(el, axis=-1, keepdims=True)
    i1 = jnp.min(jnp.where(el == m1, lanef, big), axis=-1, keepdims=True)
    el2 = jnp.where(lanef == i1, -jnp.inf, el)
    m2 = jnp.max(el2, axis=-1, keepdims=True)
    i2 = jnp.min(jnp.where(el2 == m2, lanef, big), axis=-1, keepdims=True)
    e2 = jnp.exp(m2 - m1)
    w1 = pg_top / (1.0 + e2)
    w2 = pg_top * e2 / (1.0 + e2)
    rw_o[...] = jnp.where(lane == 0, w1, jnp.where(lane == 1, w2, 0.0))
    ri_o[...] = jnp.where(lane == 0, i1, jnp.where(lane == 1, i2, 0.0)).astype(jnp.int32)


def _outproj(out_a, out_b, xf, mod3, w_top, w_bot, ln_g, ln_b, wr, br, tiles_per_seq):
    r, d = xf.shape
    tm = ROW_TILE
    row = lambda n: pl.BlockSpec((tm, n), lambda i: (i, 0))
    return pl.pallas_call(
        _outproj_kernel,
        grid=(r // tm,),
        in_specs=[row(out_a.shape[1]), row(out_b.shape[1]), row(d),
                  pl.BlockSpec((None, 6, d), lambda i: (i // tiles_per_seq, 0, 0)),
                  _resident(w_top.shape), _resident(w_bot.shape),
                  _resident((1, d)), _resident((1, d)), _resident(wr.shape), _resident(br.shape)],
        out_specs=[row(d), row(d), row(LANES), row(LANES)],
        out_shape=[jax.ShapeDtypeStruct((r, d), F32), jax.ShapeDtypeStruct((r, d), F32),
                   jax.ShapeDtypeStruct((r, LANES), F32), jax.ShapeDtypeStruct((r, LANES), jnp.int32)],
        compiler_params=_params(("parallel",)),
        name="out_proj_router",
    )(out_a, out_b, xf, mod3, w_top, w_bot, ln_g.reshape(1, d), ln_b.reshape(1, d), wr, br)


def _row_copy(src_ref, src_row, dst_ref, dst_row, sem):
    return pltpu.make_async_copy(src_ref.at[pl.ds(src_row, 1)], dst_ref.at[pl.ds(dst_row, 1)], sem)


def _dispatch_kernel(dest_ref, tail_ref, empty_ref, h_ref, buf_ref, zero_ref, sem, zsem):
    tm = h_ref.shape[0]
    step = pl.program_id(0)

    def zero_block(b):
        row = pl.multiple_of(b * MOE_BLOCK, MOE_BLOCK)
        return pltpu.make_async_copy(zero_ref, buf_ref.at[pl.ds(row, MOE_BLOCK)], zsem)

    def for_empty_blocks(fn):
        def body(b, carry):
            @pl.when(empty_ref[b] != 0)
            def _():
                fn(b)
            return carry
        lax.fori_loop(0, empty_ref.shape[0], body, 0)

    @pl.when(step == 0)
    def _zero_fill():
        zero_ref[...] = jnp.zeros(zero_ref.shape, zero_ref.dtype)
        for e in range(N_EXPERTS):
            start = pl.multiple_of(tail_ref[e] // SUBLANES * SUBLANES, SUBLANES)
            pltpu.make_async_copy(zero_ref, buf_ref.at[pl.ds(start, MOE_BLOCK)], sem).start()
        for e in range(N_EXPERTS):
            pltpu.make_async_copy(zero_ref, buf_ref.at[pl.ds(0, MOE_BLOCK)], sem).wait()
        for_empty_blocks(lambda b: zero_block(b).start())

    @pl.when(step == pl.num_programs(0) - 1)
    def _zero_done():
        for_empty_blocks(lambda b: zero_block(b).wait())

    def issue(i, carry):
        for k in range(TOP_K):
            _row_copy(h_ref, i, buf_ref, dest_ref[0, 0, TOP_K * i + k], sem).start()
        return carry

    lax.fori_loop(0, tm, issue, 0)
    for _ in range(TOP_K):
        pltpu.make_async_copy(h_ref, buf_ref.at[pl.ds(0, tm)], sem).wait()


def _dispatch(dest3, tail_row, empty_block, h2):
    r, d = h2.shape
    tm = ROW_TILE
    n_rows = empty_block.shape[0] * MOE_BLOCK
    return pl.pallas_call(
        _dispatch_kernel,
        grid=(r // tm,),
        in_specs=[pl.BlockSpec((1, 1, TOP_K * tm), lambda i: (i, 0, 0), memory_space=pltpu.SMEM),
                  pl.BlockSpec(memory_space=pltpu.SMEM),
                  pl.BlockSpec(memory_space=pltpu.SMEM),
                  pl.BlockSpec((tm, d), lambda i: (i, 0))],
        out_specs=pl.BlockSpec(memory_space=pl.ANY),
        out_shape=jax.ShapeDtypeStruct((n_rows, d), h2.dtype),
        scratch_shapes=[pltpu.VMEM((MOE_BLOCK, d), h2.dtype), pltpu.SemaphoreType.DMA,
                        pltpu.SemaphoreType.DMA],
        compiler_params=_params(("arbitrary",)),
        name="moe_dispatch",
    )(dest3, tail_row, empty_block, h2)


def _expert_kernel(ce_ref, cv_ref, nu_ref, x_ref, wg_ref, wu_ref, wd_ref, o_ref, wgb, wub, wdb):
    del ce_ref
    c = pl.program_id(0)
    j = pl.program_id(1)
    valid = cv_ref[c]

    @pl.when(valid > 0)
    def _cast_weights():
        wgb[...] = wg_ref[...].astype(BF16)
        wub[...] = wu_ref[...].astype(BF16)
        wdb[...] = wd_ref[...].astype(BF16)

    for s in range(MOE_CHUNK // MOE_BLOCK):
        rs = slice(s * MOE_BLOCK, (s + 1) * MOE_BLOCK)

        @pl.when(s * MOE_BLOCK < valid)
        def _compute():
            xb = x_ref[rs, :].astype(BF16)
            a = _silu(_bdot(xb, wgb[...])) * _bdot(xb, wub[...])
            part = _bdot(a.astype(BF16), wdb[...])

            @pl.when(j == 0)
            def _():
                o_ref[rs, :] = part

            @pl.when(j > 0)
            def _():
                o_ref[rs, :] += part

        @pl.when((s * MOE_BLOCK >= valid) & (j == 0))
        def _pad():
            o_ref[rs, :] = jnp.zeros((MOE_BLOCK, o_ref.shape[1]), F32)


def _experts(chunk_e, chunk_valid, n_used, buf, wg, wu, wd):
    d = buf.shape[1]
    n_chunks = chunk_e.shape[0]
    hid = wg.shape[2]
    tj = HID_TILE
    nj = hid // tj
    last = lambda c, nu: jnp.minimum(c, nu[0] - 1)
    jj = lambda c, j, nu: jnp.where(c < nu[0], j, nj - 1)
    return pl.pallas_call(
        _expert_kernel,
        grid_spec=pltpu.PrefetchScalarGridSpec(
            num_scalar_prefetch=3,
            grid=(n_chunks, nj),
            in_specs=[pl.BlockSpec((MOE_CHUNK, d), lambda c, j, ce, cv, nu: (last(c, nu), 0)),
                      pl.BlockSpec((None, d, tj), lambda c, j, ce, cv, nu: (ce[c], 0, jj(c, j, nu))),
                      pl.BlockSpec((None, d, tj), lambda c, j, ce, cv, nu: (ce[c], 0, jj(c, j, nu))),
                      pl.BlockSpec((None, tj, d), lambda c, j, ce, cv, nu: (ce[c], jj(c, j, nu), 0))],
            out_specs=pl.BlockSpec((MOE_CHUNK, d), lambda c, j, ce, cv, nu: (c, 0)),
            scratch_shapes=[pltpu.VMEM((d, tj), BF16), pltpu.VMEM((d, tj), BF16), pltpu.VMEM((tj, d), BF16)],
        ),
        out_shape=jax.ShapeDtypeStruct((n_chunks * MOE_CHUNK, d), F32),
        compiler_params=_params(("arbitrary", "arbitrary")),
        name="moe_experts",
    )(chunk_e, chunk_valid, n_used, buf, wg, wu, wd)


def _combine_kernel(dest_ref, y_ref, rw_ref, x1_ref, mod_ref, lng_ref, lnb_ref, o_ref, g_ref, sem):
    tm = x1_ref.shape[0]

    def issue(i, carry):
        for k in range(TOP_K):
            _row_copy(y_ref, dest_ref[0, 0, TOP_K * i + k], g_ref.at[k], i, sem).start()
        return carry

    lax.fori_loop(0, tm, issue, 0)
    for k in range(TOP_K):
        pltpu.make_async_copy(y_ref.at[pl.ds(0, tm)], g_ref.at[k], sem).wait()
    f = rw_ref[:, 0:1] * g_ref[0] + rw_ref[:, 1:2] * g_ref[1]
    o_ref[...] = _ln(ALPHA * x1_ref[...] + mod_ref[5:6, :] * f) * lng_ref[...] + lnb_ref[...]


def _combine(dest3, y, rw, x1, mod3, ln_g, ln_b, tiles_per_seq):
    r, d = x1.shape
    tm = ROW_TILE
    row = lambda n: pl.BlockSpec((tm, n), lambda i: (i, 0))
    return pl.pallas_call(
        _combine_kernel,
        grid=(r // tm,),
        in_specs=[pl.BlockSpec((1, 1, TOP_K * tm), lambda i: (i, 0, 0), memory_space=pltpu.SMEM),
                  pl.BlockSpec(memory_space=pl.ANY),
                  row(LANES), row(d),
                  pl.BlockSpec((None, 6, d), lambda i: (i // tiles_per_seq, 0, 0)),
                  _resident((1, d)), _resident((1, d))],
        out_specs=row(d),
        out_shape=jax.ShapeDtypeStruct((r, d), F32),
        scratch_shapes=[pltpu.VMEM((TOP_K, tm, d), F32), pltpu.SemaphoreType.DMA],
        compiler_params=_params(("arbitrary",)),
        name="moe_combine_ln",
    )(dest3, y, rw, x1, mod3, ln_g.reshape(1, d), ln_b.reshape(1, d))


def _routing_plan(ri, n_tokens):
    eid = ri[:, :TOP_K].reshape(-1)
    m = n_tokens * TOP_K
    onehot = (eid[:, None] == jnp.arange(N_EXPERTS, dtype=jnp.int32)[None, :]).astype(jnp.int32)
    csum = jnp.cumsum(onehot, axis=0)
    counts = csum[-1]
    rank = jnp.sum(csum * onehot, axis=1) - 1
    padded = (counts + MOE_CHUNK - 1) // MOE_CHUNK * MOE_CHUNK
    pad_end = jnp.cumsum(padded)
    pad_start = pad_end - padded
    dest = jnp.sum(onehot * pad_start[None, :], axis=1) + rank
    n_chunks = m // MOE_CHUNK + N_EXPERTS
    chunk_id = jnp.arange(n_chunks, dtype=jnp.int32)
    chunk_row = chunk_id * MOE_CHUNK
    n_used = pad_end[-1] // MOE_CHUNK
    raw_e = jnp.minimum(jnp.sum(chunk_row[:, None] >= pad_end[None, :], axis=1), N_EXPERTS - 1)
    used = chunk_id < n_used
    chunk_e = jnp.where(used, raw_e, raw_e[n_used - 1])
    valid = jnp.clip(counts[raw_e] - (chunk_row - pad_start[raw_e]), 0, MOE_CHUNK)
    chunk_valid = jnp.where(used, valid, 0)
    tail = pad_start + counts
    blk_row = jnp.arange(n_chunks * MOE_CHUNK // MOE_BLOCK + 1, dtype=jnp.int32)[:, None] * MOE_BLOCK
    empty = ~jnp.any((pad_start[None, :] < blk_row + MOE_BLOCK) & (tail[None, :] > blk_row), axis=1)
    i32 = lambda a: a.astype(jnp.int32)
    return i32(dest), i32(tail), i32(empty), i32(chunk_e), i32(chunk_valid), i32(n_used.reshape(1))


def _rope_tables(seq):
    n_freq = HEAD_DIM // 4
    inv_freq = ROPE_THETA ** (-jnp.arange(n_freq, dtype=F32) / n_freq)
    rows = seq // GRID_W
    row = jnp.repeat(jnp.arange(rows, dtype=F32), GRID_W)
    col = jnp.tile(jnp.arange(GRID_W, dtype=F32), rows)
    ar, ac = row[:, None] * inv_freq, col[:, None] * inv_freq
    zero = jnp.zeros_like(ar)
    cos_t = jnp.concatenate([jnp.cos(ar), jnp.cos(ar), jnp.cos(ac), jnp.cos(ac)], axis=1)
    sin_a = jnp.concatenate([-jnp.sin(ar), zero, -jnp.sin(ac), zero], axis=1)
    sin_b = jnp.concatenate([zero, jnp.sin(ar), zero, jnp.sin(ac)], axis=1)
    return cos_t, sin_a, sin_b


def kernel(x, c, ctx, c_ctx, w_ada, b_ada, w_in, w_gate_up, b_gate, attn_sink, gla_norm_w, w_out, ln1_g, ln1_b, w_router_group, b_router_group, w_router_expert, b_router_expert, w_exp_gate, w_exp_up, w_exp_down, ln2_g, ln2_b):
    batch, seq, d = x.shape
    n_ctx = ctx.shape[1]
    assert w_ada.shape[0] == DEPTH and batch < MOD_ROWS
    assert seq % ROW_TILE == 0 and n_ctx % ROW_TILE == 0 and seq % GLA_STEP == 0 and n_ctx == GLA_STEP
    n_tok = batch * seq
    tiles_per_seq = seq // ROW_TILE
    a_width = d // 2
    kv_width = a_width // A_GROUP
    b_width = d - a_width
    key_width = b_width // 2
    layer = 0

    cc = jnp.concatenate([c, c_ctx[None, :], jnp.zeros((MOD_ROWS - batch - 1, d), F32)], axis=0)
    mod3 = _adaln(cc, w_ada[layer], b_ada[layer]).reshape(MOD_ROWS, 6, d)

    splits = (a_width, kv_width, kv_width, key_width, key_width, b_width, b_width, 2 * GATE_RANK)
    w_in_b = w_in[layer].astype(BF16)
    weights, c0 = [], 0
    for n in splits:
        weights.append(w_in_b[:, c0:c0 + n])
        c0 += n
    zero_up = jnp.zeros((GATE_RANK, key_width), F32)
    wup2 = jnp.concatenate([jnp.concatenate([w_gate_up[layer, 0], zero_up], axis=1),
                            jnp.concatenate([zero_up, w_gate_up[layer, 1]], axis=1)], axis=0).astype(BF16)
    bg2 = b_gate[layer].reshape(1, 2 * key_width)
    tables = _rope_tables(seq)

    xf = x.reshape(n_tok, d)
    qa, ka, va, qb, kb, vb, rb, la_f, la_b = _project(
        xf, mod3, lambda i: i // tiles_per_seq, tables, weights, wup2, bg2, rope=True, tiles_per_seq=tiles_per_seq)
    _, ka_c, va_c, _, kb_c, vb_c, _, lac_f, lac_b = _project(
        ctx.reshape(batch * n_ctx, d), mod3, lambda i: batch, tables, weights, wup2, bg2, rope=False,
        tiles_per_seq=tiles_per_seq)

    out_a = _attention(attn_sink[layer], qa, ka, va, ka_c, va_c, batch, seq, n_ctx)
    o_b = _gla(kb_c, vb_c, lac_b, qb, kb, vb, la_b, batch, seq, n_ctx, reverse=True)
    out_b = _gla(kb_c, vb_c, lac_f, qb, kb, vb, la_f, batch, seq, n_ctx, reverse=False,
                 extra=(o_b, rb, gla_norm_w[layer].reshape(1, b_width)))

    w_out_b = w_out[layer].astype(BF16)
    wr = jnp.concatenate([w_router_expert[layer], w_router_group[layer],
                          jnp.zeros((d, LANES - N_EXPERTS - N_GROUPS), F32)], axis=1).astype(BF16)
    br = jnp.concatenate([b_router_expert[layer], b_router_group[layer],
                          jnp.zeros((LANES - N_EXPERTS - N_GROUPS,), F32)]).reshape(1, LANES)
    x1, h2, rw, ri = _outproj(out_a, out_b, xf, mod3, w_out_b[:a_width], w_out_b[a_width:],
                              ln1_g[layer], ln1_b[layer], wr, br, tiles_per_seq)

    dest, tail_row, empty_block, chunk_e, chunk_valid, n_used = _routing_plan(ri, n_tok)
    dest3 = dest.reshape(n_tok // ROW_TILE, 1, TOP_K * ROW_TILE)
    buf = _dispatch(dest3, tail_row, empty_block, h2)
    y = _experts(chunk_e, chunk_valid, n_used, buf, w_exp_gate[layer], w_exp_up[layer], w_exp_down[layer])
    out = _combine(dest3, y, rw, x1, mod3, ln2_g[layer], ln2_b[layer], tiles_per_seq)
    return out.reshape(batch, seq, d)
```

```python
import functools

import jax
import jax.numpy as jnp
from jax import lax
from jax.experimental import pallas as pl
from jax.experimental.pallas import tpu as pltpu

F32 = jnp.float32
BF16 = jnp.bfloat16

HEAD_DIM = 128
GRID_W = 64
WINDOW = 128
A_BLOCK = 128
A_GROUP = 4
ROPE_THETA = 10000.0
B_HEADS = 4
GATE_RANK = 16
GATE_TAU = 16.0
GLA_CHUNK = 64
N_GROUPS = 4
EXPERTS_PER_GROUP = 8
N_EXPERTS = N_GROUPS * EXPERTS_PER_GROUP
TOP_K = 2
DEPTH = 1
ALPHA = (2.0 * DEPTH) ** 0.25
LN_EPS = 1e-6
LOG2_E = 1.4426950408889634

LANES = 128
SUBLANES = 8
MOD_ROWS = 8
VMEM_LIMIT = 56 * 1024 * 1024

ROW_TILE = 256
PROJ_COLS = 1024
GLA_STEP = 256
MOE_BLOCK = 256
HID_TILE = 512
W_PIECES = 4
DMA_UNROLL = 8


def _params(sem):
    return pltpu.CompilerParams(dimension_semantics=sem, vmem_limit_bytes=VMEM_LIMIT)


def _resident(shape):
    nd = len(shape)
    return pl.BlockSpec(shape, lambda *_: (0,) * nd, pipeline_mode=pl.Buffered(1))


def _ln(x):
    mu = jnp.mean(x, axis=-1, keepdims=True)
    xc = x - mu
    var = jnp.mean(xc * xc, axis=-1, keepdims=True)
    return xc * lax.rsqrt(var + LN_EPS)


def _silu(x):
    return x * jax.nn.sigmoid(x)


def _bdot(a, b):
    return jnp.dot(a, b, preferred_element_type=F32)


def _bdot_nt(a, b):
    return lax.dot_general(a, b, (((1,), (1,)), ((), ())), preferred_element_type=F32)


def _adaln_kernel(c_ref, w_ref, b_ref, o_ref):
    s = _silu(c_ref[...]).astype(BF16)
    o_ref[...] = _bdot(s, w_ref[...].astype(BF16)) + b_ref[...]


def _adaln(cc, w_ada, b_ada):
    d, n = w_ada.shape
    tn = 1024
    return pl.pallas_call(
        _adaln_kernel,
        grid=(n // tn,),
        in_specs=[pl.BlockSpec((MOD_ROWS, d), lambda j: (0, 0)),
                  pl.BlockSpec((d, tn), lambda j: (0, j)),
                  pl.BlockSpec((1, tn), lambda j: (0, j))],
        out_specs=pl.BlockSpec((MOD_ROWS, tn), lambda j: (0, j)),
        out_shape=jax.ShapeDtypeStruct((MOD_ROWS, n), F32),
        compiler_params=_params(("arbitrary",)),
        name="adaln",
    )(cc, w_ada, b_ada.reshape(1, n))


def _proj_kernel(x_ref, mod_ref, cos_ref, sina_ref, sinb_ref,
                 wqa, wka, wva, wqb, wkb, wvb, wrb, wg, wup, bg,
                 qa_o, ka_o, va_o, qb_o, kb_o, vb_o, rb_o, laf_o, lab_o, *, rope):
    h = _ln(x_ref[...]) * (1.0 + mod_ref[1:2, :]) + mod_ref[0:1, :]
    hb = h.astype(BF16)

    def rot(t):
        return (t * cos_ref[...] + pltpu.roll(t, 96, 1) * sina_ref[...]
                + pltpu.roll(t, 32, 1) * sinb_ref[...])

    def project(w_ref, o_ref, rotate):
        n = w_ref.shape[1]
        step = min(n, PROJ_COLS)
        for c0 in range(0, n, step):
            t = _bdot(hb, w_ref[:, c0:c0 + step])
            if rotate:
                for l0 in range(0, step, LANES):
                    o_ref[:, c0 + l0:c0 + l0 + LANES] = rot(t[:, l0:l0 + LANES]).astype(o_ref.dtype)
            else:
                o_ref[:, c0:c0 + step] = t.astype(o_ref.dtype)

    project(wqa, qa_o, rope)
    project(wka, ka_o, rope)
    project(wva, va_o, False)
    project(wqb, qb_o, False)
    project(wkb, kb_o, False)
    project(wvb, vb_o, False)
    project(wrb, rb_o, False)
    gl = _bdot(hb, wg[...]).astype(BF16)
    z = _bdot(gl, wup[...]) + bg[...]
    la = (jnp.minimum(z, 0.0) - jnp.log1p(jnp.exp(-jnp.abs(z)))) / GATE_TAU
    kw = laf_o.shape[1]
    laf_o[...] = la[:, :kw]
    lab_o[...] = la[:, kw:]


def _project(xf, mod3, mod_row_of_tile, tables, weights, wup2, bg2, *, rope, tiles_per_seq):
    r, d = xf.shape
    tm = ROW_TILE
    wqa, wka, wva, wqb, wkb, wvb, wrb, wg = weights
    cos_t, sina_t, sinb_t = tables
    row = lambda n: pl.BlockSpec((tm, n), lambda i: (i, 0))
    tab = pl.BlockSpec((tm, LANES), lambda i: (i % tiles_per_seq, 0))
    outs = [(wqa.shape[1], BF16), (wka.shape[1], BF16), (wva.shape[1], BF16),
            (wqb.shape[1], F32), (wkb.shape[1], F32), (wvb.shape[1], BF16),
            (wrb.shape[1], F32), (wqb.shape[1], F32), (wqb.shape[1], F32)]
    return pl.pallas_call(
        functools.partial(_proj_kernel, rope=rope),
        grid=(r // tm,),
        in_specs=[row(d),
                  pl.BlockSpec((None, 6, d), lambda i: (mod_row_of_tile(i), 0, 0)),
                  tab, tab, tab]
                 + [_resident(w.shape) for w in weights]
                 + [_resident(wup2.shape), _resident(bg2.shape)],
        out_specs=[row(n) for n, _ in outs],
        out_shape=[jax.ShapeDtypeStruct((r, n), dt) for n, dt in outs],
        compiler_params=_params(("parallel",)),
        name="in_proj_rope" if rope else "in_proj_ctx",
    )(xf, mod3, cos_t, sina_t, sinb_t, *weights, wup2, bg2)


def _attn_kernel(sink_ref, q_ref, kp_ref, kc_ref, kn_ref, vp_ref, vc_ref, vn_ref, kx_ref, vx_ref, o_ref):
    n = pl.program_id(1)
    nb = pl.num_programs(1)
    blk = A_BLOCK
    rows = A_GROUP * blk
    scale = HEAD_DIM ** -0.5
    exp2_scale = scale * LOG2_E
    n_ctx = kx_ref.shape[0]
    qi = lax.broadcasted_iota(jnp.int32, (rows, blk), 0) % blk
    kj = lax.broadcasted_iota(jnp.int32, (rows, blk), 1)
    ok_prev = (kj >= qi) & (n > 0)
    ok_next = (kj <= qi) & (n < nb - 1)
    for hk in range(kp_ref.shape[1] // HEAD_DIM):
        ks = slice(hk * HEAD_DIM, (hk + 1) * HEAD_DIM)
        q4 = jnp.concatenate(
            [q_ref[:, (hk * A_GROUP + g) * HEAD_DIM:(hk * A_GROUP + g + 1) * HEAD_DIM] for g in range(A_GROUP)],
            axis=0)
        s_ctx = _bdot_nt(q4, kx_ref[:, ks])
        s_prev = jnp.where(ok_prev, _bdot_nt(q4, kp_ref[:, ks]), -jnp.inf)
        s_cur = _bdot_nt(q4, kc_ref[:, ks])
        s_next = jnp.where(ok_next, _bdot_nt(q4, kn_ref[:, ks]), -jnp.inf)
        sink = jnp.concatenate(
            [jnp.full((blk, 1), sink_ref[hk * A_GROUP + g] / scale, F32) for g in range(A_GROUP)], axis=0)
        ctx_lanes = [s_ctx[:, l0:l0 + LANES] for l0 in range(0, n_ctx, LANES)]
        m_lanes = jnp.maximum(jnp.maximum(s_prev, s_cur), s_next)
        for piece in ctx_lanes:
            m_lanes = jnp.maximum(m_lanes, piece)
        m = jnp.maximum(jnp.max(m_lanes, axis=-1, keepdims=True), sink)
        e_ctx = jnp.exp2((s_ctx - m) * exp2_scale)
        e_prev = jnp.exp2((s_prev - m) * exp2_scale)
        e_cur = jnp.exp2((s_cur - m) * exp2_scale)
        e_next = jnp.exp2((s_next - m) * exp2_scale)
        e_lanes = e_prev + e_cur + e_next
        for l0 in range(0, n_ctx, LANES):
            e_lanes = e_lanes + e_ctx[:, l0:l0 + LANES]
        den = jnp.exp2((sink - m) * exp2_scale) + jnp.sum(e_lanes, axis=-1, keepdims=True)
        o = (_bdot(e_ctx.astype(BF16), vx_ref[:, ks])
             + (_bdot(e_prev.astype(BF16), vp_ref[:, ks])
                + _bdot(e_cur.astype(BF16), vc_ref[:, ks])
                + _bdot(e_next.astype(BF16), vn_ref[:, ks]))) * (1.0 / den)
        for g in range(A_GROUP):
            hq = hk * A_GROUP + g
            o_ref[:, hq * HEAD_DIM:(hq + 1) * HEAD_DIM] = o[g * blk:(g + 1) * blk, :].astype(o_ref.dtype)


def _attention(sink, qa, ka, va, ka_c, va_c, batch, seq, n_ctx):
    nb = seq // A_BLOCK
    aw = qa.shape[1]
    kvw = ka.shape[1]
    blk = A_BLOCK
    prev = lambda b, n: (b * nb + jnp.maximum(n - 1, 0), 0)
    cur = lambda b, n: (b * nb + n, 0)
    nxt = lambda b, n: (b * nb + jnp.minimum(n + 1, nb - 1), 0)
    kv = lambda f: pl.BlockSpec((blk, kvw), f)
    ctx = pl.BlockSpec((n_ctx, kvw), lambda b, n: (b, 0))
    return pl.pallas_call(
        _attn_kernel,
        grid=(batch, nb),
        in_specs=[pl.BlockSpec(memory_space=pltpu.SMEM),
                  pl.BlockSpec((blk, aw), cur),
                  kv(prev), kv(cur), kv(nxt), kv(prev), kv(cur), kv(nxt), ctx, ctx],
        out_specs=pl.BlockSpec((blk, aw), cur),
        out_shape=jax.ShapeDtypeStruct((batch * seq, aw), BF16),
        compiler_params=_params(("parallel", "parallel")),
        name="window_gqa",
    )(sink, qa, ka, ka, ka, va, va, va, ka_c, va_c)


def _chunk_cumsum(g, *, reverse):
    rows = g.shape[0]
    p = lax.broadcasted_iota(jnp.int32, g.shape, 0) % GLA_CHUNK
    s = 1
    while s < GLA_CHUNK:
        if reverse:
            g = g + jnp.where(p < GLA_CHUNK - s, pltpu.roll(g, rows - s, 0), 0.0)
        else:
            g = g + jnp.where(p >= s, pltpu.roll(g, s, 0), 0.0)
        s *= 2
    return g


def _per_chunk_row(x, i):
    c = GLA_CHUNK
    return jnp.concatenate(
        [jnp.broadcast_to(x[j * c + i:j * c + i + 1, :], (c, x.shape[1])) for j in range(x.shape[0] // c)], axis=0)


def _gla_block(q, k, v, cum, state_t, mask, *, reverse, need_o):
    c = GLA_CHUNK
    rows, dk = k.shape
    n_chunks = rows // c
    i_last = 0 if reverse else c - 1
    i_mid = c // 2 if reverse else c // 2 - 1
    b_last = _per_chunk_row(cum, i_last)
    kdec = (k * jnp.exp(b_last - cum)).astype(BF16)
    if need_o:
        b_mid = _per_chunk_row(cum, i_mid)
        qc = q * dk ** -0.5
        qm = (qc * jnp.exp(cum - b_mid)).astype(BF16)
        km = (k * jnp.exp(b_mid - cum)).astype(BF16)
        a = jnp.where(mask, _bdot_nt(qm, km), 0.0)
        o_intra = _bdot(a.astype(BF16), v)
        qe = (qc * jnp.exp(cum)).astype(BF16)
    o_inter = [None] * n_chunks
    for j in (reversed(range(n_chunks)) if reverse else range(n_chunks)):
        rs = slice(j * c, (j + 1) * c)
        if need_o:
            o_inter[j] = _bdot_nt(qe[rs, :], state_t.astype(BF16))
        kv_t = lax.dot_general(v[rs, :], kdec[rs, :], (((0,), (0,)), ((), ())), preferred_element_type=F32)
        state_t = state_t * jnp.exp(cum[j * c + i_last:j * c + i_last + 1, :]) + kv_t
    o = o_intra + jnp.concatenate(o_inter, axis=0) if need_o else None
    return o, state_t


def _gla_kernel(*refs, reverse, final):
    if final:
        (kx_ref, vx_ref, gx_ref, q_ref, k_ref, v_ref, g_ref, ob_ref, r_ref, nw_ref, o_ref, st_ref) = refs
    else:
        (kx_ref, vx_ref, gx_ref, q_ref, k_ref, v_ref, g_ref, o_ref, st_ref) = refs
    t = pl.program_id(1)
    n_heads = st_ref.shape[0]
    dv, dk = st_ref.shape[1:]

    @pl.when(t == 0)
    def _context():
        cum = _chunk_cumsum(gx_ref[...], reverse=reverse)
        for h in range(n_heads):
            ks, vs = slice(h * dk, (h + 1) * dk), slice(h * dv, (h + 1) * dv)
            _, st = _gla_block(None, kx_ref[:, ks], vx_ref[:, vs], cum[:, ks], jnp.zeros((dv, dk), F32), None,
                               reverse=reverse, need_o=False)
            st_ref[h] = st

    @pl.when(t > 0)
    def _latent():
        rows = k_ref.shape[0]
        cum = _chunk_cumsum(g_ref[...], reverse=reverse)
        r = lax.broadcasted_iota(jnp.int32, (rows, rows), 0)
        s = lax.broadcasted_iota(jnp.int32, (rows, rows), 1)
        causal = (s >= r) if reverse else (s <= r)
        mask = causal & ((r // GLA_CHUNK) == (s // GLA_CHUNK))
        for h in range(n_heads):
            ks, vs = slice(h * dk, (h + 1) * dk), slice(h * dv, (h + 1) * dv)
            o, st = _gla_block(q_ref[:, ks], k_ref[:, ks], v_ref[:, vs], cum[:, ks], st_ref[h], mask,
                               reverse=reverse, need_o=True)
            st_ref[h] = st
            if final:
                o = o + ob_ref[:, vs]
                o = o * lax.rsqrt(jnp.mean(o * o, axis=-1, keepdims=True) + LN_EPS)
                o = o * nw_ref[:, vs]
                o = o * _silu(r_ref[:, vs])
            o_ref[:, vs] = o.astype(o_ref.dtype)


def _gla(kb_c, vb_c, la_c, qb, kb, vb, la, batch, seq, n_ctx, *, reverse, extra=None):
    assert n_ctx % GLA_CHUNK == 0 and GLA_STEP % GLA_CHUNK == 0
    tb = GLA_STEP
    nt = seq // tb
    kw, vw = qb.shape[1], vb.shape[1]
    final = extra is not None

    def lat(b, t):
        i = jnp.maximum(t - 1, 0)
        if reverse:
            i = nt - 1 - i
        return (b * nt + i, 0)

    cx = lambda b, t: (b, 0)
    in_specs = [pl.BlockSpec((n_ctx, kw), cx), pl.BlockSpec((n_ctx, vw), cx), pl.BlockSpec((n_ctx, kw), cx),
                pl.BlockSpec((tb, kw), lat), pl.BlockSpec((tb, kw), lat),
                pl.BlockSpec((tb, vw), lat), pl.BlockSpec((tb, kw), lat)]
    args = [kb_c, vb_c, la_c, qb, kb, vb, la]
    if final:
        o_other, rb, norm_w = extra
        in_specs += [pl.BlockSpec((tb, vw), lat), pl.BlockSpec((tb, vw), lat), _resident((1, vw))]
        args += [o_other, rb, norm_w]
    return pl.pallas_call(
        functools.partial(_gla_kernel, reverse=reverse, final=final),
        grid=(batch, nt + 1),
        in_specs=in_specs,
        out_specs=pl.BlockSpec((tb, vw), lat),
        out_shape=jax.ShapeDtypeStruct((batch * seq, vw), BF16 if final else F32),
        scratch_shapes=[pltpu.VMEM((B_HEADS, vw // B_HEADS, kw // B_HEADS), F32)],
        compiler_params=_params(("parallel", "arbitrary")),
        name="gla_fwd_out" if final else "gla_bwd",
    )(*args)


def _outproj_kernel(oa_ref, ob_ref, x_ref, mod_ref, wt_ref, wb_ref, lng_ref, lnb_ref, wr_ref, br_ref,
                    x1_o, h2_o, rw_o, ri_o):
    y = _bdot(oa_ref[...], wt_ref[...]) + _bdot(ob_ref[...], wb_ref[...])
    x1 = _ln(ALPHA * x_ref[...] + mod_ref[2:3, :] * y) * lng_ref[...] + lnb_ref[...]
    x1_o[...] = x1
    h2 = _ln(x1) * (1.0 + mod_ref[4:5, :]) + mod_ref[3:4, :]
    h2_o[...] = h2
    lg = _bdot(h2.astype(BF16), wr_ref[...]) + br_ref[...]
    lane = lax.broadcasted_iota(jnp.int32, lg.shape, 1)
    lanef = lane.astype(F32)
    big = float(LANES)
    is_g = (lane >= N_EXPERTS) & (lane < N_EXPERTS + N_GROUPS)
    gl = jnp.where(is_g, lg, -jnp.inf)
    gmax = jnp.max(gl, axis=-1, keepdims=True)
    pg_top = 1.0 / jnp.sum(jnp.exp(gl - gmax), axis=-1, keepdims=True)
    grp = jnp.min(jnp.where(gl == gmax, lanef, big), axis=-1, keepdims=True) - N_EXPERTS
    in_grp = (lane < N_EXPERTS) & ((lane // EXPERTS_PER_GROUP).astype(F32) == grp)
    el = jnp.where(in_grp, lg, -jnp.inf)
    m1 = jnp.max(el, axis=-1, keepdims=True)
    i1 = jnp.min(jnp.where(el == m1, lanef, big), axis=-1, keepdims=True)
    el2 = jnp.where(lanef == i1, -jnp.inf, el)
    m2 = jnp.max(el2, axis=-1, keepdims=True)
    i2 = jnp.min(jnp.where(el2 == m2, lanef, big), axis=-1, keepdims=True)
    e2 = jnp.exp(m2 - m1)
    w1 = pg_top / (1.0 + e2)
    w2 = pg_top * e2 / (1.0 + e2)
    rw_o[...] = jnp.where(lane == 0, w1, jnp.where(lane == 1, w2, 0.0))
    ri_o[...] = jnp.where(lane == 0, i1, jnp.where(lane == 1, i2, 0.0)).astype(jnp.int32)


def _outproj(out_a, out_b, xf, mod3, w_top, w_bot, ln_g, ln_b, wr, br, tiles_per_seq):
    r, d = xf.shape
    tm = ROW_TILE
    row = lambda n: pl.BlockSpec((tm, n), lambda i: (i, 0))
    return pl.pallas_call(
        _outproj_kernel,
        grid=(r // tm,),
        in_specs=[row(out_a.shape[1]), row(out_b.shape[1]), row(d),
                  pl.BlockSpec((None, 6, d), lambda i: (i // tiles_per_seq, 0, 0)),
                  _resident(w_top.shape), _resident(w_bot.shape),
                  _resident((1, d)), _resident((1, d)), _resident(wr.shape), _resident(br.shape)],
        out_specs=[row(d), row(d), row(LANES), row(LANES)],
        out_shape=[jax.ShapeDtypeStruct((r, d), F32), jax.ShapeDtypeStruct((r, d), F32),
                   jax.ShapeDtypeStruct((r, LANES), F32), jax.ShapeDtypeStruct((r, LANES), jnp.int32)],
        compiler_params=_params(("parallel",)),
        name="out_proj_router",
    )(out_a, out_b, xf, mod3, w_top, w_bot, ln_g.reshape(1, d), ln_b.reshape(1, d), wr, br)


def _row_copy(src_ref, src_row, dst_ref, dst_row, sem):
    return pltpu.make_async_copy(src_ref.at[pl.ds(src_row, 1)], dst_ref.at[pl.ds(dst_row, 1)], sem)


def _dispatch_kernel(dest_ref, tail_ref, empty_ref, h_ref, buf_ref, zero_ref, sem, zsem):
    tm = h_ref.shape[0]
    step = pl.program_id(0)

    def zero_block(b):
        row = pl.multiple_of(b * MOE_BLOCK, MOE_BLOCK)
        return pltpu.make_async_copy(zero_ref, buf_ref.at[pl.ds(row, MOE_BLOCK)], zsem)

    def for_empty_blocks(fn):
        def body(b, carry):
            @pl.when(empty_ref[b] != 0)
            def _():
                fn(b)
            return carry
        lax.fori_loop(0, empty_ref.shape[0], body, 0)

    @pl.when(step == 0)
    def _zero_fill():
        zero_ref[...] = jnp.zeros(zero_ref.shape, zero_ref.dtype)
        for e in range(N_EXPERTS):
            start = pl.multiple_of(tail_ref[e] // SUBLANES * SUBLANES, SUBLANES)
            pltpu.make_async_copy(zero_ref, buf_ref.at[pl.ds(start, MOE_BLOCK)], sem).start()
        for e in range(N_EXPERTS):
            pltpu.make_async_copy(zero_ref, buf_ref.at[pl.ds(0, MOE_BLOCK)], sem).wait()
        for_empty_blocks(lambda b: zero_block(b).start())

    @pl.when(step == pl.num_programs(0) - 1)
    def _zero_done():
        for_empty_blocks(lambda b: zero_block(b).wait())

    def issue(i, carry):
        for k in range(TOP_K):
            _row_copy(h_ref, i, buf_ref, dest_ref[0, 0, TOP_K * i + k], sem).start()
        return carry

    lax.fori_loop(0, tm, issue, 0, unroll=DMA_UNROLL)
    for _ in range(TOP_K):
        pltpu.make_async_copy(h_ref, buf_ref.at[pl.ds(0, tm)], sem).wait()


def _dispatch(dest3, tail_row, empty_block, h2):
    r, d = h2.shape
    tm = ROW_TILE
    n_rows = empty_block.shape[0] * MOE_BLOCK
    return pl.pallas_call(
        _dispatch_kernel,
        grid=(r // tm,),
        in_specs=[pl.BlockSpec((1, 1, TOP_K * tm), lambda i: (i, 0, 0), memory_space=pltpu.SMEM),
                  pl.BlockSpec(memory_space=pltpu.SMEM),
                  pl.BlockSpec(memory_space=pltpu.SMEM),
                  pl.BlockSpec((tm, d), lambda i: (i, 0))],
        out_specs=pl.BlockSpec(memory_space=pl.ANY),
        out_shape=jax.ShapeDtypeStruct((n_rows, d), h2.dtype),
        scratch_shapes=[pltpu.VMEM((MOE_BLOCK, d), h2.dtype), pltpu.SemaphoreType.DMA,
                        pltpu.SemaphoreType.DMA],
        compiler_params=_params(("arbitrary",)),
        name="moe_dispatch",
    )(dest3, tail_row, empty_block, h2)


def _expert_kernel(bq_ref, lo_ref, hi_ref, el_ref, nq_ref, ub_ref, x_ref, wg_hbm, wu_hbm, wd_hbm, o_ref,
                   wgb, wub, wdb, sg, su, sd, sems):
    b = pl.program_id(0)
    q = bq_ref[b]
    slot = q % 2
    mats = ((wg_hbm, sg, wgb), (wu_hbm, su, wub), (wd_hbm, sd, wdb))
    n_pieces = W_PIECES * len(mats)

    def piece_copy(m, qt, t):
        hbm, stage, _ = mats[m]
        pr = stage.shape[1]
        return pltpu.make_async_copy(hbm.at[el_ref[qt], pl.ds(t * pr, pr), :], stage.at[t % 2],
                                     sems.at[m, t % 2])

    def process(qt, lo, hi):
        for p in range(n_pieces):
            m, t = p % len(mats), p // len(mats)

            @pl.when((lo <= p) & (p < hi))
            def _():
                _, stage, resident = mats[m]
                pr = stage.shape[1]
                piece_copy(m, qt, t).wait()
                resident[qt % 2, pl.ds(t * pr, pr), :] = stage[t % 2].astype(BF16)
                q2 = qt + (t + 2) // W_PIECES

                @pl.when(q2 < nq_ref[0])
                def _():
                    piece_copy(m, q2, (t + 2) % W_PIECES).start()

    @pl.when(b == 0)
    def _first_expert():
        for m in range(len(mats)):
            for t in range(2):
                piece_copy(m, 0, t).start()
        process(0, 0, n_pieces)

    @pl.when(b < ub_ref[0])
    def _compute():
        xb = x_ref[...].astype(BF16)
        hid = wgb.shape[2]
        acts = []
        for h0 in range(0, hid, HID_TILE):
            hs = pl.ds(h0, HID_TILE)
            acts.append((_silu(_bdot(xb, wgb[slot, :, hs])) * _bdot(xb, wub[slot, :, hs])).astype(BF16))
        o_ref[...] = _bdot(jnp.concatenate(acts, axis=1), wdb[slot])

    @pl.when(b >= ub_ref[0])
    def _unused():
        o_ref[...] = jnp.zeros(o_ref.shape, o_ref.dtype)

    process(q + 1, lo_ref[b], hi_ref[b])


def _experts(plan, buf, wg, wu, wd, n_out_rows):
    d = buf.shape[1]
    hid = wg.shape[2]
    assert d % W_PIECES == 0 and hid % W_PIECES == 0 and W_PIECES % 2 == 0
    blk = lambda f: pl.BlockSpec((MOE_BLOCK, d), f)
    return pl.pallas_call(
        _expert_kernel,
        grid_spec=pltpu.PrefetchScalarGridSpec(
            num_scalar_prefetch=len(plan),
            grid=(n_out_rows // MOE_BLOCK,),
            in_specs=[blk(lambda b, bq, lo, hi, el, nq, ub: (jnp.minimum(b, ub[0] - 1), 0)),
                      pl.BlockSpec(memory_space=pl.ANY), pl.BlockSpec(memory_space=pl.ANY),
                      pl.BlockSpec(memory_space=pl.ANY)],
            out_specs=blk(lambda b, bq, lo, hi, el, nq, ub: (b, 0)),
            scratch_shapes=[pltpu.VMEM((2, d, hid), BF16), pltpu.VMEM((2, d, hid), BF16),
                            pltpu.VMEM((2, hid, d), BF16),
                            pltpu.VMEM((2, d // W_PIECES, hid), F32), pltpu.VMEM((2, d // W_PIECES, hid), F32),
                            pltpu.VMEM((2, hid // W_PIECES, d), F32),
                            pltpu.SemaphoreType.DMA((3, 2))],
        ),
        out_shape=jax.ShapeDtypeStruct((n_out_rows, d), F32),
        compiler_params=_params(("arbitrary",)),
        name="moe_experts",
    )(*plan, buf, wg, wu, wd)


def _combine_kernel(dest_ref, dnext_ref, y_ref, rw_ref, x1_ref, mod_ref, lng_ref, lnb_ref, o_ref, g_ref, sem):
    tm = x1_ref.shape[0]
    i = pl.program_id(0)
    slot = i % 2

    def gather(d_ref, sl):
        def issue(r, carry):
            for k in range(TOP_K):
                _row_copy(y_ref, d_ref[0, 0, TOP_K * r + k], g_ref.at[sl, k], r, sem.at[sl]).start()
            return carry
        lax.fori_loop(0, tm, issue, 0, unroll=DMA_UNROLL)

    @pl.when(i == 0)
    def _():
        gather(dest_ref, 0)

    @pl.when(i + 1 < pl.num_programs(0))
    def _():
        gather(dnext_ref, 1 - slot)

    for k in range(TOP_K):
        pltpu.make_async_copy(y_ref.at[pl.ds(0, tm)], g_ref.at[slot, k], sem.at[slot]).wait()
    f = rw_ref[:, 0:1] * g_ref[slot, 0] + rw_ref[:, 1:2] * g_ref[slot, 1]
    o_ref[...] = _ln(ALPHA * x1_ref[...] + mod_ref[5:6, :] * f) * lng_ref[...] + lnb_ref[...]


def _combine(dest3, y, rw, x1, mod3, ln_g, ln_b, tiles_per_seq):
    r, d = x1.shape
    tm = ROW_TILE
    n = r // tm
    row = lambda w: pl.BlockSpec((tm, w), lambda i: (i, 0))
    return pl.pallas_call(
        _combine_kernel,
        grid=(n,),
        in_specs=[pl.BlockSpec((1, 1, TOP_K * tm), lambda i: (i, 0, 0), memory_space=pltpu.SMEM),
                  pl.BlockSpec((1, 1, TOP_K * tm), lambda i: (jnp.minimum(i + 1, n - 1), 0, 0),
                               memory_space=pltpu.SMEM),
                  pl.BlockSpec(memory_space=pl.ANY),
                  row(LANES), row(d),
                  pl.BlockSpec((None, 6, d), lambda i: (i // tiles_per_seq, 0, 0)),
                  _resident((1, d)), _resident((1, d))],
        out_specs=row(d),
        out_shape=jax.ShapeDtypeStruct((r, d), F32),
        scratch_shapes=[pltpu.VMEM((2, TOP_K, tm, d), F32), pltpu.SemaphoreType.DMA((2,))],
        compiler_params=_params(("arbitrary",)),
        name="moe_combine_ln",
    )(dest3, dest3, y, rw, x1, mod3, ln_g.reshape(1, d), ln_b.reshape(1, d))


def _routing_plan(ri, n_tokens):
    eid = ri[:, :TOP_K].reshape(-1)
    m = n_tokens * TOP_K
    onehot = (eid[:, None] == jnp.arange(N_EXPERTS, dtype=jnp.int32)[None, :]).astype(jnp.int32)
    csum = jnp.cumsum(onehot, axis=0)
    counts = csum[-1]
    rank = jnp.sum(csum * onehot, axis=1) - 1
    padded = (counts + MOE_BLOCK - 1) // MOE_BLOCK * MOE_BLOCK
    pad_end = jnp.cumsum(padded)
    pad_start = pad_end - padded
    dest = jnp.sum(onehot * pad_start[None, :], axis=1) + rank
    tail = pad_start + counts
    n_blocks = m // MOE_BLOCK + N_EXPERTS
    used_blocks = pad_end[-1] // MOE_BLOCK
    empty = jnp.arange(n_blocks + 1, dtype=jnp.int32) >= used_blocks

    n_pieces = 3 * W_PIECES
    has = counts > 0
    ordinal = jnp.cumsum(has.astype(jnp.int32)) - 1
    n_ord = ordinal[-1] + 1
    e_ids = jnp.arange(N_EXPERTS, dtype=jnp.int32)
    ord_expert = jnp.sum(jnp.where(has[None, :] & (ordinal[None, :] == e_ids[:, None]), e_ids[None, :], 0), axis=1)
    bid = jnp.arange(n_blocks, dtype=jnp.int32)
    blk_e = jnp.minimum(jnp.sum(bid[:, None] * MOE_BLOCK >= pad_end[None, :], axis=1), N_EXPERTS - 1)
    used = bid < used_blocks
    blk_q = jnp.where(used, ordinal[blk_e], n_ord - 1)
    i_in_e = bid - pad_start[blk_e] // MOE_BLOCK
    k_e = jnp.maximum(padded[blk_e] // MOE_BLOCK, 1)
    brings = used & (blk_q + 1 < n_ord)
    lo = jnp.where(brings, n_pieces * i_in_e // k_e, 0)
    hi = jnp.where(brings, n_pieces * (i_in_e + 1) // k_e, 0)
    i32 = lambda a: a.astype(jnp.int32)
    plan = (i32(blk_q), i32(lo), i32(hi), i32(ord_expert), i32(n_ord.reshape(1)), i32(used_blocks.reshape(1)))
    return i32(dest), i32(tail), i32(empty), plan, n_blocks * MOE_BLOCK


def _rope_tables(seq):
    n_freq = HEAD_DIM // 4
    inv_freq = ROPE_THETA ** (-jnp.arange(n_freq, dtype=F32) / n_freq)
    rows = seq // GRID_W
    row = jnp.repeat(jnp.arange(rows, dtype=F32), GRID_W)
    col = jnp.tile(jnp.arange(GRID_W, dtype=F32), rows)
    ar, ac = row[:, None] * inv_freq, col[:, None] * inv_freq
    zero = jnp.zeros_like(ar)
    cos_t = jnp.concatenate([jnp.cos(ar), jnp.cos(ar), jnp.cos(ac), jnp.cos(ac)], axis=1)
    sin_a = jnp.concatenate([-jnp.sin(ar), zero, -jnp.sin(ac), zero], axis=1)
    sin_b = jnp.concatenate([zero, jnp.sin(ar), zero, jnp.sin(ac)], axis=1)
    return cos_t, sin_a, sin_b


def kernel(x, c, ctx, c_ctx, w_ada, b_ada, w_in, w_gate_up, b_gate, attn_sink, gla_norm_w, w_out, ln1_g, ln1_b, w_router_group, b_router_group, w_router_expert, b_router_expert, w_exp_gate, w_exp_up, w_exp_down, ln2_g, ln2_b):
    batch, seq, d = x.shape
    n_ctx = ctx.shape[1]
    assert w_ada.shape[0] == DEPTH and batch < MOD_ROWS
    assert seq % ROW_TILE == 0 and (batch * n_ctx) % ROW_TILE == 0 and seq % GLA_STEP == 0 and n_ctx == GLA_STEP
    n_tok = batch * seq
    tiles_per_seq = seq // ROW_TILE
    a_width = d // 2
    kv_width = a_width // A_GROUP
    b_width = d - a_width
    key_width = b_width // 2
    layer = 0

    cc = jnp.concatenate([c, c_ctx[None, :], jnp.zeros((MOD_ROWS - batch - 1, d), F32)], axis=0)
    mod3 = _adaln(cc, w_ada[layer], b_ada[layer]).reshape(MOD_ROWS, 6, d)

    splits = (a_width, kv_width, kv_width, key_width, key_width, b_width, b_width, 2 * GATE_RANK)
    w_in_b = w_in[layer].astype(BF16)
    weights, c0 = [], 0
    for n in splits:
        weights.append(w_in_b[:, c0:c0 + n])
        c0 += n
    zero_up = jnp.zeros((GATE_RANK, key_width), F32)
    wup2 = jnp.concatenate([jnp.concatenate([w_gate_up[layer, 0], zero_up], axis=1),
                            jnp.concatenate([zero_up, w_gate_up[layer, 1]], axis=1)], axis=0).astype(BF16)
    bg2 = b_gate[layer].reshape(1, 2 * key_width)
    tables = _rope_tables(seq)

    xf = x.reshape(n_tok, d)
    qa, ka, va, qb, kb, vb, rb, la_f, la_b = _project(
        xf, mod3, lambda i: i // tiles_per_seq, tables, weights, wup2, bg2, rope=True, tiles_per_seq=tiles_per_seq)
    _, ka_c, va_c, _, kb_c, vb_c, _, lac_f, lac_b = _project(
        ctx.reshape(batch * n_ctx, d), mod3, lambda i: batch, tables, weights, wup2, bg2, rope=False,
        tiles_per_seq=tiles_per_seq)

    out_a = _attention(attn_sink[layer], qa, ka, va, ka_c, va_c, batch, seq, n_ctx)
    o_b = _gla(kb_c, vb_c, lac_b, qb, kb, vb, la_b, batch, seq, n_ctx, reverse=True)
    out_b = _gla(kb_c, vb_c, lac_f, qb, kb, vb, la_f, batch, seq, n_ctx, reverse=False,
                 extra=(o_b, rb, gla_norm_w[layer].reshape(1, b_width)))

    w_out_b = w_out[layer].astype(BF16)
    wr = jnp.concatenate([w_router_expert[layer], w_router_group[layer],
                          jnp.zeros((d, LANES - N_EXPERTS - N_GROUPS), F32)], axis=1).astype(BF16)
    br = jnp.concatenate([b_router_expert[layer], b_router_group[layer],
                          jnp.zeros((LANES - N_EXPERTS - N_GROUPS,), F32)]).reshape(1, LANES)
    x1, h2, rw, ri = _outproj(out_a, out_b, xf, mod3, w_out_b[:a_width], w_out_b[a_width:],
                              ln1_g[layer], ln1_b[layer], wr, br, tiles_per_seq)

    dest, tail_row, empty_block, plan, n_buf_rows = _routing_plan(ri, n_tok)
    dest3 = dest.reshape(n_tok // ROW_TILE, 1, TOP_K * ROW_TILE)
    buf = _dispatch(dest3, tail_row, empty_block, h2)
    y = _experts(plan, buf, w_exp_gate[layer], w_exp_up[layer], w_exp_down[layer], n_buf_rows)
    out = _combine(dest3, y, rw, x1, mod3, ln2_g[layer], ln2_b[layer], tiles_per_seq)
    return out.reshape(batch, seq, d)
```

```python
import functools

import jax
import jax.numpy as jnp
from jax import lax
from jax.experimental import pallas as pl
from jax.experimental.pallas import tpu as pltpu

F32 = jnp.float32
BF16 = jnp.bfloat16

HEAD_DIM = 128
GRID_W = 64
WINDOW = 128
A_BLOCK = 128
A_GROUP = 4
ROPE_THETA = 10000.0
B_HEADS = 4
GATE_RANK = 16
GATE_TAU = 16.0
GLA_CHUNK = 64
N_GROUPS = 4
EXPERTS_PER_GROUP = 8
N_EXPERTS = N_GROUPS * EXPERTS_PER_GROUP
TOP_K = 2
DEPTH = 1
ALPHA = (2.0 * DEPTH) ** 0.25
LN_EPS = 1e-6
LOG2_E = 1.4426950408889634

LANES = 128
SUBLANES = 8
MOD_ROWS = 8
VMEM_LIMIT = 56 * 1024 * 1024

ROW_TILE = 256
OUT_TILE = 512
SUB_ROWS = 256
PROJ_COLS = 1024
GLA_STEP = 256
MOE_BLOCK = 256
HID_TILE = 512
W_PIECES = 4
DMA_UNROLL = 8
COMBINE_UNROLL = 8


def _params(sem):
    return pltpu.CompilerParams(dimension_semantics=sem, vmem_limit_bytes=VMEM_LIMIT)


def _resident(shape):
    nd = len(shape)
    return pl.BlockSpec(shape, lambda *_: (0,) * nd, pipeline_mode=pl.Buffered(1))


def _ln(x):
    mu = jnp.mean(x, axis=-1, keepdims=True)
    xc = x - mu
    var = jnp.mean(xc * xc, axis=-1, keepdims=True)
    return xc * lax.rsqrt(var + LN_EPS)


def _silu(x):
    return x * jax.nn.sigmoid(x)


def _bdot(a, b):
    return jnp.dot(a, b, preferred_element_type=F32)


def _bdot_nt(a, b):
    return lax.dot_general(a, b, (((1,), (1,)), ((), ())), preferred_element_type=F32)


def _adaln_kernel(c_ref, w_ref, b_ref, o_ref):
    s = _silu(c_ref[...]).astype(BF16)
    o_ref[...] = _bdot(s, w_ref[...].astype(BF16)) + b_ref[...]


def _adaln(cc, w_ada, b_ada):
    d, n = w_ada.shape
    tn = 1024
    return pl.pallas_call(
        _adaln_kernel,
        grid=(n // tn,),
        in_specs=[pl.BlockSpec((MOD_ROWS, d), lambda j: (0, 0)),
                  pl.BlockSpec((d, tn), lambda j: (0, j)),
                  pl.BlockSpec((1, tn), lambda j: (0, j))],
        out_specs=pl.BlockSpec((MOD_ROWS, tn), lambda j: (0, j)),
        out_shape=jax.ShapeDtypeStruct((MOD_ROWS, n), F32),
        compiler_params=_params(("arbitrary",)),
        name="adaln",
    )(cc, w_ada, b_ada.reshape(1, n))


def _proj_kernel(x_ref, mod_ref, cos_ref, sina_ref, sinb_ref, w_ref, wup, bg,
                 qa_o, ka_o, va_o, qb_o, kb_o, vb_o, rb_o, laf_o, lab_o, *, rope):
    h = _ln(x_ref[...]) * (1.0 + mod_ref[1:2, :]) + mod_ref[0:1, :]
    hb = h.astype(BF16)

    def rot(t):
        return (t * cos_ref[...] + pltpu.roll(t, 96, 1) * sina_ref[...]
                + pltpu.roll(t, 32, 1) * sinb_ref[...])

    col = 0
    for o_ref, rotate in ((qa_o, rope), (ka_o, rope), (va_o, False), (qb_o, False), (kb_o, False),
                          (vb_o, False), (rb_o, False)):
        n = o_ref.shape[1]
        step = min(n, PROJ_COLS)
        for c0 in range(0, n, step):
            t = _bdot(hb, w_ref[:, col + c0:col + c0 + step])
            if rotate:
                for l0 in range(0, step, LANES):
                    o_ref[:, c0 + l0:c0 + l0 + LANES] = rot(t[:, l0:l0 + LANES]).astype(o_ref.dtype)
            else:
                o_ref[:, c0:c0 + step] = t.astype(o_ref.dtype)
        col += n
    gl = _bdot(hb, w_ref[:, col:]).astype(BF16)
    z = _bdot(gl, wup[...]) + bg[...]
    la = (jnp.minimum(z, 0.0) - jnp.log1p(jnp.exp(-jnp.abs(z)))) / GATE_TAU
    kw = laf_o.shape[1]
    laf_o[...] = la[:, :kw]
    lab_o[...] = la[:, kw:]


def _project(xf, mod3, mod_row_of_tile, tables, w_in_b, splits, wup2, bg2, *, rope, tiles_per_seq):
    r, d = xf.shape
    tm = ROW_TILE
    n_qa, n_ka, n_va, n_qb, n_kb, n_vb, n_rb, n_gate = splits
    assert sum(splits) == w_in_b.shape[1] and (sum(splits) - n_gate) % LANES == 0
    cos_t, sina_t, sinb_t = tables
    row = lambda n: pl.BlockSpec((tm, n), lambda i: (i, 0))
    tab = pl.BlockSpec((tm, LANES), lambda i: (i % tiles_per_seq, 0))
    outs = [(n_qa, BF16), (n_ka, BF16), (n_va, BF16), (n_qb, F32), (n_kb, F32), (n_vb, BF16),
            (n_rb, F32), (n_qb, F32), (n_qb, F32)]
    return pl.pallas_call(
        functools.partial(_proj_kernel, rope=rope),
        grid=(r // tm,),
        in_specs=[row(d),
                  pl.BlockSpec((None, 6, d), lambda i: (mod_row_of_tile(i), 0, 0)),
                  tab, tab, tab,
                  _resident(w_in_b.shape), _resident(wup2.shape), _resident(bg2.shape)],
        out_specs=[row(n) for n, _ in outs],
        out_shape=[jax.ShapeDtypeStruct((r, n), dt) for n, dt in outs],
        compiler_params=_params(("parallel",)),
        name="in_proj_rope" if rope else "in_proj_ctx",
    )(xf, mod3, cos_t, sina_t, sinb_t, w_in_b, wup2, bg2)


def _attn_kernel(sink_ref, q_ref, kp_ref, kc_ref, kn_ref, vp_ref, vc_ref, vn_ref, kx_ref, vx_ref, o_ref):
    n = pl.program_id(1)
    nb = pl.num_programs(1)
    blk = A_BLOCK
    rows = A_GROUP * blk
    scale = HEAD_DIM ** -0.5
    exp2_scale = scale * LOG2_E
    n_ctx = kx_ref.shape[0]
    qi = lax.broadcasted_iota(jnp.int32, (rows, blk), 0) % blk
    kj = lax.broadcasted_iota(jnp.int32, (rows, blk), 1)
    ok_prev = (kj >= qi) & (n > 0)
    ok_next = (kj <= qi) & (n < nb - 1)
    for hk in range(kp_ref.shape[1] // HEAD_DIM):
        ks = slice(hk * HEAD_DIM, (hk + 1) * HEAD_DIM)
        q4 = jnp.concatenate(
            [q_ref[:, (hk * A_GROUP + g) * HEAD_DIM:(hk * A_GROUP + g + 1) * HEAD_DIM] for g in range(A_GROUP)],
            axis=0)
        s_ctx = _bdot_nt(q4, kx_ref[:, ks])
        s_prev = jnp.where(ok_prev, _bdot_nt(q4, kp_ref[:, ks]), -jnp.inf)
        s_cur = _bdot_nt(q4, kc_ref[:, ks])
        s_next = jnp.where(ok_next, _bdot_nt(q4, kn_ref[:, ks]), -jnp.inf)
        sink = jnp.concatenate(
            [jnp.full((blk, 1), sink_ref[hk * A_GROUP + g] / scale, F32) for g in range(A_GROUP)], axis=0)
        ctx_lanes = [s_ctx[:, l0:l0 + LANES] for l0 in range(0, n_ctx, LANES)]
        m_lanes = jnp.maximum(jnp.maximum(s_prev, s_cur), s_next)
        for piece in ctx_lanes:
            m_lanes = jnp.maximum(m_lanes, piece)
        m = jnp.maximum(jnp.max(m_lanes, axis=-1, keepdims=True), sink)
        e_ctx = jnp.exp2((s_ctx - m) * exp2_scale)
        e_prev = jnp.exp2((s_prev - m) * exp2_scale)
        e_cur = jnp.exp2((s_cur - m) * exp2_scale)
        e_next = jnp.exp2((s_next - m) * exp2_scale)
        e_lanes = e_prev + e_cur + e_next
        for l0 in range(0, n_ctx, LANES):
            e_lanes = e_lanes + e_ctx[:, l0:l0 + LANES]
        den = jnp.exp2((sink - m) * exp2_scale) + jnp.sum(e_lanes, axis=-1, keepdims=True)
        o = (_bdot(e_ctx.astype(BF16), vx_ref[:, ks])
             + (_bdot(e_prev.astype(BF16), vp_ref[:, ks])
                + _bdot(e_cur.astype(BF16), vc_ref[:, ks])
                + _bdot(e_next.astype(BF16), vn_ref[:, ks]))) * (1.0 / den)
        for g in range(A_GROUP):
            hq = hk * A_GROUP + g
            o_ref[:, hq * HEAD_DIM:(hq + 1) * HEAD_DIM] = o[g * blk:(g + 1) * blk, :].astype(o_ref.dtype)


def _attention(sink, qa, ka, va, ka_c, va_c, batch, seq, n_ctx):
    nb = seq // A_BLOCK
    aw = qa.shape[1]
    kvw = ka.shape[1]
    blk = A_BLOCK
    prev = lambda b, n: (b * nb + jnp.maximum(n - 1, 0), 0)
    cur = lambda b, n: (b * nb + n, 0)
    nxt = lambda b, n: (b * nb + jnp.minimum(n + 1, nb - 1), 0)
    kv = lambda f: pl.BlockSpec((blk, kvw), f)
    ctx = pl.BlockSpec((n_ctx, kvw), lambda b, n: (b, 0))
    return pl.pallas_call(
        _attn_kernel,
        grid=(batch, nb),
        in_specs=[pl.BlockSpec(memory_space=pltpu.SMEM),
                  pl.BlockSpec((blk, aw), cur),
                  kv(prev), kv(cur), kv(nxt), kv(prev), kv(cur), kv(nxt), ctx, ctx],
        out_specs=pl.BlockSpec((blk, aw), cur),
        out_shape=jax.ShapeDtypeStruct((batch * seq, aw), BF16),
        compiler_params=_params(("parallel", "parallel")),
        name="window_gqa",
    )(sink, qa, ka, ka, ka, va, va, va, ka_c, va_c)


def _chunk_cumsum(g, *, reverse):
    rows = g.shape[0]
    p = lax.broadcasted_iota(jnp.int32, g.shape, 0) % GLA_CHUNK
    s = 1
    while s < GLA_CHUNK:
        if reverse:
            g = g + jnp.where(p < GLA_CHUNK - s, pltpu.roll(g, rows - s, 0), 0.0)
        else:
            g = g + jnp.where(p >= s, pltpu.roll(g, s, 0), 0.0)
        s *= 2
    return g


def _per_chunk_row(x, i):
    c = GLA_CHUNK
    return jnp.concatenate(
        [jnp.broadcast_to(x[j * c + i:j * c + i + 1, :], (c, x.shape[1])) for j in range(x.shape[0] // c)], axis=0)


def _gla_block(q, k, v, cum, state_t, mask, *, reverse, need_o):
    c = GLA_CHUNK
    rows, dk = k.shape
    n_chunks = rows // c
    i_last = 0 if reverse else c - 1
    i_mid = c // 2 if reverse else c // 2 - 1
    b_last = _per_chunk_row(cum, i_last)
    kdec = (k * jnp.exp(b_last - cum)).astype(BF16)
    if need_o:
        b_mid = _per_chunk_row(cum, i_mid)
        qc = q * dk ** -0.5
        qm = (qc * jnp.exp(cum - b_mid)).astype(BF16)
        km = (k * jnp.exp(b_mid - cum)).astype(BF16)
        a = jnp.where(mask, _bdot_nt(qm, km), 0.0)
        o_intra = _bdot(a.astype(BF16), v)
        qe = (qc * jnp.exp(cum)).astype(BF16)
    o_inter = [None] * n_chunks
    for j in (reversed(range(n_chunks)) if reverse else range(n_chunks)):
        rs = slice(j * c, (j + 1) * c)
        if need_o:
            o_inter[j] = _bdot_nt(qe[rs, :], state_t.astype(BF16))
        kv_t = lax.dot_general(v[rs, :], kdec[rs, :], (((0,), (0,)), ((), ())), preferred_element_type=F32)
        state_t = state_t * jnp.exp(cum[j * c + i_last:j * c + i_last + 1, :]) + kv_t
    o = o_intra + jnp.concatenate(o_inter, axis=0) if need_o else None
    return o, state_t


def _gla_kernel(*refs, reverse, final):
    if final:
        (kx_ref, vx_ref, gx_ref, q_ref, k_ref, v_ref, g_ref, ob_ref, r_ref, nw_ref, o_ref, st_ref) = refs
    else:
        (kx_ref, vx_ref, gx_ref, q_ref, k_ref, v_ref, g_ref, o_ref, st_ref) = refs
    t = pl.program_id(1)
    n_heads = st_ref.shape[0]
    dv, dk = st_ref.shape[1:]

    @pl.when(t == 0)
    def _context():
        cum = _chunk_cumsum(gx_ref[...], reverse=reverse)
        for h in range(n_heads):
            ks, vs = slice(h * dk, (h + 1) * dk), slice(h * dv, (h + 1) * dv)
            _, st = _gla_block(None, kx_ref[:, ks], vx_ref[:, vs], cum[:, ks], jnp.zeros((dv, dk), F32), None,
                               reverse=reverse, need_o=False)
            st_ref[h] = st

    @pl.when(t > 0)
    def _latent():
        rows = k_ref.shape[0]
        cum = _chunk_cumsum(g_ref[...], reverse=reverse)
        r = lax.broadcasted_iota(jnp.int32, (rows, rows), 0)
        s = lax.broadcasted_iota(jnp.int32, (rows, rows), 1)
        causal = (s >= r) if reverse else (s <= r)
        mask = causal & ((r // GLA_CHUNK) == (s // GLA_CHUNK))
        for h in range(n_heads):
            ks, vs = slice(h * dk, (h + 1) * dk), slice(h * dv, (h + 1) * dv)
            o, st = _gla_block(q_ref[:, ks], k_ref[:, ks], v_ref[:, vs], cum[:, ks], st_ref[h], mask,
                               reverse=reverse, need_o=True)
            st_ref[h] = st
            if final:
                o = o + ob_ref[:, vs]
                o = o * lax.rsqrt(jnp.mean(o * o, axis=-1, keepdims=True) + LN_EPS)
                o = o * nw_ref[:, vs]
                o = o * _silu(r_ref[:, vs])
            o_ref[:, vs] = o.astype(o_ref.dtype)


def _gla(kb_c, vb_c, la_c, qb, kb, vb, la, batch, seq, n_ctx, *, reverse, extra=None):
    assert n_ctx % GLA_CHUNK == 0 and GLA_STEP % GLA_CHUNK == 0
    tb = GLA_STEP
    nt = seq // tb
    kw, vw = qb.shape[1], vb.shape[1]
    final = extra is not None

    def lat(b, t):
        i = jnp.maximum(t - 1, 0)
        if reverse:
            i = nt - 1 - i
        return (b * nt + i, 0)

    cx = lambda b, t: (b, 0)
    in_specs = [pl.BlockSpec((n_ctx, kw), cx), pl.BlockSpec((n_ctx, vw), cx), pl.BlockSpec((n_ctx, kw), cx),
                pl.BlockSpec((tb, kw), lat), pl.BlockSpec((tb, kw), lat),
                pl.BlockSpec((tb, vw), lat), pl.BlockSpec((tb, kw), lat)]
    args = [kb_c, vb_c, la_c, qb, kb, vb, la]
    if final:
        o_other, rb, norm_w = extra
        in_specs += [pl.BlockSpec((tb, vw), lat), pl.BlockSpec((tb, vw), lat), _resident((1, vw))]
        args += [o_other, rb, norm_w]
    return pl.pallas_call(
        functools.partial(_gla_kernel, reverse=reverse, final=final),
        grid=(batch, nt + 1),
        in_specs=in_specs,
        out_specs=pl.BlockSpec((tb, vw), lat),
        out_shape=jax.ShapeDtypeStruct((batch * seq, vw), BF16 if final else F32),
        scratch_shapes=[pltpu.VMEM((B_HEADS, vw // B_HEADS, kw // B_HEADS), F32)],
        compiler_params=_params(("parallel", "arbitrary")),
        name="gla_fwd_out" if final else "gla_bwd",
    )(*args)


def _outproj_kernel(oa_ref, ob_ref, x_ref, mod_ref, wt_ref, wb_ref, lng_ref, lnb_ref, wr_ref, br_ref,
                    x1_o, h2_o, rw_o, ri_o):
    for r0 in range(0, x_ref.shape[0], SUB_ROWS):
        _outproj_rows(slice(r0, r0 + SUB_ROWS), oa_ref, ob_ref, x_ref, mod_ref, wt_ref, wb_ref, lng_ref, lnb_ref,
                      wr_ref, br_ref, x1_o, h2_o, rw_o, ri_o)


def _outproj_rows(rs, oa_ref, ob_ref, x_ref, mod_ref, wt_ref, wb_ref, lng_ref, lnb_ref, wr_ref, br_ref,
                  x1_o, h2_o, rw_o, ri_o):
    y = _bdot(oa_ref[rs, :], wt_ref[...]) + _bdot(ob_ref[rs, :], wb_ref[...])
    x1 = _ln(ALPHA * x_ref[rs, :] + mod_ref[2:3, :] * y) * lng_ref[...] + lnb_ref[...]
    x1_o[rs, :] = x1
    h2 = _ln(x1) * (1.0 + mod_ref[4:5, :]) + mod_ref[3:4, :]
    h2_o[rs, :] = h2
    lg = _bdot(h2.astype(BF16), wr_ref[...]) + br_ref[...]
    lane = lax.broadcasted_iota(jnp.int32, lg.shape, 1)
    lanef = lane.astype(F32)
    big = float(LANES)
    is_g = (lane >= N_EXPERTS) & (lane < N_EXPERTS + N_GROUPS)
    gl = jnp.where(is_g, lg, -jnp.inf)
    gmax = jnp.max(gl, axis=-1, keepdims=True)
    pg_top = 1.0 / jnp.sum(jnp.exp(gl - gmax), axis=-1, keepdims=True)
    grp = jnp.min(jnp.where(gl == gmax, lanef, big), axis=-1, keepdims=True) - N_EXPERTS
    in_grp = (lane < N_EXPERTS) & ((lane // EXPERTS_PER_GROUP).astype(F32) == grp)
    el = jnp.where(in_grp, lg, -jnp.inf)
    m1 = jnp.max(el, axis=-1, keepdims=True)
    i1 = jnp.min(jnp.where(el == m1, lanef, big), axis=-1, keepdims=True)
    el2 = jnp.where(lanef == i1, -jnp.inf, el)
    m2 = jnp.max(el2, axis=-1, keepdims=True)
    i2 = jnp.min(jnp.where(el2 == m2, lanef, big), axis=-1, keepdims=True)
    e2 = jnp.exp(m2 - m1)
    w1 = pg_top / (1.0 + e2)
    w2 = pg_top * e2 / (1.0 + e2)
    rw_o[rs, :] = jnp.where(lane == 0, w1, jnp.where(lane == 1, w2, 0.0))
    ri_o[rs, :] = jnp.where(lane == 0, i1, jnp.where(lane == 1, i2, 0.0)).astype(jnp.int32)


def _outproj(out_a, out_b, xf, mod3, w_top, w_bot, ln_g, ln_b, wr, br, seq):
    r, d = xf.shape
    tm = OUT_TILE
    assert seq % tm == 0 and tm % SUB_ROWS == 0
    tiles_per_seq = seq // tm
    row = lambda n: pl.BlockSpec((tm, n), lambda i: (i, 0))
    return pl.pallas_call(
        _outproj_kernel,
        grid=(r // tm,),
        in_specs=[row(out_a.shape[1]), row(out_b.shape[1]), row(d),
                  pl.BlockSpec((None, 6, d), lambda i: (i // tiles_per_seq, 0, 0)),
                  _resident(w_top.shape), _resident(w_bot.shape),
                  _resident((1, d)), _resident((1, d)), _resident(wr.shape), _resident(br.shape)],
        out_specs=[row(d), row(d), row(LANES), row(LANES)],
        out_shape=[jax.ShapeDtypeStruct((r, d), F32), jax.ShapeDtypeStruct((r, d), F32),
                   jax.ShapeDtypeStruct((r, LANES), F32), jax.ShapeDtypeStruct((r, LANES), jnp.int32)],
        compiler_params=_params(("parallel",)),
        name="out_proj_router",
    )(out_a, out_b, xf, mod3, w_top, w_bot, ln_g.reshape(1, d), ln_b.reshape(1, d), wr, br)


def _row_copy(src_ref, src_row, dst_ref, dst_row, sem):
    return pltpu.make_async_copy(src_ref.at[pl.ds(src_row, 1)], dst_ref.at[pl.ds(dst_row, 1)], sem)


def _dispatch_kernel(dest_ref, tail_ref, empty_ref, h_ref, buf_ref, zero_ref, hbuf, sem, zsem, lsem, rsem):
    tm = hbuf.shape[1]
    step = pl.program_id(0)

    def zero_block(b):
        row = pl.multiple_of(b * MOE_BLOCK, MOE_BLOCK)
        return pltpu.make_async_copy(zero_ref, buf_ref.at[pl.ds(row, MOE_BLOCK)], zsem)

    def for_empty_blocks(fn):
        def body(b, carry):
            @pl.when(empty_ref[b] != 0)
            def _():
                fn(b)
            return carry
        lax.fori_loop(0, empty_ref.shape[0], body, 0)

    @pl.when(step == 0)
    def _zero_fill():
        zero_ref[...] = jnp.zeros(zero_ref.shape, zero_ref.dtype)
        for e in range(N_EXPERTS):
            start = pl.multiple_of(tail_ref[e] // SUBLANES * SUBLANES, SUBLANES)
            pltpu.make_async_copy(zero_ref, buf_ref.at[pl.ds(start, MOE_BLOCK)], sem).start()
        for e in range(N_EXPERTS):
            pltpu.make_async_copy(zero_ref, buf_ref.at[pl.ds(0, MOE_BLOCK)], sem).wait()
        for_empty_blocks(lambda b: zero_block(b).start())

    @pl.when(step == pl.num_programs(0) - 1)
    def _zero_done():
        for_empty_blocks(lambda b: zero_block(b).wait())

    n_slots = hbuf.shape[0]
    slot = step % n_slots

    def load(tile, sl):
        row = pl.multiple_of(tile * tm, tm)
        return pltpu.make_async_copy(h_ref.at[pl.ds(row, tm)], hbuf.at[sl], lsem.at[sl])

    @pl.when(step == 0)
    def _():
        load(0, 0).start()

    @pl.when(step + 1 < pl.num_programs(0))
    def _():
        load(step + 1, (step + 1) % n_slots).start()

    load(step, slot).wait()

    def issue(i, carry):
        for k in range(TOP_K):
            _row_copy(hbuf.at[slot], i, buf_ref, dest_ref[0, 0, TOP_K * i + k], rsem.at[slot]).start()
        return carry

    lax.fori_loop(0, tm, issue, 0, unroll=DMA_UNROLL)

    def wait_tile(sl):
        for _ in range(TOP_K):
            pltpu.make_async_copy(hbuf.at[sl], buf_ref.at[pl.ds(0, tm)], rsem.at[sl]).wait()

    pl.when(step > 0)(lambda: wait_tile((step + n_slots - 1) % n_slots))
    pl.when(step == pl.num_programs(0) - 1)(lambda: wait_tile(slot))


def _dispatch(dest3, tail_row, empty_block, h2):
    r, d = h2.shape
    tm = ROW_TILE
    n_rows = empty_block.shape[0] * MOE_BLOCK
    return pl.pallas_call(
        _dispatch_kernel,
        grid=(r // tm,),
        in_specs=[pl.BlockSpec((1, 1, TOP_K * tm), lambda i: (i, 0, 0), memory_space=pltpu.SMEM),
                  pl.BlockSpec(memory_space=pltpu.SMEM),
                  pl.BlockSpec(memory_space=pltpu.SMEM),
                  pl.BlockSpec(memory_space=pl.ANY)],
        out_specs=pl.BlockSpec(memory_space=pl.ANY),
        out_shape=jax.ShapeDtypeStruct((n_rows, d), h2.dtype),
        scratch_shapes=[pltpu.VMEM((MOE_BLOCK, d), h2.dtype), pltpu.VMEM((3, tm, d), h2.dtype),
                        pltpu.SemaphoreType.DMA, pltpu.SemaphoreType.DMA, pltpu.SemaphoreType.DMA((3,)),
                        pltpu.SemaphoreType.DMA((3,))],
        compiler_params=_params(("arbitrary",)),
        name="moe_dispatch",
    )(dest3, tail_row, empty_block, h2)


def _expert_kernel(bq_ref, lo_ref, hi_ref, el_ref, nq_ref, ub_ref, x_ref, wg_hbm, wu_hbm, wd_hbm, o_ref,
                   wgb, wub, wdb, sg, su, sd, sems):
    b = pl.program_id(0)
    q = bq_ref[b]
    slot = q % 2
    mats = ((wg_hbm, sg, wgb), (wu_hbm, su, wub), (wd_hbm, sd, wdb))
    n_pieces = W_PIECES * len(mats)

    def piece_copy(m, qt, t):
        hbm, stage, _ = mats[m]
        pr = stage.shape[1]
        return pltpu.make_async_copy(hbm.at[el_ref[qt], pl.ds(t * pr, pr), :], stage.at[t % 2],
                                     sems.at[m, t % 2])

    def process(qt, lo, hi):
        for p in range(n_pieces):
            m, t = p % len(mats), p // len(mats)

            @pl.when((lo <= p) & (p < hi))
            def _():
                _, stage, resident = mats[m]
                pr = stage.shape[1]
                piece_copy(m, qt, t).wait()
                resident[qt % 2, pl.ds(t * pr, pr), :] = stage[t % 2].astype(BF16)
                q2 = qt + (t + 2) // W_PIECES

                @pl.when(q2 < nq_ref[0])
                def _():
                    piece_copy(m, q2, (t + 2) % W_PIECES).start()

    @pl.when(b == 0)
    def _first_expert():
        for m in range(len(mats)):
            for t in range(2):
                piece_copy(m, 0, t).start()
        process(0, 0, n_pieces)

    @pl.when(b < ub_ref[0])
    def _compute():
        xb = x_ref[...].astype(BF16)
        hid = wgb.shape[2]
        acts = []
        for h0 in range(0, hid, HID_TILE):
            hs = pl.ds(h0, HID_TILE)
            acts.append((_silu(_bdot(xb, wgb[slot, :, hs])) * _bdot(xb, wub[slot, :, hs])).astype(BF16))
        o_ref[...] = _bdot(jnp.concatenate(acts, axis=1), wdb[slot])

    @pl.when(b >= ub_ref[0])
    def _unused():
        o_ref[...] = jnp.zeros(o_ref.shape, o_ref.dtype)

    process(q + 1, lo_ref[b], hi_ref[b])


def _experts(plan, buf, wg, wu, wd, n_out_rows):
    d = buf.shape[1]
    hid = wg.shape[2]
    assert d % W_PIECES == 0 and hid % W_PIECES == 0 and W_PIECES % 2 == 0
    blk = lambda f: pl.BlockSpec((MOE_BLOCK, d), f)
    return pl.pallas_call(
        _expert_kernel,
        grid_spec=pltpu.PrefetchScalarGridSpec(
            num_scalar_prefetch=len(plan),
            grid=(n_out_rows // MOE_BLOCK,),
            in_specs=[blk(lambda b, bq, lo, hi, el, nq, ub: (jnp.minimum(b, ub[0] - 1), 0)),
                      pl.BlockSpec(memory_space=pl.ANY), pl.BlockSpec(memory_space=pl.ANY),
                      pl.BlockSpec(memory_space=pl.ANY)],
            out_specs=blk(lambda b, bq, lo, hi, el, nq, ub: (b, 0)),
            scratch_shapes=[pltpu.VMEM((2, d, hid), BF16), pltpu.VMEM((2, d, hid), BF16),
                            pltpu.VMEM((2, hid, d), BF16),
                            pltpu.VMEM((2, d // W_PIECES, hid), F32), pltpu.VMEM((2, d // W_PIECES, hid), F32),
                            pltpu.VMEM((2, hid // W_PIECES, d), F32),
                            pltpu.SemaphoreType.DMA((3, 2))],
        ),
        out_shape=jax.ShapeDtypeStruct((n_out_rows, d), F32),
        compiler_params=_params(("arbitrary",)),
        name="moe_experts",
    )(*plan, buf, wg, wu, wd)


def _combine_kernel(dest_ref, dnext_ref, y_ref, rw_ref, x1_ref, mod_ref, lng_ref, lnb_ref, o_ref,
                    g_even, g_odd, sem):
    tm = x1_ref.shape[0]
    i = pl.program_id(0)
    n_groups = tm // DMA_UNROLL
    has_next = i + 1 < pl.num_programs(0)

    def gather_rows(d_ref, g_ref, sl, r0):
        for r in range(DMA_UNROLL):
            for k in range(TOP_K):
                _row_copy(y_ref, d_ref[0, 0, TOP_K * (r0 + r) + k], g_ref.at[k], r0 + r, sem.at[sl]).start()

    def finish_rows(g_ref, r0):
        rs = pl.ds(pl.multiple_of(r0, DMA_UNROLL), DMA_UNROLL)
        f = rw_ref[rs, 0:1] * g_ref[0, rs, :] + rw_ref[rs, 1:2] * g_ref[1, rs, :]
        o_ref[rs, :] = _ln(ALPHA * x1_ref[rs, :] + mod_ref[5:6, :] * f) * lng_ref[...] + lnb_ref[...]

    @pl.when(i == 0)
    def _():
        def body(j, carry):
            gather_rows(dest_ref, g_even, 0, j * DMA_UNROLL)
            return carry
        lax.fori_loop(0, n_groups, body, 0)

    def run(sl, g_cur, g_next, prefetch):
        for k in range(TOP_K):
            pltpu.make_async_copy(y_ref.at[pl.ds(0, tm)], g_cur.at[k], sem.at[sl]).wait()

        def body(j, carry):
            if prefetch:
                gather_rows(dnext_ref, g_next, 1 - sl, j * DMA_UNROLL)
            finish_rows(g_cur, j * DMA_UNROLL)
            return carry
        lax.fori_loop(0, n_groups, body, 0, unroll=COMBINE_UNROLL)

    for sl, g_cur, g_next in ((0, g_even, g_odd), (1, g_odd, g_even)):
        for prefetch in (True, False):
            pl.when((i % 2 == sl) & (has_next == prefetch))(functools.partial(run, sl, g_cur, g_next, prefetch))


def _combine(dest3, y, rw, x1, mod3, ln_g, ln_b, tiles_per_seq):
    r, d = x1.shape
    tm = ROW_TILE
    n = r // tm
    row = lambda w: pl.BlockSpec((tm, w), lambda i: (i, 0))
    return pl.pallas_call(
        _combine_kernel,
        grid=(n,),
        in_specs=[pl.BlockSpec((1, 1, TOP_K * tm), lambda i: (i, 0, 0), memory_space=pltpu.SMEM),
                  pl.BlockSpec((1, 1, TOP_K * tm), lambda i: (jnp.minimum(i + 1, n - 1), 0, 0),
                               memory_space=pltpu.SMEM),
                  pl.BlockSpec(memory_space=pl.ANY),
                  row(LANES), row(d),
                  pl.BlockSpec((None, 6, d), lambda i: (i // tiles_per_seq, 0, 0)),
                  _resident((1, d)), _resident((1, d))],
        out_specs=row(d),
        out_shape=jax.ShapeDtypeStruct((r, d), F32),
        scratch_shapes=[pltpu.VMEM((TOP_K, tm, d), F32), pltpu.VMEM((TOP_K, tm, d), F32),
                        pltpu.SemaphoreType.DMA((2,))],
        compiler_params=_params(("arbitrary",)),
        name="moe_combine_ln",
    )(dest3, dest3, y, rw, x1, mod3, ln_g.reshape(1, d), ln_b.reshape(1, d))


def _routing_plan(ri, n_tokens):
    eid = ri[:, :TOP_K].reshape(-1)
    m = n_tokens * TOP_K
    onehot = (eid[:, None] == jnp.arange(N_EXPERTS, dtype=jnp.int32)[None, :]).astype(jnp.int32)
    csum = jnp.cumsum(onehot, axis=0)
    counts = csum[-1]
    rank = jnp.sum(csum * onehot, axis=1) - 1
    padded = (counts + MOE_BLOCK - 1) // MOE_BLOCK * MOE_BLOCK
    pad_end = jnp.cumsum(padded)
    pad_start = pad_end - padded
    dest = jnp.sum(onehot * pad_start[None, :], axis=1) + rank
    tail = pad_start + counts
    n_blocks = m // MOE_BLOCK + N_EXPERTS
    used_blocks = pad_end[-1] // MOE_BLOCK
    empty = jnp.arange(n_blocks + 1, dtype=jnp.int32) >= used_blocks

    n_pieces = 3 * W_PIECES
    has = counts > 0
    ordinal = jnp.cumsum(has.astype(jnp.int32)) - 1
    n_ord = ordinal[-1] + 1
    e_ids = jnp.arange(N_EXPERTS, dtype=jnp.int32)
    ord_expert = jnp.sum(jnp.where(has[None, :] & (ordinal[None, :] == e_ids[:, None]), e_ids[None, :], 0), axis=1)
    bid = jnp.arange(n_blocks, dtype=jnp.int32)
    brow = bid[:, None] * MOE_BLOCK
    in_e = ((pad_start[None, :] <= brow) & (brow < pad_end[None, :])).astype(jnp.int32)
    pick = lambda v: jnp.sum(in_e * v[None, :], axis=1)
    used = bid < used_blocks
    blk_q = jnp.where(used, pick(ordinal), n_ord - 1)
    i_in_e = bid - pick(pad_start) // MOE_BLOCK
    k_e = jnp.maximum(pick(padded) // MOE_BLOCK, 1)
    brings = used & (blk_q + 1 < n_ord)
    lo = jnp.where(brings, n_pieces * i_in_e // k_e, 0)
    hi = jnp.where(brings, n_pieces * (i_in_e + 1) // k_e, 0)
    i32 = lambda a: a.astype(jnp.int32)
    plan = (i32(blk_q), i32(lo), i32(hi), i32(ord_expert), i32(n_ord.reshape(1)), i32(used_blocks.reshape(1)))
    return i32(dest), i32(tail), i32(empty), plan, n_blocks * MOE_BLOCK


def _rope_tables(seq):
    n_freq = HEAD_DIM // 4
    inv_freq = ROPE_THETA ** (-jnp.arange(n_freq, dtype=F32) / n_freq)
    rows = seq // GRID_W
    ar = jnp.arange(rows, dtype=F32)[:, None] * inv_freq
    ac = jnp.arange(GRID_W, dtype=F32)[:, None] * inv_freq
    by_row = lambda t: jnp.broadcast_to(t[:, None, :], (rows, GRID_W, n_freq)).reshape(seq, n_freq)
    by_col = lambda t: jnp.broadcast_to(t[None, :, :], (rows, GRID_W, n_freq)).reshape(seq, n_freq)
    cos_r, sin_r, cos_c, sin_c = by_row(jnp.cos(ar)), by_row(jnp.sin(ar)), by_col(jnp.cos(ac)), by_col(jnp.sin(ac))
    zero = jnp.zeros_like(cos_r)
    cos_t = jnp.concatenate([cos_r, cos_r, cos_c, cos_c], axis=1)
    sin_a = jnp.concatenate([-sin_r, zero, -sin_c, zero], axis=1)
    sin_b = jnp.concatenate([zero, sin_r, zero, sin_c], axis=1)
    return cos_t, sin_a, sin_b


def kernel(x, c, ctx, c_ctx, w_ada, b_ada, w_in, w_gate_up, b_gate, attn_sink, gla_norm_w, w_out, ln1_g, ln1_b, w_router_group, b_router_group, w_router_expert, b_router_expert, w_exp_gate, w_exp_up, w_exp_down, ln2_g, ln2_b):
    batch, seq, d = x.shape
    n_ctx = ctx.shape[1]
    assert w_ada.shape[0] == DEPTH and batch < MOD_ROWS
    assert seq % ROW_TILE == 0 and (batch * n_ctx) % ROW_TILE == 0 and seq % GLA_STEP == 0 and n_ctx == GLA_STEP
    n_tok = batch * seq
    tiles_per_seq = seq // ROW_TILE
    a_width = d // 2
    kv_width = a_width // A_GROUP
    b_width = d - a_width
    key_width = b_width // 2
    layer = 0

    cc = jnp.concatenate([c, c_ctx[None, :], jnp.zeros((MOD_ROWS - batch - 1, d), F32)], axis=0)
    mod3 = _adaln(cc, w_ada[layer], b_ada[layer]).reshape(MOD_ROWS, 6, d)

    splits = (a_width, kv_width, kv_width, key_width, key_width, b_width, b_width, 2 * GATE_RANK)
    w_in_b = w_in[layer].astype(BF16)
    zero_up = jnp.zeros((GATE_RANK, key_width), F32)
    wup2 = jnp.concatenate([jnp.concatenate([w_gate_up[layer, 0], zero_up], axis=1),
                            jnp.concatenate([zero_up, w_gate_up[layer, 1]], axis=1)], axis=0).astype(BF16)
    bg2 = b_gate[layer].reshape(1, 2 * key_width)
    tables = _rope_tables(seq)

    xf = x.reshape(n_tok, d)
    qa, ka, va, qb, kb, vb, rb, la_f, la_b = _project(
        xf, mod3, lambda i: i // tiles_per_seq, tables, w_in_b, splits, wup2, bg2, rope=True,
        tiles_per_seq=tiles_per_seq)
    _, ka_c, va_c, _, kb_c, vb_c, _, lac_f, lac_b = _project(
        ctx.reshape(batch * n_ctx, d), mod3, lambda i: batch, tables, w_in_b, splits, wup2, bg2, rope=False,
        tiles_per_seq=tiles_per_seq)

    out_a = _attention(attn_sink[layer], qa, ka, va, ka_c, va_c, batch, seq, n_ctx)
    o_b = _gla(kb_c, vb_c, lac_b, qb, kb, vb, la_b, batch, seq, n_ctx, reverse=True)
    out_b = _gla(kb_c, vb_c, lac_f, qb, kb, vb, la_f, batch, seq, n_ctx, reverse=False,
                 extra=(o_b, rb, gla_norm_w[layer].reshape(1, b_width)))

    w_out_b = w_out[layer].astype(BF16)
    wr = jnp.concatenate([w_router_expert[layer], w_router_group[layer],
                          jnp.zeros((d, LANES - N_EXPERTS - N_GROUPS), F32)], axis=1).astype(BF16)
    br = jnp.concatenate([b_router_expert[layer], b_router_group[layer],
                          jnp.zeros((LANES - N_EXPERTS - N_GROUPS,), F32)]).reshape(1, LANES)
    x1, h2, rw, ri = _outproj(out_a, out_b, xf, mod3, w_out_b[:a_width], w_out_b[a_width:],
                              ln1_g[layer], ln1_b[layer], wr, br, seq)

    dest, tail_row, empty_block, plan, n_buf_rows = _routing_plan(ri, n_tok)
    dest3 = dest.reshape(n_tok // ROW_TILE, 1, TOP_K * ROW_TILE)
    buf = _dispatch(dest3, tail_row, empty_block, h2)
    y = _experts(plan, buf, w_exp_gate[layer], w_exp_up[layer], w_exp_down[layer], n_buf_rows)
    out = _combine(dest3, y, rw, x1, mod3, ln2_g[layer], ln2_b[layer], tiles_per_seq)
    return out.reshape(batch, seq, d)
```

```python
import functools

import jax
import jax.numpy as jnp
from jax import lax
from jax.experimental import pallas as pl
from jax.experimental.pallas import tpu as pltpu

F32 = jnp.float32
BF16 = jnp.bfloat16

HEAD_DIM = 128
GRID_W = 64
WINDOW = 128
A_BLOCK = 128
A_GROUP = 4
ROPE_THETA = 10000.0
B_HEADS = 4
GATE_RANK = 16
GATE_TAU = 16.0
GLA_CHUNK = 64
N_GROUPS = 4
EXPERTS_PER_GROUP = 8
N_EXPERTS = N_GROUPS * EXPERTS_PER_GROUP
TOP_K = 2
DEPTH = 1
ALPHA = (2.0 * DEPTH) ** 0.25
LN_EPS = 1e-6
LOG2_E = 1.4426950408889634
ATTN_EXP2_SCALE = HEAD_DIM ** -0.5 * LOG2_E

LANES = 128
SUBLANES = 8
MOD_ROWS = 8
VMEM_LIMIT = 56 * 1024 * 1024

ROW_TILE = 256
OUT_TILE = 512
SUB_ROWS = 256
PROJ_COLS = 1024
GLA_STEP = 256
MOE_BLOCK = 256
HID_TILE = 512
W_PIECES = 4
DMA_UNROLL = 8


def _params(sem):
    return pltpu.CompilerParams(dimension_semantics=sem, vmem_limit_bytes=VMEM_LIMIT)


def _resident(shape):
    nd = len(shape)
    return pl.BlockSpec(shape, lambda *_: (0,) * nd, pipeline_mode=pl.Buffered(1))


def _ln(x):
    mu = jnp.mean(x, axis=-1, keepdims=True)
    xc = x - mu
    var = jnp.mean(xc * xc, axis=-1, keepdims=True)
    return xc * lax.rsqrt(var + LN_EPS)


def _silu(x):
    return x * jax.nn.sigmoid(x)


def _bdot(a, b):
    return jnp.dot(a, b, preferred_element_type=F32)


def _bdot_nt(a, b):
    return lax.dot_general(a, b, (((1,), (1,)), ((), ())), preferred_element_type=F32)


def _adaln_kernel(c_ref, w_ref, b_ref, o_ref):
    s = _silu(c_ref[...]).astype(BF16)
    o_ref[...] = _bdot(s, w_ref[...].astype(BF16)) + b_ref[...]


def _adaln(cc, w_ada, b_ada):
    d, n = w_ada.shape
    tn = 1024
    return pl.pallas_call(
        _adaln_kernel,
        grid=(n // tn,),
        in_specs=[pl.BlockSpec((MOD_ROWS, d), lambda j: (0, 0)),
                  pl.BlockSpec((d, tn), lambda j: (0, j)),
                  pl.BlockSpec((1, tn), lambda j: (0, j))],
        out_specs=pl.BlockSpec((MOD_ROWS, tn), lambda j: (0, j)),
        out_shape=jax.ShapeDtypeStruct((MOD_ROWS, n), F32),
        compiler_params=_params(("arbitrary",)),
        name="adaln",
    )(cc, w_ada, b_ada.reshape(1, n))


def _proj_kernel(x_ref, mod_ref, cos_ref, sina_ref, sinb_ref, w_ref, wup, bg, *out_refs, groups, rope):
    for r0 in range(0, x_ref.shape[0], SUB_ROWS):
        _proj_rows(slice(r0, r0 + SUB_ROWS), x_ref, mod_ref, cos_ref, sina_ref, sinb_ref, w_ref, wup, bg,
                   out_refs, groups, rope)


def _proj_rows(rs, x_ref, mod_ref, cos_ref, sina_ref, sinb_ref, w_ref, wup, bg, out_refs, groups, rope):
    h = _ln(x_ref[rs, :]) * (1.0 + mod_ref[1:2, :]) + mod_ref[0:1, :]
    hb = h.astype(BF16)

    def rot(t):
        return (t * cos_ref[rs, :] + pltpu.roll(t, 96, 1) * sina_ref[rs, :]
                + pltpu.roll(t, 32, 1) * sinb_ref[rs, :])

    col = 0
    refs = iter(out_refs)
    for n, wanted, rotary, post in groups:
        if wanted:
            o_ref = next(refs)
            step = min(n, PROJ_COLS)
            for c0 in range(0, n, step):
                t = _bdot(hb, w_ref[:, col + c0:col + c0 + step])
                if rotary and rope:
                    for l0 in range(0, step, LANES):
                        r = rot(t[:, l0:l0 + LANES])
                        o_ref[rs, c0 + l0:c0 + l0 + LANES] = (r if post == 1.0 else r * post).astype(o_ref.dtype)
                else:
                    o_ref[rs, c0:c0 + step] = t.astype(o_ref.dtype)
        col += n
    laf_o, lab_o = refs
    gl = _bdot(hb, w_ref[:, col:]).astype(BF16)
    z = _bdot(gl, wup[...]) + bg[...]
    la = (jnp.minimum(z, 0.0) - jnp.log1p(jnp.exp(-jnp.abs(z)))) / GATE_TAU
    kw = laf_o.shape[1]
    laf_o[rs, :] = la[:, :kw]
    lab_o[rs, :] = la[:, kw:]


def _project(xf, mod3, mod_row, tables, w_in_b, splits, wup2, bg2, *, rope, seq, keys_values_only=False):
    r, d = xf.shape
    tm = OUT_TILE
    assert r % tm == 0 and seq % tm == 0 and tm % SUB_ROWS == 0
    tiles_per_seq = seq // tm
    mod_row_of_tile = lambda i: mod_row(i, tiles_per_seq)
    n_gate = splits[-1]
    assert sum(splits) == w_in_b.shape[1] and (sum(splits) - n_gate) % LANES == 0
    kv = keys_values_only
    spec = ((BF16, not kv, True, ATTN_EXP2_SCALE), (BF16, True, True, 1.0), (BF16, True, False, 1.0),
            (F32, not kv, False, 1.0), (F32, True, False, 1.0), (BF16, True, False, 1.0), (F32, not kv, False, 1.0))
    groups = tuple((n, wanted, rotary, post) for n, (_, wanted, rotary, post) in zip(splits, spec))
    key_width = splits[3]
    outs = [(n, dt) for n, (dt, wanted, _, _) in zip(splits, spec) if wanted] + [(key_width, F32)] * 2
    cos_t, sina_t, sinb_t = tables
    row = lambda n: pl.BlockSpec((tm, n), lambda i: (i, 0))
    tab = pl.BlockSpec((tm, LANES), lambda i: (i % tiles_per_seq, 0))
    res = pl.pallas_call(
        functools.partial(_proj_kernel, groups=groups, rope=rope),
        grid=(r // tm,),
        in_specs=[row(d),
                  pl.BlockSpec((None, 6, d), lambda i: (mod_row_of_tile(i), 0, 0)),
                  tab, tab, tab,
                  _resident(w_in_b.shape), _resident(wup2.shape), _resident(bg2.shape)],
        out_specs=[row(n) for n, _ in outs],
        out_shape=[jax.ShapeDtypeStruct((r, n), dt) for n, dt in outs],
        compiler_params=_params(("parallel",)),
        name="in_proj_rope" if rope else "in_proj_ctx",
    )(xf, mod3, cos_t, sina_t, sinb_t, w_in_b, wup2, bg2)
    res = iter(res)
    return tuple(next(res) if wanted else None for _, wanted, _, _ in spec) + tuple(res)


def _attn_kernel(sink_ref, q_ref, kp_ref, kc_ref, kn_ref, vp_ref, vc_ref, vn_ref, kx_ref, vx_ref, o_ref):
    n = pl.program_id(1)
    nb = pl.num_programs(1)
    blk = A_BLOCK
    rows = A_GROUP * blk
    n_ctx = kx_ref.shape[0]
    qi = lax.broadcasted_iota(jnp.int32, (rows, blk), 0) % blk
    kj = lax.broadcasted_iota(jnp.int32, (rows, blk), 1)
    ok_prev = (kj >= qi) & (n > 0)
    ok_next = (kj <= qi) & (n < nb - 1)
    for hk in range(kp_ref.shape[1] // HEAD_DIM):
        ks = slice(hk * HEAD_DIM, (hk + 1) * HEAD_DIM)
        q4 = jnp.concatenate(
            [q_ref[:, (hk * A_GROUP + g) * HEAD_DIM:(hk * A_GROUP + g + 1) * HEAD_DIM] for g in range(A_GROUP)],
            axis=0)
        s_ctx = _bdot_nt(q4, kx_ref[:, ks])
        s_prev = jnp.where(ok_prev, _bdot_nt(q4, kp_ref[:, ks]), -jnp.inf)
        s_cur = _bdot_nt(q4, kc_ref[:, ks])
        s_next = jnp.where(ok_next, _bdot_nt(q4, kn_ref[:, ks]), -jnp.inf)
        sink = jnp.concatenate(
            [jnp.full((blk, 1), sink_ref[hk * A_GROUP + g] * LOG2_E, F32) for g in range(A_GROUP)], axis=0)
        ctx_lanes = [s_ctx[:, l0:l0 + LANES] for l0 in range(0, n_ctx, LANES)]
        m_lanes = jnp.maximum(jnp.maximum(s_prev, s_cur), s_next)
        for piece in ctx_lanes:
            m_lanes = jnp.maximum(m_lanes, piece)
        m = jnp.maximum(jnp.max(m_lanes, axis=-1, keepdims=True), sink)
        e_ctx = jnp.exp2(s_ctx - m)
        e_prev = jnp.exp2(s_prev - m)
        e_cur = jnp.exp2(s_cur - m)
        e_next = jnp.exp2(s_next - m)
        e_lanes = e_prev + e_cur + e_next
        for l0 in range(0, n_ctx, LANES):
            e_lanes = e_lanes + e_ctx[:, l0:l0 + LANES]
        den = jnp.exp2(sink - m) + jnp.sum(e_lanes, axis=-1, keepdims=True)
        o = (_bdot(e_ctx.astype(BF16), vx_ref[:, ks])
             + (_bdot(e_prev.astype(BF16), vp_ref[:, ks])
                + _bdot(e_cur.astype(BF16), vc_ref[:, ks])
                + _bdot(e_next.astype(BF16), vn_ref[:, ks]))) * (1.0 / den)
        for g in range(A_GROUP):
            hq = hk * A_GROUP + g
            o_ref[:, hq * HEAD_DIM:(hq + 1) * HEAD_DIM] = o[g * blk:(g + 1) * blk, :].astype(o_ref.dtype)


def _attention(sink, qa, ka, va, ka_c, va_c, batch, seq, n_ctx):
    nb = seq // A_BLOCK
    aw = qa.shape[1]
    kvw = ka.shape[1]
    blk = A_BLOCK
    prev = lambda b, n: (b * nb + jnp.maximum(n - 1, 0), 0)
    cur = lambda b, n: (b * nb + n, 0)
    nxt = lambda b, n: (b * nb + jnp.minimum(n + 1, nb - 1), 0)
    kv = lambda f: pl.BlockSpec((blk, kvw), f)
    ctx = pl.BlockSpec((n_ctx, kvw), lambda b, n: (b, 0))
    return pl.pallas_call(
        _attn_kernel,
        grid=(batch, nb),
        in_specs=[pl.BlockSpec(memory_space=pltpu.SMEM),
                  pl.BlockSpec((blk, aw), cur),
                  kv(prev), kv(cur), kv(nxt), kv(prev), kv(cur), kv(nxt), ctx, ctx],
        out_specs=pl.BlockSpec((blk, aw), cur),
        out_shape=jax.ShapeDtypeStruct((batch * seq, aw), BF16),
        compiler_params=_params(("parallel", "parallel")),
        name="window_gqa",
    )(sink, qa, ka, ka, ka, va, va, va, ka_c, va_c)


def _chunk_cumsum(g, *, reverse):
    rows = g.shape[0]
    p = lax.broadcasted_iota(jnp.int32, g.shape, 0) % GLA_CHUNK
    s = 1
    while s < GLA_CHUNK:
        if reverse:
            g = g + jnp.where(p < GLA_CHUNK - s, pltpu.roll(g, rows - s, 0), 0.0)
        else:
            g = g + jnp.where(p >= s, pltpu.roll(g, s, 0), 0.0)
        s *= 2
    return g


def _per_chunk_row(x, i):
    c = GLA_CHUNK
    return jnp.concatenate(
        [jnp.broadcast_to(x[j * c + i:j * c + i + 1, :], (c, x.shape[1])) for j in range(x.shape[0] // c)], axis=0)


def _gla_block(q, k, v, cum, state_t, mask, *, reverse, need_o):
    c = GLA_CHUNK
    rows, dk = k.shape
    n_chunks = rows // c
    i_last = 0 if reverse else c - 1
    i_mid = c // 2 if reverse else c // 2 - 1
    b_last = _per_chunk_row(cum, i_last)
    kdec = (k * jnp.exp(b_last - cum)).astype(BF16)
    if need_o:
        b_mid = _per_chunk_row(cum, i_mid)
        qc = q * dk ** -0.5
        qm = (qc * jnp.exp(cum - b_mid)).astype(BF16)
        km = (k * jnp.exp(b_mid - cum)).astype(BF16)
        a = jnp.where(mask, _bdot_nt(qm, km), 0.0)
        o_intra = _bdot(a.astype(BF16), v)
        qe = (qc * jnp.exp(cum)).astype(BF16)
    o_inter = [None] * n_chunks
    for j in (reversed(range(n_chunks)) if reverse else range(n_chunks)):
        rs = slice(j * c, (j + 1) * c)
        if need_o:
            o_inter[j] = _bdot_nt(qe[rs, :], state_t.astype(BF16))
        kv_t = lax.dot_general(v[rs, :], kdec[rs, :], (((0,), (0,)), ((), ())), preferred_element_type=F32)
        state_t = state_t * jnp.exp(cum[j * c + i_last:j * c + i_last + 1, :]) + kv_t
    o = o_intra + jnp.concatenate(o_inter, axis=0) if need_o else None
    return o, state_t


def _gla_kernel(*refs, reverse, final):
    if final:
        (kx_ref, vx_ref, gx_ref, q_ref, k_ref, v_ref, g_ref, ob_ref, r_ref, nw_ref, o_ref, st_ref) = refs
    else:
        (kx_ref, vx_ref, gx_ref, q_ref, k_ref, v_ref, g_ref, o_ref, st_ref) = refs
    t = pl.program_id(1)
    n_heads = st_ref.shape[0]
    dv, dk = st_ref.shape[1:]

    @pl.when(t == 0)
    def _context():
        cum = _chunk_cumsum(gx_ref[...], reverse=reverse)
        for h in range(n_heads):
            ks, vs = slice(h * dk, (h + 1) * dk), slice(h * dv, (h + 1) * dv)
            _, st = _gla_block(None, kx_ref[:, ks], vx_ref[:, vs], cum[:, ks], jnp.zeros((dv, dk), F32), None,
                               reverse=reverse, need_o=False)
            st_ref[h] = st

    @pl.when(t > 0)
    def _latent():
        rows = k_ref.shape[0]
        cum = _chunk_cumsum(g_ref[...], reverse=reverse)
        r = lax.broadcasted_iota(jnp.int32, (rows, rows), 0)
        s = lax.broadcasted_iota(jnp.int32, (rows, rows), 1)
        causal = (s >= r) if reverse else (s <= r)
        mask = causal & ((r // GLA_CHUNK) == (s // GLA_CHUNK))
        for h in range(n_heads):
            ks, vs = slice(h * dk, (h + 1) * dk), slice(h * dv, (h + 1) * dv)
            o, st = _gla_block(q_ref[:, ks], k_ref[:, ks], v_ref[:, vs], cum[:, ks], st_ref[h], mask,
                               reverse=reverse, need_o=True)
            st_ref[h] = st
            if final:
                o = o + ob_ref[:, vs]
                o = o * lax.rsqrt(jnp.mean(o * o, axis=-1, keepdims=True) + LN_EPS)
                o = o * nw_ref[:, vs]
                o = o * _silu(r_ref[:, vs])
            o_ref[:, vs] = o.astype(o_ref.dtype)


def _gla(kb_c, vb_c, la_c, qb, kb, vb, la, batch, seq, n_ctx, *, reverse, extra=None):
    assert n_ctx % GLA_CHUNK == 0 and GLA_STEP % GLA_CHUNK == 0
    tb = GLA_STEP
    nt = seq // tb
    kw, vw = qb.shape[1], vb.shape[1]
    final = extra is not None

    def lat(b, t):
        i = jnp.maximum(t - 1, 0)
        if reverse:
            i = nt - 1 - i
        return (b * nt + i, 0)

    cx = lambda b, t: (b, 0)
    in_specs = [pl.BlockSpec((n_ctx, kw), cx), pl.BlockSpec((n_ctx, vw), cx), pl.BlockSpec((n_ctx, kw), cx),
                pl.BlockSpec((tb, kw), lat), pl.BlockSpec((tb, kw), lat),
                pl.BlockSpec((tb, vw), lat), pl.BlockSpec((tb, kw), lat)]
    args = [kb_c, vb_c, la_c, qb, kb, vb, la]
    if final:
        o_other, rb, norm_w = extra
        in_specs += [pl.BlockSpec((tb, vw), lat), pl.BlockSpec((tb, vw), lat), _resident((1, vw))]
        args += [o_other, rb, norm_w]
    return pl.pallas_call(
        functools.partial(_gla_kernel, reverse=reverse, final=final),
        grid=(batch, nt + 1),
        in_specs=in_specs,
        out_specs=pl.BlockSpec((tb, vw), lat),
        out_shape=jax.ShapeDtypeStruct((batch * seq, vw), BF16 if final else F32),
        scratch_shapes=[pltpu.VMEM((B_HEADS, vw // B_HEADS, kw // B_HEADS), F32)],
        compiler_params=_params(("parallel", "arbitrary")),
        name="gla_fwd_out" if final else "gla_bwd",
    )(*args)


def _outproj_kernel(oa_ref, ob_ref, x_ref, mod_ref, wt_ref, wb_ref, lng_ref, lnb_ref, wr_ref, br_ref,
                    x1_o, h2_o, rw_o, ri_o):
    for r0 in range(0, x_ref.shape[0], SUB_ROWS):
        _outproj_rows(slice(r0, r0 + SUB_ROWS), oa_ref, ob_ref, x_ref, mod_ref, wt_ref, wb_ref, lng_ref, lnb_ref,
                      wr_ref, br_ref, x1_o, h2_o, rw_o, ri_o)


def _outproj_rows(rs, oa_ref, ob_ref, x_ref, mod_ref, wt_ref, wb_ref, lng_ref, lnb_ref, wr_ref, br_ref,
                  x1_o, h2_o, rw_o, ri_o):
    y = _bdot(oa_ref[rs, :], wt_ref[...]) + _bdot(ob_ref[rs, :], wb_ref[...])
    x1 = _ln(ALPHA * x_ref[rs, :] + mod_ref[2:3, :] * y) * lng_ref[...] + lnb_ref[...]
    x1_o[rs, :] = x1
    h2 = _ln(x1) * (1.0 + mod_ref[4:5, :]) + mod_ref[3:4, :]
    h2_o[rs, :] = h2
    lg = _bdot(h2.astype(BF16), wr_ref[...]) + br_ref[...]
    lane = lax.broadcasted_iota(jnp.int32, lg.shape, 1)
    lanef = lane.astype(F32)
    big = float(LANES)
    is_g = (lane >= N_EXPERTS) & (lane < N_EXPERTS + N_GROUPS)
    gl = jnp.where(is_g, lg, -jnp.inf)
    gmax = jnp.max(gl, axis=-1, keepdims=True)
    pg_top = 1.0 / jnp.sum(jnp.exp(gl - gmax), axis=-1, keepdims=True)
    grp = jnp.min(jnp.where(gl == gmax, lanef, big), axis=-1, keepdims=True) - N_EXPERTS
    in_grp = (lane < N_EXPERTS) & ((lane // EXPERTS_PER_GROUP).astype(F32) == grp)
    el = jnp.where(in_grp, lg, -jnp.inf)
    m1 = jnp.max(el, axis=-1, keepdims=True)
    i1 = jnp.min(jnp.where(el == m1, lanef, big), axis=-1, keepdims=True)
    el2 = jnp.where(lanef == i1, -jnp.inf, el)
    m2 = jnp.max(el2, axis=-1, keepdims=True)
    i2 = jnp.min(jnp.where(el2 == m2, lanef, big), axis=-1, keepdims=True)
    e2 = jnp.exp(m2 - m1)
    w1 = pg_top / (1.0 + e2)
    w2 = pg_top * e2 / (1.0 + e2)
    rw_o[rs, :] = jnp.where(lane == 0, w1, jnp.where(lane == 1, w2, 0.0))
    ri_o[rs, :] = jnp.where(lane == 0, i1, jnp.where(lane == 1, i2, 0.0)).astype(jnp.int32)


def _outproj(out_a, out_b, xf, mod3, w_top, w_bot, ln_g, ln_b, wr, br, seq):
    r, d = xf.shape
    tm = OUT_TILE
    assert seq % tm == 0 and tm % SUB_ROWS == 0
    tiles_per_seq = seq // tm
    row = lambda n: pl.BlockSpec((tm, n), lambda i: (i, 0))
    return pl.pallas_call(
        _outproj_kernel,
        grid=(r // tm,),
        in_specs=[row(out_a.shape[1]), row(out_b.shape[1]), row(d),
                  pl.BlockSpec((None, 6, d), lambda i: (i // tiles_per_seq, 0, 0)),
                  _resident(w_top.shape), _resident(w_bot.shape),
                  _resident((1, d)), _resident((1, d)), _resident(wr.shape), _resident(br.shape)],
        out_specs=[row(d), row(d), row(LANES), row(LANES)],
        out_shape=[jax.ShapeDtypeStruct((r, d), F32), jax.ShapeDtypeStruct((r, d), F32),
                   jax.ShapeDtypeStruct((r, LANES), F32), jax.ShapeDtypeStruct((r, LANES), jnp.int32)],
        compiler_params=_params(("parallel",)),
        name="out_proj_router",
    )(out_a, out_b, xf, mod3, w_top, w_bot, ln_g.reshape(1, d), ln_b.reshape(1, d), wr, br)


def _row_copy(src_ref, src_row, dst_ref, dst_row, sem):
    return pltpu.make_async_copy(src_ref.at[pl.ds(src_row, 1)], dst_ref.at[pl.ds(dst_row, 1)], sem)


def _dispatch_kernel(dest_ref, tail_ref, empty_ref, h_ref, buf_ref, zero_ref, hbuf, sem, zsem, lsem, rsem):
    tm = hbuf.shape[1]
    step = pl.program_id(0)

    def zero_block(b):
        row = pl.multiple_of(b * MOE_BLOCK, MOE_BLOCK)
        return pltpu.make_async_copy(zero_ref, buf_ref.at[pl.ds(row, MOE_BLOCK)], zsem)

    def for_empty_blocks(fn):
        def body(b, carry):
            @pl.when(empty_ref[b] != 0)
            def _():
                fn(b)
            return carry
        lax.fori_loop(0, empty_ref.shape[0], body, 0)

    @pl.when(step == 0)
    def _zero_fill():
        zero_ref[...] = jnp.zeros(zero_ref.shape, zero_ref.dtype)
        for e in range(N_EXPERTS):
            start = pl.multiple_of(tail_ref[e] // SUBLANES * SUBLANES, SUBLANES)
            pltpu.make_async_copy(zero_ref, buf_ref.at[pl.ds(start, MOE_BLOCK)], sem).start()
        for e in range(N_EXPERTS):
            pltpu.make_async_copy(zero_ref, buf_ref.at[pl.ds(0, MOE_BLOCK)], sem).wait()
        for_empty_blocks(lambda b: zero_block(b).start())

    @pl.when(step == pl.num_programs(0) - 1)
    def _zero_done():
        for_empty_blocks(lambda b: zero_block(b).wait())

    n_slots = hbuf.shape[0]
    slot = step % n_slots

    def load(tile, sl):
        row = pl.multiple_of(tile * tm, tm)
        return pltpu.make_async_copy(h_ref.at[pl.ds(row, tm)], hbuf.at[sl], lsem.at[sl])

    @pl.when(step == 0)
    def _():
        load(0, 0).start()

    @pl.when(step + 1 < pl.num_programs(0))
    def _():
        load(step + 1, (step + 1) % n_slots).start()

    load(step, slot).wait()

    def issue(i, carry):
        for k in range(TOP_K):
            _row_copy(hbuf.at[slot], i, buf_ref, dest_ref[0, 0, TOP_K * i + k], rsem.at[slot]).start()
        return carry

    lax.fori_loop(0, tm, issue, 0, unroll=DMA_UNROLL)

    def wait_tile(sl):
        for _ in range(TOP_K):
            pltpu.make_async_copy(hbuf.at[sl], buf_ref.at[pl.ds(0, tm)], rsem.at[sl]).wait()

    pl.when(step > 0)(lambda: wait_tile((step + n_slots - 1) % n_slots))
    pl.when(step == pl.num_programs(0) - 1)(lambda: wait_tile(slot))


def _dispatch(dest3, tail_row, empty_block, h2):
    r, d = h2.shape
    tm = ROW_TILE
    n_rows = empty_block.shape[0] * MOE_BLOCK
    return pl.pallas_call(
        _dispatch_kernel,
        grid=(r // tm,),
        in_specs=[pl.BlockSpec((1, 1, TOP_K * tm), lambda i: (i, 0, 0), memory_space=pltpu.SMEM),
                  pl.BlockSpec(memory_space=pltpu.SMEM),
                  pl.BlockSpec(memory_space=pltpu.SMEM),
                  pl.BlockSpec(memory_space=pl.ANY)],
        out_specs=pl.BlockSpec(memory_space=pl.ANY),
        out_shape=jax.ShapeDtypeStruct((n_rows, d), h2.dtype),
        scratch_shapes=[pltpu.VMEM((MOE_BLOCK, d), h2.dtype), pltpu.VMEM((3, tm, d), h2.dtype),
                        pltpu.SemaphoreType.DMA, pltpu.SemaphoreType.DMA, pltpu.SemaphoreType.DMA((3,)),
                        pltpu.SemaphoreType.DMA((3,))],
        compiler_params=_params(("arbitrary",)),
        name="moe_dispatch",
    )(dest3, tail_row, empty_block, h2)


def _expert_kernel(bq_ref, lo_ref, hi_ref, el_ref, nq_ref, ub_ref, x_ref, wg_hbm, wu_hbm, wd_hbm, o_ref,
                   wgb, wub, wdb, sg, su, sd, sems):
    b = pl.program_id(0)
    q = bq_ref[b]
    slot = q % 2
    mats = ((wg_hbm, sg, wgb), (wu_hbm, su, wub), (wd_hbm, sd, wdb))
    n_pieces = W_PIECES * len(mats)

    def piece_copy(m, qt, t):
        hbm, stage, _ = mats[m]
        pr = stage.shape[1]
        return pltpu.make_async_copy(hbm.at[el_ref[qt], pl.ds(t * pr, pr), :], stage.at[t % 2],
                                     sems.at[m, t % 2])

    def process(qt, lo, hi):
        for p in range(n_pieces):
            m, t = p % len(mats), p // len(mats)

            @pl.when((lo <= p) & (p < hi))
            def _():
                _, stage, resident = mats[m]
                pr = stage.shape[1]
                piece_copy(m, qt, t).wait()
                resident[qt % 2, pl.ds(t * pr, pr), :] = stage[t % 2].astype(BF16)
                q2 = qt + (t + 2) // W_PIECES

                @pl.when(q2 < nq_ref[0])
                def _():
                    piece_copy(m, q2, (t + 2) % W_PIECES).start()

    @pl.when(b == 0)
    def _first_expert():
        for m in range(len(mats)):
            for t in range(2):
                piece_copy(m, 0, t).start()
        process(0, 0, n_pieces)

    @pl.when(b < ub_ref[0])
    def _compute():
        xb = x_ref[...].astype(BF16)
        hid = wgb.shape[2]
        acts = []
        for h0 in range(0, hid, HID_TILE):
            hs = pl.ds(h0, HID_TILE)
            acts.append((_silu(_bdot(xb, wgb[slot, :, hs])) * _bdot(xb, wub[slot, :, hs])).astype(BF16))
        o_ref[...] = _bdot(jnp.concatenate(acts, axis=1), wdb[slot])

    @pl.when(b >= ub_ref[0])
    def _unused():
        o_ref[...] = jnp.zeros(o_ref.shape, o_ref.dtype)

    process(q + 1, lo_ref[b], hi_ref[b])


def _experts(plan, buf, wg, wu, wd, n_out_rows):
    d = buf.shape[1]
    hid = wg.shape[2]
    assert d % W_PIECES == 0 and hid % W_PIECES == 0 and W_PIECES % 2 == 0
    blk = lambda f: pl.BlockSpec((MOE_BLOCK, d), f)
    return pl.pallas_call(
        _expert_kernel,
        grid_spec=pltpu.PrefetchScalarGridSpec(
            num_scalar_prefetch=len(plan),
            grid=(n_out_rows // MOE_BLOCK,),
            in_specs=[blk(lambda b, bq, lo, hi, el, nq, ub: (jnp.minimum(b, ub[0] - 1), 0)),
                      pl.BlockSpec(memory_space=pl.ANY), pl.BlockSpec(memory_space=pl.ANY),
                      pl.BlockSpec(memory_space=pl.ANY)],
            out_specs=blk(lambda b, bq, lo, hi, el, nq, ub: (b, 0)),
            scratch_shapes=[pltpu.VMEM((2, d, hid), BF16), pltpu.VMEM((2, d, hid), BF16),
                            pltpu.VMEM((2, hid, d), BF16),
                            pltpu.VMEM((2, d // W_PIECES, hid), F32), pltpu.VMEM((2, d // W_PIECES, hid), F32),
                            pltpu.VMEM((2, hid // W_PIECES, d), F32),
                            pltpu.SemaphoreType.DMA((3, 2))],
        ),
        out_shape=jax.ShapeDtypeStruct((n_out_rows, d), F32),
        compiler_params=_params(("arbitrary",)),
        name="moe_experts",
    )(*plan, buf, wg, wu, wd)


def _combine_kernel(dest_ref, dnext_ref, y_ref, rw_ref, x1_ref, mod_ref, lng_ref, lnb_ref, o_ref,
                    g_even, g_odd, sem):
    tm = x1_ref.shape[0]
    i = pl.program_id(0)
    n_groups = tm // DMA_UNROLL
    has_next = i + 1 < pl.num_programs(0)

    def gather(d_ref, g_ref, sl):
        def body(j, carry):
            for r in range(DMA_UNROLL):
                row = j * DMA_UNROLL + r
                for k in range(TOP_K):
                    _row_copy(y_ref, d_ref[0, 0, TOP_K * row + k], g_ref.at[k], row, sem.at[sl]).start()
            return carry
        lax.fori_loop(0, n_groups, body, 0)

    @pl.when(i == 0)
    def _():
        gather(dest_ref, g_even, 0)

    def run(sl, g_cur, g_next, prefetch):
        if prefetch:
            gather(dnext_ref, g_next, 1 - sl)
        for k in range(TOP_K):
            pltpu.make_async_copy(y_ref.at[pl.ds(0, tm)], g_cur.at[k], sem.at[sl]).wait()
        f = rw_ref[:, 0:1] * g_cur[0] + rw_ref[:, 1:2] * g_cur[1]
        o_ref[...] = _ln(ALPHA * x1_ref[...] + mod_ref[5:6, :] * f) * lng_ref[...] + lnb_ref[...]

    for sl, g_cur, g_next in ((0, g_even, g_odd), (1, g_odd, g_even)):
        for prefetch in (True, False):
            pl.when((i % 2 == sl) & (has_next == prefetch))(functools.partial(run, sl, g_cur, g_next, prefetch))


def _combine(dest3, y, rw, x1, mod3, ln_g, ln_b, tiles_per_seq):
    r, d = x1.shape
    tm = ROW_TILE
    n = r // tm
    row = lambda w: pl.BlockSpec((tm, w), lambda i: (i, 0))
    return pl.pallas_call(
        _combine_kernel,
        grid=(n,),
        in_specs=[pl.BlockSpec((1, 1, TOP_K * tm), lambda i: (i, 0, 0), memory_space=pltpu.SMEM),
                  pl.BlockSpec((1, 1, TOP_K * tm), lambda i: (jnp.minimum(i + 1, n - 1), 0, 0),
                               memory_space=pltpu.SMEM),
                  pl.BlockSpec(memory_space=pl.ANY),
                  row(LANES), row(d),
                  pl.BlockSpec((None, 6, d), lambda i: (i // tiles_per_seq, 0, 0)),
                  _resident((1, d)), _resident((1, d))],
        out_specs=row(d),
        out_shape=jax.ShapeDtypeStruct((r, d), F32),
        scratch_shapes=[pltpu.VMEM((TOP_K, tm, d), F32), pltpu.VMEM((TOP_K, tm, d), F32),
                        pltpu.SemaphoreType.DMA((2,))],
        compiler_params=_params(("arbitrary",)),
        name="moe_combine_ln",
    )(dest3, dest3, y, rw, x1, mod3, ln_g.reshape(1, d), ln_b.reshape(1, d))


def _routing_plan(ri, n_tokens):
    eid = ri[:, :TOP_K].reshape(-1)
    m = n_tokens * TOP_K
    onehot = (eid[:, None] == jnp.arange(N_EXPERTS, dtype=jnp.int32)[None, :]).astype(jnp.int32)
    csum = jnp.cumsum(onehot, axis=0)
    counts = csum[-1]
    rank = jnp.sum(csum * onehot, axis=1) - 1
    padded = (counts + MOE_BLOCK - 1) // MOE_BLOCK * MOE_BLOCK
    pad_end = jnp.cumsum(padded)
    pad_start = pad_end - padded
    dest = jnp.sum(onehot * pad_start[None, :], axis=1) + rank
    tail = pad_start + counts
    n_blocks = m // MOE_BLOCK + N_EXPERTS
    used_blocks = pad_end[-1] // MOE_BLOCK
    empty = jnp.arange(n_blocks + 1, dtype=jnp.int32) >= used_blocks

    n_pieces = 3 * W_PIECES
    has = counts > 0
    ordinal = jnp.cumsum(has.astype(jnp.int32)) - 1
    n_ord = ordinal[-1] + 1
    e_ids = jnp.arange(N_EXPERTS, dtype=jnp.int32)
    ord_expert = jnp.sum(jnp.where(has[None, :] & (ordinal[None, :] == e_ids[:, None]), e_ids[None, :], 0), axis=1)
    bid = jnp.arange(n_blocks, dtype=jnp.int32)
    brow = bid[:, None] * MOE_BLOCK
    in_e = ((pad_start[None, :] <= brow) & (brow < pad_end[None, :])).astype(jnp.int32)
    pick = lambda v: jnp.sum(in_e * v[None, :], axis=1)
    used = bid < used_blocks
    blk_q = jnp.where(used, pick(ordinal), n_ord - 1)
    i_in_e = bid - pick(pad_start) // MOE_BLOCK
    k_e = jnp.maximum(pick(padded) // MOE_BLOCK, 1)
    brings = used & (blk_q + 1 < n_ord)
    lo = jnp.where(brings, n_pieces * i_in_e // k_e, 0)
    hi = jnp.where(brings, n_pieces * (i_in_e + 1) // k_e, 0)
    i32 = lambda a: a.astype(jnp.int32)
    plan = (i32(blk_q), i32(lo), i32(hi), i32(ord_expert), i32(n_ord.reshape(1)), i32(used_blocks.reshape(1)))
    return i32(dest), i32(tail), i32(empty), plan, n_blocks * MOE_BLOCK


def _rope_tables(seq):
    n_freq = HEAD_DIM // 4
    inv_freq = ROPE_THETA ** (-jnp.arange(n_freq, dtype=F32) / n_freq)
    rows = seq // GRID_W
    ar = jnp.arange(rows, dtype=F32)[:, None] * inv_freq
    ac = jnp.arange(GRID_W, dtype=F32)[:, None] * inv_freq
    by_row = lambda t: jnp.broadcast_to(t[:, None, :], (rows, GRID_W, n_freq)).reshape(seq, n_freq)
    by_col = lambda t: jnp.broadcast_to(t[None, :, :], (rows, GRID_W, n_freq)).reshape(seq, n_freq)
    cos_r, sin_r, cos_c, sin_c = by_row(jnp.cos(ar)), by_row(jnp.sin(ar)), by_col(jnp.cos(ac)), by_col(jnp.sin(ac))
    zero = jnp.zeros_like(cos_r)
    cos_t = jnp.concatenate([cos_r, cos_r, cos_c, cos_c], axis=1)
    sin_a = jnp.concatenate([-sin_r, zero, -sin_c, zero], axis=1)
    sin_b = jnp.concatenate([zero, sin_r, zero, sin_c], axis=1)
    return cos_t, sin_a, sin_b


def kernel(x, c, ctx, c_ctx, w_ada, b_ada, w_in, w_gate_up, b_gate, attn_sink, gla_norm_w, w_out, ln1_g, ln1_b, w_router_group, b_router_group, w_router_expert, b_router_expert, w_exp_gate, w_exp_up, w_exp_down, ln2_g, ln2_b):
    batch, seq, d = x.shape
    n_ctx = ctx.shape[1]
    assert w_ada.shape[0] == DEPTH and batch < MOD_ROWS
    assert seq % ROW_TILE == 0 and (batch * n_ctx) % ROW_TILE == 0 and seq % GLA_STEP == 0 and n_ctx == GLA_STEP
    n_tok = batch * seq
    tiles_per_seq = seq // ROW_TILE
    a_width = d // 2
    kv_width = a_width // A_GROUP
    b_width = d - a_width
    key_width = b_width // 2
    layer = 0

    cc = jnp.concatenate([c, c_ctx[None, :], jnp.zeros((MOD_ROWS - batch - 1, d), F32)], axis=0)
    mod3 = _adaln(cc, w_ada[layer], b_ada[layer]).reshape(MOD_ROWS, 6, d)

    splits = (a_width, kv_width, kv_width, key_width, key_width, b_width, b_width, 2 * GATE_RANK)
    w_in_b = w_in[layer].astype(BF16)
    zero_up = jnp.zeros((GATE_RANK, key_width), F32)
    wup2 = jnp.concatenate([jnp.concatenate([w_gate_up[layer, 0], zero_up], axis=1),
                            jnp.concatenate([zero_up, w_gate_up[layer, 1]], axis=1)], axis=0).astype(BF16)
    bg2 = b_gate[layer].reshape(1, 2 * key_width)
    tables = _rope_tables(seq)

    xf = x.reshape(n_tok, d)
    qa, ka, va, qb, kb, vb, rb, la_f, la_b = _project(
        xf, mod3, lambda i, per_seq: i // per_seq, tables, w_in_b, splits, wup2, bg2, rope=True, seq=seq)
    _, ka_c, va_c, _, kb_c, vb_c, _, lac_f, lac_b = _project(
        ctx.reshape(batch * n_ctx, d), mod3, lambda i, per_seq: batch, tables, w_in_b, splits, wup2, bg2,
        rope=False, seq=seq, keys_values_only=True)

    out_a = _attention(attn_sink[layer], qa, ka, va, ka_c, va_c, batch, seq, n_ctx)
    o_b = _gla(kb_c, vb_c, lac_b, qb, kb, vb, la_b, batch, seq, n_ctx, reverse=True)
    out_b = _gla(kb_c, vb_c, lac_f, qb, kb, vb, la_f, batch, seq, n_ctx, reverse=False,
                 extra=(o_b, rb, gla_norm_w[layer].reshape(1, b_width)))

    w_out_b = w_out[layer].astype(BF16)
    wr = jnp.concatenate([w_router_expert[layer], w_router_group[layer],
                          jnp.zeros((d, LANES - N_EXPERTS - N_GROUPS), F32)], axis=1).astype(BF16)
    br = jnp.concatenate([b_router_expert[layer], b_router_group[layer],
                          jnp.zeros((LANES - N_EXPERTS - N_GROUPS,), F32)]).reshape(1, LANES)
    x1, h2, rw, ri = _outproj(out_a, out_b, xf, mod3, w_out_b[:a_width], w_out_b[a_width:],
                              ln1_g[layer], ln1_b[layer], wr, br, seq)

    dest, tail_row, empty_block, plan, n_buf_rows = _routing_plan(ri, n_tok)
    dest3 = dest.reshape(n_tok // ROW_TILE, 1, TOP_K * ROW_TILE)
    buf = _dispatch(dest3, tail_row, empty_block, h2)
    y = _experts(plan, buf, w_exp_gate[layer], w_exp_up[layer], w_exp_down[layer], n_buf_rows)
    out = _combine(dest3, y, rw, x1, mod3, ln2_g[layer], ln2_b[layer], tiles_per_seq)
    return out.reshape(batch, seq, d)
```

```python
import functools

import jax
import jax.numpy as jnp
from jax import lax
from jax.experimental import pallas as pl
from jax.experimental.pallas import tpu as pltpu

F32 = jnp.float32
BF16 = jnp.bfloat16

HEAD_DIM = 128
GRID_W = 64
WINDOW = 128
A_BLOCK = 128
A_GROUP = 4
ROPE_THETA = 10000.0
B_HEADS = 4
GATE_RANK = 16
GATE_TAU = 16.0
GLA_CHUNK = 64
N_GROUPS = 4
EXPERTS_PER_GROUP = 8
N_EXPERTS = N_GROUPS * EXPERTS_PER_GROUP
TOP_K = 2
DEPTH = 1
ALPHA = (2.0 * DEPTH) ** 0.25
LN_EPS = 1e-6
LOG2_E = 1.4426950408889634
ATTN_EXP2_SCALE = HEAD_DIM ** -0.5 * LOG2_E

LANES = 128
SUBLANES = 8
MOD_ROWS = 8
VMEM_LIMIT = 56 * 1024 * 1024

ROW_TILE = 256
OUT_TILE = 512
SUB_ROWS = 256
PROJ_COLS = 1024
Q_BLOCKS = 2
GLA_STEP = 256
MOE_BLOCK = 256
HID_TILE = 512
W_PIECES = 4
DMA_UNROLL = 8


def _params(sem):
    return pltpu.CompilerParams(dimension_semantics=sem, vmem_limit_bytes=VMEM_LIMIT)


def _resident(shape):
    nd = len(shape)
    return pl.BlockSpec(shape, lambda *_: (0,) * nd, pipeline_mode=pl.Buffered(1))


def _ln(x):
    mu = jnp.mean(x, axis=-1, keepdims=True)
    xc = x - mu
    var = jnp.mean(xc * xc, axis=-1, keepdims=True)
    return xc * lax.rsqrt(var + LN_EPS)


def _silu(x):
    return x * jax.nn.sigmoid(x)


def _bdot(a, b):
    return jnp.dot(a, b, preferred_element_type=F32)


def _bdot_nt(a, b):
    return lax.dot_general(a, b, (((1,), (1,)), ((), ())), preferred_element_type=F32)


def _adaln_kernel(c_ref, w_ref, b_ref, o_ref):
    s = _silu(c_ref[...]).astype(BF16)
    o_ref[...] = _bdot(s, w_ref[...].astype(BF16)) + b_ref[...]


def _adaln(cc, w_ada, b_ada):
    d, n = w_ada.shape
    tn = 1024
    return pl.pallas_call(
        _adaln_kernel,
        grid=(n // tn,),
        in_specs=[pl.BlockSpec((MOD_ROWS, d), lambda j: (0, 0)),
                  pl.BlockSpec((d, tn), lambda j: (0, j)),
                  pl.BlockSpec((1, tn), lambda j: (0, j))],
        out_specs=pl.BlockSpec((MOD_ROWS, tn), lambda j: (0, j)),
        out_shape=jax.ShapeDtypeStruct((MOD_ROWS, n), F32),
        compiler_params=_params(("arbitrary",)),
        name="adaln",
    )(cc, w_ada, b_ada.reshape(1, n))


def _proj_kernel(x_ref, mod_ref, cos_ref, sina_ref, sinb_ref, w_ref, wup, bg, *out_refs, groups, rope):
    for r0 in range(0, x_ref.shape[0], SUB_ROWS):
        _proj_rows(slice(r0, r0 + SUB_ROWS), x_ref, mod_ref, cos_ref, sina_ref, sinb_ref, w_ref, wup, bg,
                   out_refs, groups, rope)


def _proj_rows(rs, x_ref, mod_ref, cos_ref, sina_ref, sinb_ref, w_ref, wup, bg, out_refs, groups, rope):
    h = _ln(x_ref[rs, :]) * (1.0 + mod_ref[1:2, :]) + mod_ref[0:1, :]
    hb = h.astype(BF16)

    def rot(t):
        return (t * cos_ref[rs, :] + pltpu.roll(t, 96, 1) * sina_ref[rs, :]
                + pltpu.roll(t, 32, 1) * sinb_ref[rs, :])

    col = 0
    refs = iter(out_refs)
    for n, wanted, rotary, post in groups:
        if wanted:
            o_ref = next(refs)
            step = min(n, PROJ_COLS)
            for c0 in range(0, n, step):
                t = _bdot(hb, w_ref[:, col + c0:col + c0 + step])
                if rotary and rope:
                    for l0 in range(0, step, LANES):
                        r = rot(t[:, l0:l0 + LANES])
                        o_ref[rs, c0 + l0:c0 + l0 + LANES] = (r if post == 1.0 else r * post).astype(o_ref.dtype)
                else:
                    o_ref[rs, c0:c0 + step] = t.astype(o_ref.dtype)
        col += n
    laf_o, lab_o = refs
    gl = _bdot(hb, w_ref[:, col:]).astype(BF16)
    z = _bdot(gl, wup[...]) + bg[...]
    la = (jnp.minimum(z, 0.0) - jnp.log1p(jnp.exp(-jnp.abs(z)))) / GATE_TAU
    kw = laf_o.shape[1]
    laf_o[rs, :] = la[:, :kw]
    lab_o[rs, :] = la[:, kw:]


def _project(xf, mod3, mod_row, tables, w_in_b, splits, wup2, bg2, *, rope, seq, keys_values_only=False):
    r, d = xf.shape
    tm = OUT_TILE
    assert r % tm == 0 and seq % tm == 0 and tm % SUB_ROWS == 0
    tiles_per_seq = seq // tm
    mod_row_of_tile = lambda i: mod_row(i, tiles_per_seq)
    n_gate = splits[-1]
    assert sum(splits) == w_in_b.shape[1] and (sum(splits) - n_gate) % LANES == 0
    kv = keys_values_only
    spec = ((BF16, not kv, True, ATTN_EXP2_SCALE), (BF16, True, True, 1.0), (BF16, True, False, 1.0),
            (BF16, not kv, False, 1.0), (BF16, True, False, 1.0), (BF16, True, False, 1.0),
            (BF16, not kv, False, 1.0))
    groups = tuple((n, wanted, rotary, post) for n, (_, wanted, rotary, post) in zip(splits, spec))
    key_width = splits[3]
    outs = [(n, dt) for n, (dt, wanted, _, _) in zip(splits, spec) if wanted] + [(key_width, F32)] * 2
    cos_t, sina_t, sinb_t = tables
    row = lambda n: pl.BlockSpec((tm, n), lambda i: (i, 0))
    tab = pl.BlockSpec((tm, LANES), lambda i: (i % tiles_per_seq, 0))
    res = pl.pallas_call(
        functools.partial(_proj_kernel, groups=groups, rope=rope),
        grid=(r // tm,),
        in_specs=[row(d),
                  pl.BlockSpec((None, 6, d), lambda i: (mod_row_of_tile(i), 0, 0)),
                  tab, tab, tab,
                  _resident(w_in_b.shape), _resident(wup2.shape), _resident(bg2.shape)],
        out_specs=[row(n) for n, _ in outs],
        out_shape=[jax.ShapeDtypeStruct((r, n), dt) for n, dt in outs],
        compiler_params=_params(("parallel",)),
        name="in_proj_rope" if rope else "in_proj_ctx",
    )(xf, mod3, cos_t, sina_t, sinb_t, w_in_b, wup2, bg2)
    res = iter(res)
    return tuple(next(res) if wanted else None for _, wanted, _, _ in spec) + tuple(res)


def _attn_kernel(sink_ref, q_ref, *refs):
    n_kv = Q_BLOCKS + 2
    k_refs, v_refs = refs[:n_kv], refs[n_kv:2 * n_kv]
    kx_ref, vx_ref, o_ref = refs[2 * n_kv:]
    n = pl.program_id(1)
    last = pl.num_programs(1) - 1
    blk = A_BLOCK
    rows = A_GROUP * blk
    n_ctx = kx_ref.shape[0]
    qi = lax.broadcasted_iota(jnp.int32, (rows, blk), 0) % blk
    kj = lax.broadcasted_iota(jnp.int32, (rows, blk), 1)
    for u in range(Q_BLOCKS):
        ok_prev = (kj >= qi) & (n > 0) if u == 0 else (kj >= qi)
        ok_next = (kj <= qi) & (n < last) if u == Q_BLOCKS - 1 else (kj <= qi)
        qs = slice(u * blk, (u + 1) * blk)
        kp_ref, kc_ref, kn_ref = k_refs[u:u + 3]
        vp_ref, vc_ref, vn_ref = v_refs[u:u + 3]
        for hk in range(kc_ref.shape[1] // HEAD_DIM):
            ks = slice(hk * HEAD_DIM, (hk + 1) * HEAD_DIM)
            q4 = jnp.concatenate(
                [q_ref[qs, (hk * A_GROUP + g) * HEAD_DIM:(hk * A_GROUP + g + 1) * HEAD_DIM]
                 for g in range(A_GROUP)], axis=0)
            s_ctx = _bdot_nt(q4, kx_ref[:, ks])
            s_prev = jnp.where(ok_prev, _bdot_nt(q4, kp_ref[:, ks]), -jnp.inf)
            s_cur = _bdot_nt(q4, kc_ref[:, ks])
            s_next = jnp.where(ok_next, _bdot_nt(q4, kn_ref[:, ks]), -jnp.inf)
            sink = jnp.concatenate(
                [jnp.full((blk, 1), sink_ref[hk * A_GROUP + g] * LOG2_E, F32) for g in range(A_GROUP)], axis=0)
            m_lanes = jnp.maximum(jnp.maximum(s_prev, s_cur), s_next)
            for l0 in range(0, n_ctx, LANES):
                m_lanes = jnp.maximum(m_lanes, s_ctx[:, l0:l0 + LANES])
            m = jnp.maximum(jnp.max(m_lanes, axis=-1, keepdims=True), sink)
            e_ctx = jnp.exp2(s_ctx - m)
            e_prev = jnp.exp2(s_prev - m)
            e_cur = jnp.exp2(s_cur - m)
            e_next = jnp.exp2(s_next - m)
            e_lanes = e_prev + e_cur + e_next
            for l0 in range(0, n_ctx, LANES):
                e_lanes = e_lanes + e_ctx[:, l0:l0 + LANES]
            den = jnp.exp2(sink - m) + jnp.sum(e_lanes, axis=-1, keepdims=True)
            o = (_bdot(e_ctx.astype(BF16), vx_ref[:, ks])
                 + (_bdot(e_prev.astype(BF16), vp_ref[:, ks])
                    + _bdot(e_cur.astype(BF16), vc_ref[:, ks])
                    + _bdot(e_next.astype(BF16), vn_ref[:, ks]))) * (1.0 / den)
            for g in range(A_GROUP):
                hq = hk * A_GROUP + g
                o_ref[qs, hq * HEAD_DIM:(hq + 1) * HEAD_DIM] = o[g * blk:(g + 1) * blk, :].astype(o_ref.dtype)


def _attention(sink, qa, ka, va, ka_c, va_c, batch, seq, n_ctx):
    blk = A_BLOCK
    nb = seq // blk
    assert nb % Q_BLOCKS == 0
    steps = nb // Q_BLOCKS
    aw = qa.shape[1]
    kvw = ka.shape[1]
    kv = lambda j: pl.BlockSpec(
        (blk, kvw), lambda b, n: (b * nb + jnp.clip(n * Q_BLOCKS - 1 + j, 0, nb - 1), 0))
    kv_specs = [kv(j) for j in range(Q_BLOCKS + 2)]
    qo = pl.BlockSpec((Q_BLOCKS * blk, aw), lambda b, n: (b * steps + n, 0))
    ctx = pl.BlockSpec((n_ctx, kvw), lambda b, n: (b, 0))
    return pl.pallas_call(
        _attn_kernel,
        grid=(batch, steps),
        in_specs=[pl.BlockSpec(memory_space=pltpu.SMEM), qo] + kv_specs + kv_specs + [ctx, ctx],
        out_specs=qo,
        out_shape=jax.ShapeDtypeStruct((batch * seq, aw), BF16),
        compiler_params=_params(("parallel", "parallel")),
        name="window_gqa",
    )(sink, qa, *([ka] * (Q_BLOCKS + 2)), *([va] * (Q_BLOCKS + 2)), ka_c, va_c)


def _chunk_cumsum(g, *, reverse):
    rows = g.shape[0]
    p = lax.broadcasted_iota(jnp.int32, g.shape, 0) % GLA_CHUNK
    s = 1
    while s < GLA_CHUNK:
        if reverse:
            g = g + jnp.where(p < GLA_CHUNK - s, pltpu.roll(g, rows - s, 0), 0.0)
        else:
            g = g + jnp.where(p >= s, pltpu.roll(g, s, 0), 0.0)
        s *= 2
    return g


def _per_chunk_row(x, i):
    c = GLA_CHUNK
    return jnp.concatenate(
        [jnp.broadcast_to(x[j * c + i:j * c + i + 1, :], (c, x.shape[1])) for j in range(x.shape[0] // c)], axis=0)


def _gla_block(q, k, v, cum, state_t, mask, *, reverse, need_o):
    c = GLA_CHUNK
    rows, dk = k.shape
    n_chunks = rows // c
    k = k.astype(F32)
    i_last = 0 if reverse else c - 1
    i_mid = c // 2 if reverse else c // 2 - 1
    b_last = _per_chunk_row(cum, i_last)
    kdec = (k * jnp.exp(b_last - cum)).astype(BF16)
    if need_o:
        b_mid = _per_chunk_row(cum, i_mid)
        qc = q.astype(F32) * dk ** -0.5
        qm = (qc * jnp.exp(cum - b_mid)).astype(BF16)
        km = (k * jnp.exp(b_mid - cum)).astype(BF16)
        a = jnp.where(mask, _bdot_nt(qm, km), 0.0)
        o_intra = _bdot(a.astype(BF16), v)
        qe = (qc * jnp.exp(cum)).astype(BF16)
    o_inter = [None] * n_chunks
    for j in (reversed(range(n_chunks)) if reverse else range(n_chunks)):
        rs = slice(j * c, (j + 1) * c)
        if need_o:
            o_inter[j] = _bdot_nt(qe[rs, :], state_t.astype(BF16))
        kv_t = lax.dot_general(v[rs, :], kdec[rs, :], (((0,), (0,)), ((), ())), preferred_element_type=F32)
        state_t = state_t * jnp.exp(cum[j * c + i_last:j * c + i_last + 1, :]) + kv_t
    o = o_intra + jnp.concatenate(o_inter, axis=0) if need_o else None
    return o, state_t


def _gla_kernel(*refs, reverse, final):
    if final:
        (kx_ref, vx_ref, gx_ref, q_ref, k_ref, v_ref, g_ref, ob_ref, r_ref, nw_ref, o_ref, st_ref) = refs
    else:
        (kx_ref, vx_ref, gx_ref, q_ref, k_ref, v_ref, g_ref, o_ref, st_ref) = refs
    t = pl.program_id(1)
    n_heads = st_ref.shape[0]
    dv, dk = st_ref.shape[1:]

    @pl.when(t == 0)
    def _context():
        cum = _chunk_cumsum(gx_ref[...], reverse=reverse)
        for h in range(n_heads):
            ks, vs = slice(h * dk, (h + 1) * dk), slice(h * dv, (h + 1) * dv)
            _, st = _gla_block(None, kx_ref[:, ks], vx_ref[:, vs], cum[:, ks], jnp.zeros((dv, dk), F32), None,
                               reverse=reverse, need_o=False)
            st_ref[h] = st

    @pl.when(t > 0)
    def _latent():
        rows = k_ref.shape[0]
        cum = _chunk_cumsum(g_ref[...], reverse=reverse)
        r = lax.broadcasted_iota(jnp.int32, (rows, rows), 0)
        s = lax.broadcasted_iota(jnp.int32, (rows, rows), 1)
        causal = (s >= r) if reverse else (s <= r)
        mask = causal & ((r // GLA_CHUNK) == (s // GLA_CHUNK))
        for h in range(n_heads):
            ks, vs = slice(h * dk, (h + 1) * dk), slice(h * dv, (h + 1) * dv)
            o, st = _gla_block(q_ref[:, ks], k_ref[:, ks], v_ref[:, vs], cum[:, ks], st_ref[h], mask,
                               reverse=reverse, need_o=True)
            st_ref[h] = st
            if final:
                o = o + ob_ref[:, vs].astype(F32)
                o = o * lax.rsqrt(jnp.mean(o * o, axis=-1, keepdims=True) + LN_EPS)
                o = o * nw_ref[:, vs]
                o = o * _silu(r_ref[:, vs].astype(F32))
            o_ref[:, vs] = o.astype(o_ref.dtype)


def _gla(kb_c, vb_c, la_c, qb, kb, vb, la, batch, seq, n_ctx, *, reverse, extra=None):
    assert n_ctx % GLA_CHUNK == 0 and GLA_STEP % GLA_CHUNK == 0
    tb = GLA_STEP
    nt = seq // tb
    kw, vw = qb.shape[1], vb.shape[1]
    final = extra is not None

    def lat(b, t):
        i = jnp.maximum(t - 1, 0)
        if reverse:
            i = nt - 1 - i
        return (b * nt + i, 0)

    cx = lambda b, t: (b, 0)
    in_specs = [pl.BlockSpec((n_ctx, kw), cx), pl.BlockSpec((n_ctx, vw), cx), pl.BlockSpec((n_ctx, kw), cx),
                pl.BlockSpec((tb, kw), lat), pl.BlockSpec((tb, kw), lat),
                pl.BlockSpec((tb, vw), lat), pl.BlockSpec((tb, kw), lat)]
    args = [kb_c, vb_c, la_c, qb, kb, vb, la]
    if final:
        o_other, rb, norm_w = extra
        in_specs += [pl.BlockSpec((tb, vw), lat), pl.BlockSpec((tb, vw), lat), _resident((1, vw))]
        args += [o_other, rb, norm_w]
    return pl.pallas_call(
        functools.partial(_gla_kernel, reverse=reverse, final=final),
        grid=(batch, nt + 1),
        in_specs=in_specs,
        out_specs=pl.BlockSpec((tb, vw), lat),
        out_shape=jax.ShapeDtypeStruct((batch * seq, vw), BF16),
        scratch_shapes=[pltpu.VMEM((B_HEADS, vw // B_HEADS, kw // B_HEADS), F32)],
        compiler_params=_params(("parallel", "arbitrary")),
        name="gla_fwd_out" if final else "gla_bwd",
    )(*args)


def _outproj_kernel(oa_ref, ob_ref, x_ref, mod_ref, wt_ref, wb_ref, lng_ref, lnb_ref, wr_ref, br_ref,
                    x1_o, h2_o, rw_o, ri_o):
    for r0 in range(0, x_ref.shape[0], SUB_ROWS):
        _outproj_rows(slice(r0, r0 + SUB_ROWS), oa_ref, ob_ref, x_ref, mod_ref, wt_ref, wb_ref, lng_ref, lnb_ref,
                      wr_ref, br_ref, x1_o, h2_o, rw_o, ri_o)


def _outproj_rows(rs, oa_ref, ob_ref, x_ref, mod_ref, wt_ref, wb_ref, lng_ref, lnb_ref, wr_ref, br_ref,
                  x1_o, h2_o, rw_o, ri_o):
    y = _bdot(oa_ref[rs, :], wt_ref[...]) + _bdot(ob_ref[rs, :], wb_ref[...])
    x1 = _ln(ALPHA * x_ref[rs, :] + mod_ref[2:3, :] * y) * lng_ref[...] + lnb_ref[...]
    x1_o[rs, :] = x1
    h2 = _ln(x1) * (1.0 + mod_ref[4:5, :]) + mod_ref[3:4, :]
    h2_o[rs, :] = h2
    lg = _bdot(h2.astype(BF16), wr_ref[...]) + br_ref[...]
    lane = lax.broadcasted_iota(jnp.int32, lg.shape, 1)
    lanef = lane.astype(F32)
    big = float(LANES)
    is_g = (lane >= N_EXPERTS) & (lane < N_EXPERTS + N_GROUPS)
    gl = jnp.where(is_g, lg, -jnp.inf)
    gmax = jnp.max(gl, axis=-1, keepdims=True)
    pg_top = 1.0 / jnp.sum(jnp.exp(gl - gmax), axis=-1, keepdims=True)
    grp = jnp.min(jnp.where(gl == gmax, lanef, big), axis=-1, keepdims=True) - N_EXPERTS
    in_grp = (lane < N_EXPERTS) & ((lane // EXPERTS_PER_GROUP).astype(F32) == grp)
    el = jnp.where(in_grp, lg, -jnp.inf)
    m1 = jnp.max(el, axis=-1, keepdims=True)
    i1 = jnp.min(jnp.where(el == m1, lanef, big), axis=-1, keepdims=True)
    el2 = jnp.where(lanef == i1, -jnp.inf, el)
    m2 = jnp.max(el2, axis=-1, keepdims=True)
    i2 = jnp.min(jnp.where(el2 == m2, lanef, big), axis=-1, keepdims=True)
    e2 = jnp.exp(m2 - m1)
    w1 = pg_top / (1.0 + e2)
    w2 = pg_top * e2 / (1.0 + e2)
    rw_o[rs, :] = jnp.where(lane == 0, w1, jnp.where(lane == 1, w2, 0.0))
    ri_o[rs, :] = jnp.where(lane == 0, i1, jnp.where(lane == 1, i2, 0.0)).astype(jnp.int32)


def _outproj(out_a, out_b, xf, mod3, w_top, w_bot, ln_g, ln_b, wr, br, seq):
    r, d = xf.shape
    tm = OUT_TILE
    assert seq % tm == 0 and tm % SUB_ROWS == 0
    tiles_per_seq = seq // tm
    row = lambda n: pl.BlockSpec((tm, n), lambda i: (i, 0))
    return pl.pallas_call(
        _outproj_kernel,
        grid=(r // tm,),
        in_specs=[row(out_a.shape[1]), row(out_b.shape[1]), row(d),
                  pl.BlockSpec((None, 6, d), lambda i: (i // tiles_per_seq, 0, 0)),
                  _resident(w_top.shape), _resident(w_bot.shape),
                  _resident((1, d)), _resident((1, d)), _resident(wr.shape), _resident(br.shape)],
        out_specs=[row(d), row(d), row(LANES), row(LANES)],
        out_shape=[jax.ShapeDtypeStruct((r, d), F32), jax.ShapeDtypeStruct((r, d), F32),
                   jax.ShapeDtypeStruct((r, LANES), F32), jax.ShapeDtypeStruct((r, LANES), jnp.int32)],
        compiler_params=_params(("parallel",)),
        name="out_proj_router",
    )(out_a, out_b, xf, mod3, w_top, w_bot, ln_g.reshape(1, d), ln_b.reshape(1, d), wr, br)


def _row_copy(src_ref, src_row, dst_ref, dst_row, sem):
    return pltpu.make_async_copy(src_ref.at[pl.ds(src_row, 1)], dst_ref.at[pl.ds(dst_row, 1)], sem)


def _dispatch_kernel(dest_ref, tail_ref, empty_ref, h_ref, buf_ref, zero_ref, hbuf, sem, zsem, lsem, rsem):
    tm = hbuf.shape[1]
    step = pl.program_id(0)

    def zero_block(b):
        row = pl.multiple_of(b * MOE_BLOCK, MOE_BLOCK)
        return pltpu.make_async_copy(zero_ref, buf_ref.at[pl.ds(row, MOE_BLOCK)], zsem)

    def for_empty_blocks(fn):
        def body(b, carry):
            @pl.when(empty_ref[b] != 0)
            def _():
                fn(b)
            return carry
        lax.fori_loop(0, empty_ref.shape[0], body, 0)

    @pl.when(step == 0)
    def _zero_fill():
        zero_ref[...] = jnp.zeros(zero_ref.shape, zero_ref.dtype)
        for e in range(N_EXPERTS):
            start = pl.multiple_of(tail_ref[e] // SUBLANES * SUBLANES, SUBLANES)
            pltpu.make_async_copy(zero_ref, buf_ref.at[pl.ds(start, MOE_BLOCK)], sem).start()
        for e in range(N_EXPERTS):
            pltpu.make_async_copy(zero_ref, buf_ref.at[pl.ds(0, MOE_BLOCK)], sem).wait()
        for_empty_blocks(lambda b: zero_block(b).start())

    @pl.when(step == pl.num_programs(0) - 1)
    def _zero_done():
        for_empty_blocks(lambda b: zero_block(b).wait())

    n_slots = hbuf.shape[0]
    slot = step % n_slots

    def load(tile, sl):
        row = pl.multiple_of(tile * tm, tm)
        return pltpu.make_async_copy(h_ref.at[pl.ds(row, tm)], hbuf.at[sl], lsem.at[sl])

    @pl.when(step == 0)
    def _():
        load(0, 0).start()

    @pl.when(step + 1 < pl.num_programs(0))
    def _():
        load(step + 1, (step + 1) % n_slots).start()

    load(step, slot).wait()

    def issue(i, carry):
        for k in range(TOP_K):
            _row_copy(hbuf.at[slot], i, buf_ref, dest_ref[0, 0, TOP_K * i + k], rsem.at[slot]).start()
        return carry

    lax.fori_loop(0, tm, issue, 0, unroll=DMA_UNROLL)

    def wait_tile(sl):
        for _ in range(TOP_K):
            pltpu.make_async_copy(hbuf.at[sl], buf_ref.at[pl.ds(0, tm)], rsem.at[sl]).wait()

    pl.when(step > 0)(lambda: wait_tile((step + n_slots - 1) % n_slots))
    pl.when(step == pl.num_programs(0) - 1)(lambda: wait_tile(slot))


def _dispatch(dest3, tail_row, empty_block, h2):
    r, d = h2.shape
    tm = ROW_TILE
    n_rows = empty_block.shape[0] * MOE_BLOCK
    return pl.pallas_call(
        _dispatch_kernel,
        grid=(r // tm,),
        in_specs=[pl.BlockSpec((1, 1, TOP_K * tm), lambda i: (i, 0, 0), memory_space=pltpu.SMEM),
                  pl.BlockSpec(memory_space=pltpu.SMEM),
                  pl.BlockSpec(memory_space=pltpu.SMEM),
                  pl.BlockSpec(memory_space=pl.ANY)],
        out_specs=pl.BlockSpec(memory_space=pl.ANY),
        out_shape=jax.ShapeDtypeStruct((n_rows, d), h2.dtype),
        scratch_shapes=[pltpu.VMEM((MOE_BLOCK, d), h2.dtype), pltpu.VMEM((3, tm, d), h2.dtype),
                        pltpu.SemaphoreType.DMA, pltpu.SemaphoreType.DMA, pltpu.SemaphoreType.DMA((3,)),
                        pltpu.SemaphoreType.DMA((3,))],
        compiler_params=_params(("arbitrary",)),
        name="moe_dispatch",
    )(dest3, tail_row, empty_block, h2)


def _expert_kernel(bq_ref, lo_ref, hi_ref, el_ref, nq_ref, ub_ref, x_ref, wg_hbm, wu_hbm, wd_hbm, o_ref,
                   wgb, wub, wdb, sg, su, sd, sems):
    b = pl.program_id(0)
    q = bq_ref[b]
    slot = q % 2
    mats = ((wg_hbm, sg, wgb), (wu_hbm, su, wub), (wd_hbm, sd, wdb))
    n_pieces = W_PIECES * len(mats)

    def piece_copy(m, qt, t):
        hbm, stage, _ = mats[m]
        pr = stage.shape[1]
        return pltpu.make_async_copy(hbm.at[el_ref[qt], pl.ds(t * pr, pr), :], stage.at[t % 2],
                                     sems.at[m, t % 2])

    def process(qt, lo, hi):
        for p in range(n_pieces):
            m, t = p % len(mats), p // len(mats)

            @pl.when((lo <= p) & (p < hi))
            def _():
                _, stage, resident = mats[m]
                pr = stage.shape[1]
                piece_copy(m, qt, t).wait()
                resident[qt % 2, pl.ds(t * pr, pr), :] = stage[t % 2].astype(BF16)
                q2 = qt + (t + 2) // W_PIECES

                @pl.when(q2 < nq_ref[0])
                def _():
                    piece_copy(m, q2, (t + 2) % W_PIECES).start()

    @pl.when(b == 0)
    def _first_expert():
        for m in range(len(mats)):
            for t in range(2):
                piece_copy(m, 0, t).start()
        process(0, 0, n_pieces)

    @pl.when(b < ub_ref[0])
    def _compute():
        xb = x_ref[...].astype(BF16)
        hid = wgb.shape[2]
        acts = []
        for h0 in range(0, hid, HID_TILE):
            hs = pl.ds(h0, HID_TILE)
            acts.append((_silu(_bdot(xb, wgb[slot, :, hs])) * _bdot(xb, wub[slot, :, hs])).astype(BF16))
        o_ref[...] = _bdot(jnp.concatenate(acts, axis=1), wdb[slot])

    @pl.when(b >= ub_ref[0])
    def _unused():
        o_ref[...] = jnp.zeros(o_ref.shape, o_ref.dtype)

    process(q + 1, lo_ref[b], hi_ref[b])


def _experts(plan, buf, wg, wu, wd, n_out_rows):
    d = buf.shape[1]
    hid = wg.shape[2]
    assert d % W_PIECES == 0 and hid % W_PIECES == 0 and W_PIECES % 2 == 0
    blk = lambda f: pl.BlockSpec((MOE_BLOCK, d), f)
    return pl.pallas_call(
        _expert_kernel,
        grid_spec=pltpu.PrefetchScalarGridSpec(
            num_scalar_prefetch=len(plan),
            grid=(n_out_rows // MOE_BLOCK,),
            in_specs=[blk(lambda b, bq, lo, hi, el, nq, ub: (jnp.minimum(b, ub[0] - 1), 0)),
                      pl.BlockSpec(memory_space=pl.ANY), pl.BlockSpec(memory_space=pl.ANY),
                      pl.BlockSpec(memory_space=pl.ANY)],
            out_specs=blk(lambda b, bq, lo, hi, el, nq, ub: (b, 0)),
            scratch_shapes=[pltpu.VMEM((2, d, hid), BF16), pltpu.VMEM((2, d, hid), BF16),
                            pltpu.VMEM((2, hid, d), BF16),
                            pltpu.VMEM((2, d // W_PIECES, hid), F32), pltpu.VMEM((2, d // W_PIECES, hid), F32),
                            pltpu.VMEM((2, hid // W_PIECES, d), F32),
                            pltpu.SemaphoreType.DMA((3, 2))],
        ),
        out_shape=jax.ShapeDtypeStruct((n_out_rows, d), F32),
        compiler_params=_params(("arbitrary",)),
        name="moe_experts",
    )(*plan, buf, wg, wu, wd)


def _combine_kernel(dest_ref, dnext_ref, y_ref, rw_ref, x1_ref, mod_ref, lng_ref, lnb_ref, o_ref,
                    g_even, g_odd, sem):
    tm = x1_ref.shape[0]
    i = pl.program_id(0)
    n_groups = tm // DMA_UNROLL
    has_next = i + 1 < pl.num_programs(0)

    def gather(d_ref, g_ref, sl):
        def body(j, carry):
            for r in range(DMA_UNROLL):
                row = j * DMA_UNROLL + r
                for k in range(TOP_K):
                    _row_copy(y_ref, d_ref[0, 0, TOP_K * row + k], g_ref.at[k], row, sem.at[sl]).start()
            return carry
        lax.fori_loop(0, n_groups, body, 0)

    @pl.when(i == 0)
    def _():
        gather(dest_ref, g_even, 0)

    def run(sl, g_cur, g_next, prefetch):
        if prefetch:
            gather(dnext_ref, g_next, 1 - sl)
        for k in range(TOP_K):
            pltpu.make_async_copy(y_ref.at[pl.ds(0, tm)], g_cur.at[k], sem.at[sl]).wait()
        f = rw_ref[:, 0:1] * g_cur[0] + rw_ref[:, 1:2] * g_cur[1]
        o_ref[...] = _ln(ALPHA * x1_ref[...] + mod_ref[5:6, :] * f) * lng_ref[...] + lnb_ref[...]

    for sl, g_cur, g_next in ((0, g_even, g_odd), (1, g_odd, g_even)):
        for prefetch in (True, False):
            pl.when((i % 2 == sl) & (has_next == prefetch))(functools.partial(run, sl, g_cur, g_next, prefetch))


def _combine(dest3, y, rw, x1, mod3, ln_g, ln_b, tiles_per_seq):
    r, d = x1.shape
    tm = ROW_TILE
    n = r // tm
    row = lambda w: pl.BlockSpec((tm, w), lambda i: (i, 0))
    return pl.pallas_call(
        _combine_kernel,
        grid=(n,),
        in_specs=[pl.BlockSpec((1, 1, TOP_K * tm), lambda i: (i, 0, 0), memory_space=pltpu.SMEM),
                  pl.BlockSpec((1, 1, TOP_K * tm), lambda i: (jnp.minimum(i + 1, n - 1), 0, 0),
                               memory_space=pltpu.SMEM),
                  pl.BlockSpec(memory_space=pl.ANY),
                  row(LANES), row(d),
                  pl.BlockSpec((None, 6, d), lambda i: (i // tiles_per_seq, 0, 0)),
                  _resident((1, d)), _resident((1, d))],
        out_specs=row(d),
        out_shape=jax.ShapeDtypeStruct((r, d), F32),
        scratch_shapes=[pltpu.VMEM((TOP_K, tm, d), F32), pltpu.VMEM((TOP_K, tm, d), F32),
                        pltpu.SemaphoreType.DMA((2,))],
        compiler_params=_params(("arbitrary",)),
        name="moe_combine_ln",
    )(dest3, dest3, y, rw, x1, mod3, ln_g.reshape(1, d), ln_b.reshape(1, d))


def _routing_plan(ri, n_tokens):
    eid = ri[:, :TOP_K].reshape(-1)
    m = n_tokens * TOP_K
    onehot = (eid[:, None] == jnp.arange(N_EXPERTS, dtype=jnp.int32)[None, :]).astype(jnp.int32)
    csum = jnp.cumsum(onehot, axis=0)
    counts = csum[-1]
    rank = jnp.sum(csum * onehot, axis=1) - 1
    padded = (counts + MOE_BLOCK - 1) // MOE_BLOCK * MOE_BLOCK
    pad_end = jnp.cumsum(padded)
    pad_start = pad_end - padded
    dest = jnp.sum(onehot * pad_start[None, :], axis=1) + rank
    tail = pad_start + counts
    n_blocks = m // MOE_BLOCK + N_EXPERTS
    used_blocks = pad_end[-1] // MOE_BLOCK
    empty = jnp.arange(n_blocks + 1, dtype=jnp.int32) >= used_blocks

    n_pieces = 3 * W_PIECES
    has = counts > 0
    ordinal = jnp.cumsum(has.astype(jnp.int32)) - 1
    n_ord = ordinal[-1] + 1
    e_ids = jnp.arange(N_EXPERTS, dtype=jnp.int32)
    ord_expert = jnp.sum(jnp.where(has[None, :] & (ordinal[None, :] == e_ids[:, None]), e_ids[None, :], 0), axis=1)
    bid = jnp.arange(n_blocks, dtype=jnp.int32)
    brow = bid[:, None] * MOE_BLOCK
    in_e = ((pad_start[None, :] <= brow) & (brow < pad_end[None, :])).astype(jnp.int32)
    pick = lambda v: jnp.sum(in_e * v[None, :], axis=1)
    used = bid < used_blocks
    blk_q = jnp.where(used, pick(ordinal), n_ord - 1)
    i_in_e = bid - pick(pad_start) // MOE_BLOCK
    k_e = jnp.maximum(pick(padded) // MOE_BLOCK, 1)
    brings = used & (blk_q + 1 < n_ord)
    lo = jnp.where(brings, n_pieces * i_in_e // k_e, 0)
    hi = jnp.where(brings, n_pieces * (i_in_e + 1) // k_e, 0)
    i32 = lambda a: a.astype(jnp.int32)
    plan = (i32(blk_q), i32(lo), i32(hi), i32(ord_expert), i32(n_ord.reshape(1)), i32(used_blocks.reshape(1)))
    return i32(dest), i32(tail), i32(empty), plan, n_blocks * MOE_BLOCK


def _rope_tables(seq):
    n_freq = HEAD_DIM // 4
    inv_freq = ROPE_THETA ** (-jnp.arange(n_freq, dtype=F32) / n_freq)
    rows = seq // GRID_W
    ar = jnp.arange(rows, dtype=F32)[:, None] * inv_freq
    ac = jnp.arange(GRID_W, dtype=F32)[:, None] * inv_freq
    by_row = lambda t: jnp.broadcast_to(t[:, None, :], (rows, GRID_W, n_freq)).reshape(seq, n_freq)
    by_col = lambda t: jnp.broadcast_to(t[None, :, :], (rows, GRID_W, n_freq)).reshape(seq, n_freq)
    cos_r, sin_r, cos_c, sin_c = by_row(jnp.cos(ar)), by_row(jnp.sin(ar)), by_col(jnp.cos(ac)), by_col(jnp.sin(ac))
    zero = jnp.zeros_like(cos_r)
    cos_t = jnp.concatenate([cos_r, cos_r, cos_c, cos_c], axis=1)
    sin_a = jnp.concatenate([-sin_r, zero, -sin_c, zero], axis=1)
    sin_b = jnp.concatenate([zero, sin_r, zero, sin_c], axis=1)
    return cos_t, sin_a, sin_b


def kernel(x, c, ctx, c_ctx, w_ada, b_ada, w_in, w_gate_up, b_gate, attn_sink, gla_norm_w, w_out, ln1_g, ln1_b, w_router_group, b_router_group, w_router_expert, b_router_expert, w_exp_gate, w_exp_up, w_exp_down, ln2_g, ln2_b):
    batch, seq, d = x.shape
    n_ctx = ctx.shape[1]
    assert w_ada.shape[0] == DEPTH and batch < MOD_ROWS
    assert seq % ROW_TILE == 0 and (batch * n_ctx) % ROW_TILE == 0 and seq % GLA_STEP == 0 and n_ctx == GLA_STEP
    n_tok = batch * seq
    tiles_per_seq = seq // ROW_TILE
    a_width = d // 2
    kv_width = a_width // A_GROUP
    b_width = d - a_width
    key_width = b_width // 2
    layer = 0

    cc = jnp.concatenate([c, c_ctx[None, :], jnp.zeros((MOD_ROWS - batch - 1, d), F32)], axis=0)
    mod3 = _adaln(cc, w_ada[layer], b_ada[layer]).reshape(MOD_ROWS, 6, d)

    splits = (a_width, kv_width, kv_width, key_width, key_width, b_width, b_width, 2 * GATE_RANK)
    w_in_b = w_in[layer].astype(BF16)
    zero_up = jnp.zeros((GATE_RANK, key_width), F32)
    wup2 = jnp.concatenate([jnp.concatenate([w_gate_up[layer, 0], zero_up], axis=1),
                            jnp.concatenate([zero_up, w_gate_up[layer, 1]], axis=1)], axis=0).astype(BF16)
    bg2 = b_gate[layer].reshape(1, 2 * key_width)
    tables = _rope_tables(seq)

    xf = x.reshape(n_tok, d)
    qa, ka, va, qb, kb, vb, rb, la_f, la_b = _project(
        xf, mod3, lambda i, per_seq: i // per_seq, tables, w_in_b, splits, wup2, bg2, rope=True, seq=seq)
    _, ka_c, va_c, _, kb_c, vb_c, _, lac_f, lac_b = _project(
        ctx.reshape(batch * n_ctx, d), mod3, lambda i, per_seq: batch, tables, w_in_b, splits, wup2, bg2,
        rope=False, seq=seq, keys_values_only=True)

    out_a = _attention(attn_sink[layer], qa, ka, va, ka_c, va_c, batch, seq, n_ctx)
    o_b = _gla(kb_c, vb_c, lac_b, qb, kb, vb, la_b, batch, seq, n_ctx, reverse=True)
    out_b = _gla(kb_c, vb_c, lac_f, qb, kb, vb, la_f, batch, seq, n_ctx, reverse=False,
                 extra=(o_b, rb, gla_norm_w[layer].reshape(1, b_width)))

    w_out_b = w_out[layer].astype(BF16)
    wr = jnp.concatenate([w_router_expert[layer], w_router_group[layer],
                          jnp.zeros((d, LANES - N_EXPERTS - N_GROUPS), F32)], axis=1).astype(BF16)
    br = jnp.concatenate([b_router_expert[layer], b_router_group[layer],
                          jnp.zeros((LANES - N_EXPERTS - N_GROUPS,), F32)]).reshape(1, LANES)
    x1, h2, rw, ri = _outproj(out_a, out_b, xf, mod3, w_out_b[:a_width], w_out_b[a_width:],
                              ln1_g[layer], ln1_b[layer], wr, br, seq)

    dest, tail_row, empty_block, plan, n_buf_rows = _routing_plan(ri, n_tok)
    dest3 = dest.reshape(n_tok // ROW_TILE, 1, TOP_K * ROW_TILE)
    buf = _dispatch(dest3, tail_row, empty_block, h2)
    y = _experts(plan, buf, w_exp_gate[layer], w_exp_up[layer], w_exp_down[layer], n_buf_rows)
    out = _combine(dest3, y, rw, x1, mod3, ln2_g[layer], ln2_b[layer], tiles_per_seq)
    return out.reshape(batch, seq, d)
```

```python
import functools

import jax
import jax.numpy as jnp
from jax import lax
from jax.experimental import pallas as pl
from jax.experimental.pallas import tpu as pltpu

F32 = jnp.float32
BF16 = jnp.bfloat16

HEAD_DIM = 128
GRID_W = 64
WINDOW = 128
A_BLOCK = 128
A_GROUP = 4
ROPE_THETA = 10000.0
B_HEADS = 4
GATE_RANK = 16
GATE_TAU = 16.0
GLA_CHUNK = 64
N_GROUPS = 4
EXPERTS_PER_GROUP = 8
N_EXPERTS = N_GROUPS * EXPERTS_PER_GROUP
TOP_K = 2
DEPTH = 1
ALPHA = (2.0 * DEPTH) ** 0.25
LN_EPS = 1e-6
LOG2_E = 1.4426950408889634
ATTN_EXP2_SCALE = HEAD_DIM ** -0.5 * LOG2_E

LANES = 128
SUBLANES = 8
MOD_ROWS = 8
VMEM_LIMIT = 56 * 1024 * 1024

ROW_TILE = 512
OUT_TILE = 512
SUB_ROWS = 256
PROJ_COLS = 1024
Q_BLOCKS = 4
GLA_STEP = 256
MOE_BLOCK = 256
HID_TILE = 512
W_PIECES = 4
DMA_UNROLL = 8


def _params(sem):
    return pltpu.CompilerParams(dimension_semantics=sem, vmem_limit_bytes=VMEM_LIMIT)


def _resident(shape):
    nd = len(shape)
    return pl.BlockSpec(shape, lambda *_: (0,) * nd, pipeline_mode=pl.Buffered(1))


def _ln(x):
    mu = jnp.mean(x, axis=-1, keepdims=True)
    xc = x - mu
    var = jnp.mean(xc * xc, axis=-1, keepdims=True)
    return xc * lax.rsqrt(var + LN_EPS)


def _silu(x):
    return x * jax.nn.sigmoid(x)


def _bdot(a, b):
    return jnp.dot(a, b, preferred_element_type=F32)


def _bdot_nt(a, b):
    return lax.dot_general(a, b, (((1,), (1,)), ((), ())), preferred_element_type=F32)


def _adaln_kernel(c_ref, w_ref, b_ref, o_ref):
    s = _silu(c_ref[...]).astype(BF16)
    o_ref[...] = _bdot(s, w_ref[...].astype(BF16)) + b_ref[...]


def _adaln(cc, w_ada, b_ada):
    d, n = w_ada.shape
    tn = 1024
    return pl.pallas_call(
        _adaln_kernel,
        grid=(n // tn,),
        in_specs=[pl.BlockSpec((MOD_ROWS, d), lambda j: (0, 0)),
                  pl.BlockSpec((d, tn), lambda j: (0, j)),
                  pl.BlockSpec((1, tn), lambda j: (0, j))],
        out_specs=pl.BlockSpec((MOD_ROWS, tn), lambda j: (0, j)),
        out_shape=jax.ShapeDtypeStruct((MOD_ROWS, n), F32),
        compiler_params=_params(("arbitrary",)),
        name="adaln",
    )(cc, w_ada, b_ada.reshape(1, n))


def _proj_kernel(x_ref, mod_ref, cos_ref, sina_ref, sinb_ref, w_ref, wup, bg, *out_refs, groups, rope):
    for r0 in range(0, x_ref.shape[0], SUB_ROWS):
        _proj_rows(slice(r0, r0 + SUB_ROWS), x_ref, mod_ref, cos_ref, sina_ref, sinb_ref, w_ref, wup, bg,
                   out_refs, groups, rope)


def _proj_rows(rs, x_ref, mod_ref, cos_ref, sina_ref, sinb_ref, w_ref, wup, bg, out_refs, groups, rope):
    h = _ln(x_ref[rs, :]) * (1.0 + mod_ref[1:2, :]) + mod_ref[0:1, :]
    hb = h.astype(BF16)

    def rot(t):
        return (t * cos_ref[rs, :] + pltpu.roll(t, 96, 1) * sina_ref[rs, :]
                + pltpu.roll(t, 32, 1) * sinb_ref[rs, :])

    col = 0
    refs = iter(out_refs)
    for n, wanted, rotary, post in groups:
        if wanted:
            o_ref = next(refs)
            step = min(n, PROJ_COLS)
            for c0 in range(0, n, step):
                t = _bdot(hb, w_ref[:, col + c0:col + c0 + step])
                if rotary and rope:
                    for l0 in range(0, step, LANES):
                        r = rot(t[:, l0:l0 + LANES])
                        o_ref[rs, c0 + l0:c0 + l0 + LANES] = (r if post == 1.0 else r * post).astype(o_ref.dtype)
                else:
                    o_ref[rs, c0:c0 + step] = t.astype(o_ref.dtype)
        col += n
    laf_o, lab_o = refs
    gl = _bdot(hb, w_ref[:, col:]).astype(BF16)
    z = _bdot(gl, wup[...]) + bg[...]
    la = (jnp.minimum(z, 0.0) - jnp.log1p(jnp.exp(-jnp.abs(z)))) / GATE_TAU
    kw = laf_o.shape[1]
    laf_o[rs, :] = la[:, :kw]
    lab_o[rs, :] = la[:, kw:]


def _project(xf, mod3, mod_row, tables, w_in_b, splits, wup2, bg2, *, rope, seq, keys_values_only=False):
    r, d = xf.shape
    tm = OUT_TILE
    assert r % tm == 0 and seq % tm == 0 and tm % SUB_ROWS == 0
    tiles_per_seq = seq // tm
    mod_row_of_tile = lambda i: mod_row(i, tiles_per_seq)
    n_gate = splits[-1]
    assert sum(splits) == w_in_b.shape[1] and (sum(splits) - n_gate) % LANES == 0
    kv = keys_values_only
    spec = ((BF16, not kv, True, ATTN_EXP2_SCALE), (BF16, True, True, 1.0), (BF16, True, False, 1.0),
            (BF16, not kv, False, 1.0), (BF16, True, False, 1.0), (BF16, True, False, 1.0),
            (BF16, not kv, False, 1.0))
    groups = tuple((n, wanted, rotary, post) for n, (_, wanted, rotary, post) in zip(splits, spec))
    key_width = splits[3]
    outs = [(n, dt) for n, (dt, wanted, _, _) in zip(splits, spec) if wanted] + [(key_width, F32)] * 2
    cos_t, sina_t, sinb_t = tables
    row = lambda n: pl.BlockSpec((tm, n), lambda i: (i, 0))
    tab = pl.BlockSpec((tm, LANES), lambda i: (i % tiles_per_seq, 0))
    res = pl.pallas_call(
        functools.partial(_proj_kernel, groups=groups, rope=rope),
        grid=(r // tm,),
        in_specs=[row(d),
                  pl.BlockSpec((None, 6, d), lambda i: (mod_row_of_tile(i), 0, 0)),
                  tab, tab, tab,
                  _resident(w_in_b.shape), _resident(wup2.shape), _resident(bg2.shape)],
        out_specs=[row(n) for n, _ in outs],
        out_shape=[jax.ShapeDtypeStruct((r, n), dt) for n, dt in outs],
        compiler_params=_params(("parallel",)),
        name="in_proj_rope" if rope else "in_proj_ctx",
    )(xf, mod3, cos_t, sina_t, sinb_t, w_in_b, wup2, bg2)
    res = iter(res)
    return tuple(next(res) if wanted else None for _, wanted, _, _ in spec) + tuple(res)


def _attn_kernel(sink_ref, q_ref, *refs):
    n_kv = Q_BLOCKS + 2
    k_refs, v_refs = refs[:n_kv], refs[n_kv:2 * n_kv]
    kx_ref, vx_ref, o_ref = refs[2 * n_kv:]
    n = pl.program_id(1)
    last = pl.num_programs(1) - 1
    blk = A_BLOCK
    rows = A_GROUP * blk
    n_ctx = kx_ref.shape[0]
    qi = lax.broadcasted_iota(jnp.int32, (rows, blk), 0) % blk
    kj = lax.broadcasted_iota(jnp.int32, (rows, blk), 1)
    for u in range(Q_BLOCKS):
        ok_prev = (kj >= qi) & (n > 0) if u == 0 else (kj >= qi)
        ok_next = (kj <= qi) & (n < last) if u == Q_BLOCKS - 1 else (kj <= qi)
        qs = slice(u * blk, (u + 1) * blk)
        kp_ref, kc_ref, kn_ref = k_refs[u:u + 3]
        vp_ref, vc_ref, vn_ref = v_refs[u:u + 3]
        for hk in range(kc_ref.shape[1] // HEAD_DIM):
            ks = slice(hk * HEAD_DIM, (hk + 1) * HEAD_DIM)
            q4 = jnp.concatenate(
                [q_ref[qs, (hk * A_GROUP + g) * HEAD_DIM:(hk * A_GROUP + g + 1) * HEAD_DIM]
                 for g in range(A_GROUP)], axis=0)
            s_ctx = _bdot_nt(q4, kx_ref[:, ks])
            s_prev = jnp.where(ok_prev, _bdot_nt(q4, kp_ref[:, ks]), -jnp.inf)
            s_cur = _bdot_nt(q4, kc_ref[:, ks])
            s_next = jnp.where(ok_next, _bdot_nt(q4, kn_ref[:, ks]), -jnp.inf)
            sink = jnp.concatenate(
                [jnp.full((blk, 1), sink_ref[hk * A_GROUP + g] * LOG2_E, F32) for g in range(A_GROUP)], axis=0)
            m_lanes = jnp.maximum(jnp.maximum(s_prev, s_cur), s_next)
            for l0 in range(0, n_ctx, LANES):
                m_lanes = jnp.maximum(m_lanes, s_ctx[:, l0:l0 + LANES])
            m = jnp.maximum(jnp.max(m_lanes, axis=-1, keepdims=True), sink)
            e_ctx = jnp.exp2(s_ctx - m)
            e_prev = jnp.exp2(s_prev - m)
            e_cur = jnp.exp2(s_cur - m)
            e_next = jnp.exp2(s_next - m)
            e_lanes = e_prev + e_cur + e_next
            for l0 in range(0, n_ctx, LANES):
                e_lanes = e_lanes + e_ctx[:, l0:l0 + LANES]
            den = jnp.exp2(sink - m) + jnp.sum(e_lanes, axis=-1, keepdims=True)
            o = (_bdot(e_ctx.astype(BF16), vx_ref[:, ks])
                 + (_bdot(e_prev.astype(BF16), vp_ref[:, ks])
                    + _bdot(e_cur.astype(BF16), vc_ref[:, ks])
                    + _bdot(e_next.astype(BF16), vn_ref[:, ks]))) * (1.0 / den)
            for g in range(A_GROUP):
                hq = hk * A_GROUP + g
                o_ref[qs, hq * HEAD_DIM:(hq + 1) * HEAD_DIM] = o[g * blk:(g + 1) * blk, :].astype(o_ref.dtype)


def _attention(sink, qa, ka, va, ka_c, va_c, batch, seq, n_ctx):
    blk = A_BLOCK
    nb = seq // blk
    assert nb % Q_BLOCKS == 0
    steps = nb // Q_BLOCKS
    aw = qa.shape[1]
    kvw = ka.shape[1]
    kv = lambda j: pl.BlockSpec(
        (blk, kvw), lambda b, n: (b * nb + jnp.clip(n * Q_BLOCKS - 1 + j, 0, nb - 1), 0))
    kv_specs = [kv(j) for j in range(Q_BLOCKS + 2)]
    qo = pl.BlockSpec((Q_BLOCKS * blk, aw), lambda b, n: (b * steps + n, 0))
    ctx = pl.BlockSpec((n_ctx, kvw), lambda b, n: (b, 0))
    return pl.pallas_call(
        _attn_kernel,
        grid=(batch, steps),
        in_specs=[pl.BlockSpec(memory_space=pltpu.SMEM), qo] + kv_specs + kv_specs + [ctx, ctx],
        out_specs=qo,
        out_shape=jax.ShapeDtypeStruct((batch * seq, aw), BF16),
        compiler_params=_params(("parallel", "parallel")),
        name="window_gqa",
    )(sink, qa, *([ka] * (Q_BLOCKS + 2)), *([va] * (Q_BLOCKS + 2)), ka_c, va_c)


def _chunk_cumsum(g, *, reverse):
    rows = g.shape[0]
    p = lax.broadcasted_iota(jnp.int32, g.shape, 0) % GLA_CHUNK
    s = 1
    while s < GLA_CHUNK:
        if reverse:
            g = g + jnp.where(p < GLA_CHUNK - s, pltpu.roll(g, rows - s, 0), 0.0)
        else:
            g = g + jnp.where(p >= s, pltpu.roll(g, s, 0), 0.0)
        s *= 2
    return g


def _per_chunk_row(x, i):
    c = GLA_CHUNK
    return jnp.concatenate(
        [jnp.broadcast_to(x[j * c + i:j * c + i + 1, :], (c, x.shape[1])) for j in range(x.shape[0] // c)], axis=0)


def _gla_block(q, k, v, cum, state_t, mask, *, reverse, need_o):
    c = GLA_CHUNK
    rows, dk = k.shape
    n_chunks = rows // c
    k = k.astype(F32)
    i_last = 0 if reverse else c - 1
    i_mid = c // 2 if reverse else c // 2 - 1
    b_last = _per_chunk_row(cum, i_last)
    kdec = (k * jnp.exp(b_last - cum)).astype(BF16)
    if need_o:
        b_mid = _per_chunk_row(cum, i_mid)
        qc = q.astype(F32) * dk ** -0.5
        qm = (qc * jnp.exp(cum - b_mid)).astype(BF16)
        km = (k * jnp.exp(b_mid - cum)).astype(BF16)
        a = jnp.where(mask, _bdot_nt(qm, km), 0.0)
        o_intra = _bdot(a.astype(BF16), v)
        qe = (qc * jnp.exp(cum)).astype(BF16)
    o_inter = [None] * n_chunks
    for j in (reversed(range(n_chunks)) if reverse else range(n_chunks)):
        rs = slice(j * c, (j + 1) * c)
        if need_o:
            o_inter[j] = _bdot_nt(qe[rs, :], state_t.astype(BF16))
        kv_t = lax.dot_general(v[rs, :], kdec[rs, :], (((0,), (0,)), ((), ())), preferred_element_type=F32)
        state_t = state_t * jnp.exp(cum[j * c + i_last:j * c + i_last + 1, :]) + kv_t
    o = o_intra + jnp.concatenate(o_inter, axis=0) if need_o else None
    return o, state_t


def _gla_kernel(*refs, reverse, final):
    if final:
        (kx_ref, vx_ref, gx_ref, q_ref, k_ref, v_ref, g_ref, ob_ref, r_ref, nw_ref, o_ref, st_ref) = refs
    else:
        (kx_ref, vx_ref, gx_ref, q_ref, k_ref, v_ref, g_ref, o_ref, st_ref) = refs
    t = pl.program_id(1)
    n_heads = st_ref.shape[0]
    dv, dk = st_ref.shape[1:]

    @pl.when(t == 0)
    def _context():
        cum = _chunk_cumsum(gx_ref[...], reverse=reverse)
        for h in range(n_heads):
            ks, vs = slice(h * dk, (h + 1) * dk), slice(h * dv, (h + 1) * dv)
            _, st = _gla_block(None, kx_ref[:, ks], vx_ref[:, vs], cum[:, ks], jnp.zeros((dv, dk), F32), None,
                               reverse=reverse, need_o=False)
            st_ref[h] = st

    @pl.when(t > 0)
    def _latent():
        rows = k_ref.shape[0]
        cum = _chunk_cumsum(g_ref[...], reverse=reverse)
        r = lax.broadcasted_iota(jnp.int32, (rows, rows), 0)
        s = lax.broadcasted_iota(jnp.int32, (rows, rows), 1)
        causal = (s >= r) if reverse else (s <= r)
        mask = causal & ((r // GLA_CHUNK) == (s // GLA_CHUNK))
        for h in range(n_heads):
            ks, vs = slice(h * dk, (h + 1) * dk), slice(h * dv, (h + 1) * dv)
            o, st = _gla_block(q_ref[:, ks], k_ref[:, ks], v_ref[:, vs], cum[:, ks], st_ref[h], mask,
                               reverse=reverse, need_o=True)
            st_ref[h] = st
            if final:
                o = o + ob_ref[:, vs].astype(F32)
                o = o * lax.rsqrt(jnp.mean(o * o, axis=-1, keepdims=True) + LN_EPS)
                o = o * nw_ref[:, vs]
                o = o * _silu(r_ref[:, vs].astype(F32))
            o_ref[:, vs] = o.astype(o_ref.dtype)


def _gla(kb_c, vb_c, la_c, qb, kb, vb, la, batch, seq, n_ctx, *, reverse, extra=None):
    assert n_ctx % GLA_CHUNK == 0 and GLA_STEP % GLA_CHUNK == 0
    tb = GLA_STEP
    nt = seq // tb
    kw, vw = qb.shape[1], vb.shape[1]
    final = extra is not None

    def lat(b, t):
        i = jnp.maximum(t - 1, 0)
        if reverse:
            i = nt - 1 - i
        return (b * nt + i, 0)

    cx = lambda b, t: (b, 0)
    in_specs = [pl.BlockSpec((n_ctx, kw), cx), pl.BlockSpec((n_ctx, vw), cx), pl.BlockSpec((n_ctx, kw), cx),
                pl.BlockSpec((tb, kw), lat), pl.BlockSpec((tb, kw), lat),
                pl.BlockSpec((tb, vw), lat), pl.BlockSpec((tb, kw), lat)]
    args = [kb_c, vb_c, la_c, qb, kb, vb, la]
    if final:
        o_other, rb, norm_w = extra
        in_specs += [pl.BlockSpec((tb, vw), lat), pl.BlockSpec((tb, vw), lat), _resident((1, vw))]
        args += [o_other, rb, norm_w]
    return pl.pallas_call(
        functools.partial(_gla_kernel, reverse=reverse, final=final),
        grid=(batch, nt + 1),
        in_specs=in_specs,
        out_specs=pl.BlockSpec((tb, vw), lat),
        out_shape=jax.ShapeDtypeStruct((batch * seq, vw), BF16),
        scratch_shapes=[pltpu.VMEM((B_HEADS, vw // B_HEADS, kw // B_HEADS), F32)],
        compiler_params=_params(("parallel", "arbitrary")),
        name="gla_fwd_out" if final else "gla_bwd",
    )(*args)


def _outproj_kernel(oa_ref, ob_ref, x_ref, mod_ref, wt_ref, wb_ref, lng_ref, lnb_ref, wr_ref, br_ref,
                    x1_o, h2_o, rw_o, ri_o):
    for r0 in range(0, x_ref.shape[0], SUB_ROWS):
        _outproj_rows(slice(r0, r0 + SUB_ROWS), oa_ref, ob_ref, x_ref, mod_ref, wt_ref, wb_ref, lng_ref, lnb_ref,
                      wr_ref, br_ref, x1_o, h2_o, rw_o, ri_o)


def _outproj_rows(rs, oa_ref, ob_ref, x_ref, mod_ref, wt_ref, wb_ref, lng_ref, lnb_ref, wr_ref, br_ref,
                  x1_o, h2_o, rw_o, ri_o):
    y = _bdot(oa_ref[rs, :], wt_ref[...]) + _bdot(ob_ref[rs, :], wb_ref[...])
    x1 = _ln(ALPHA * x_ref[rs, :] + mod_ref[2:3, :] * y) * lng_ref[...] + lnb_ref[...]
    x1_o[rs, :] = x1
    h2 = _ln(x1) * (1.0 + mod_ref[4:5, :]) + mod_ref[3:4, :]
    h2_o[rs, :] = h2
    lg = _bdot(h2.astype(BF16), wr_ref[...]) + br_ref[...]
    lane = lax.broadcasted_iota(jnp.int32, lg.shape, 1)
    lanef = lane.astype(F32)
    big = float(LANES)
    is_g = (lane >= N_EXPERTS) & (lane < N_EXPERTS + N_GROUPS)
    gl = jnp.where(is_g, lg, -jnp.inf)
    gmax = jnp.max(gl, axis=-1, keepdims=True)
    pg_top = 1.0 / jnp.sum(jnp.exp(gl - gmax), axis=-1, keepdims=True)
    grp = jnp.min(jnp.where(gl == gmax, lanef, big), axis=-1, keepdims=True) - N_EXPERTS
    in_grp = (lane < N_EXPERTS) & ((lane // EXPERTS_PER_GROUP).astype(F32) == grp)
    el = jnp.where(in_grp, lg, -jnp.inf)
    m1 = jnp.max(el, axis=-1, keepdims=True)
    i1 = jnp.min(jnp.where(el == m1, lanef, big), axis=-1, keepdims=True)
    el2 = jnp.where(lanef == i1, -jnp.inf, el)
    m2 = jnp.max(el2, axis=-1, keepdims=True)
    i2 = jnp.min(jnp.where(el2 == m2, lanef, big), axis=-1, keepdims=True)
    e2 = jnp.exp(m2 - m1)
    w1 = pg_top / (1.0 + e2)
    w2 = pg_top * e2 / (1.0 + e2)
    rw_o[rs, :] = jnp.where(lane == 0, w1, jnp.where(lane == 1, w2, 0.0))
    ri_o[rs, :] = jnp.where(lane == 0, i1, jnp.where(lane == 1, i2, 0.0)).astype(jnp.int32)


def _outproj(out_a, out_b, xf, mod3, w_top, w_bot, ln_g, ln_b, wr, br, seq):
    r, d = xf.shape
    tm = OUT_TILE
    assert seq % tm == 0 and tm % SUB_ROWS == 0
    tiles_per_seq = seq // tm
    row = lambda n: pl.BlockSpec((tm, n), lambda i: (i, 0))
    return pl.pallas_call(
        _outproj_kernel,
        grid=(r // tm,),
        in_specs=[row(out_a.shape[1]), row(out_b.shape[1]), row(d),
                  pl.BlockSpec((None, 6, d), lambda i: (i // tiles_per_seq, 0, 0)),
                  _resident(w_top.shape), _resident(w_bot.shape),
                  _resident((1, d)), _resident((1, d)), _resident(wr.shape), _resident(br.shape)],
        out_specs=[row(d), row(d), row(LANES), row(LANES)],
        out_shape=[jax.ShapeDtypeStruct((r, d), F32), jax.ShapeDtypeStruct((r, d), F32),
                   jax.ShapeDtypeStruct((r, LANES), F32), jax.ShapeDtypeStruct((r, LANES), jnp.int32)],
        compiler_params=_params(("parallel",)),
        name="out_proj_router",
    )(out_a, out_b, xf, mod3, w_top, w_bot, ln_g.reshape(1, d), ln_b.reshape(1, d), wr, br)


def _row_copy(src_ref, src_row, dst_ref, dst_row, sem):
    return pltpu.make_async_copy(src_ref.at[pl.ds(src_row, 1)], dst_ref.at[pl.ds(dst_row, 1)], sem)


def _dispatch_kernel(dest_ref, tail_ref, empty_ref, h_ref, buf_ref, zero_ref, hbuf, sem, zsem, lsem, rsem):
    tm = hbuf.shape[1]
    step = pl.program_id(0)

    def zero_block(b):
        row = pl.multiple_of(b * MOE_BLOCK, MOE_BLOCK)
        return pltpu.make_async_copy(zero_ref, buf_ref.at[pl.ds(row, MOE_BLOCK)], zsem)

    def for_empty_blocks(fn):
        def body(b, carry):
            @pl.when(empty_ref[b] != 0)
            def _():
                fn(b)
            return carry
        lax.fori_loop(0, empty_ref.shape[0], body, 0)

    @pl.when(step == 0)
    def _zero_fill():
        zero_ref[...] = jnp.zeros(zero_ref.shape, zero_ref.dtype)
        for e in range(N_EXPERTS):
            start = pl.multiple_of(tail_ref[e] // SUBLANES * SUBLANES, SUBLANES)
            pltpu.make_async_copy(zero_ref, buf_ref.at[pl.ds(start, MOE_BLOCK)], sem).start()
        for e in range(N_EXPERTS):
            pltpu.make_async_copy(zero_ref, buf_ref.at[pl.ds(0, MOE_BLOCK)], sem).wait()
        for_empty_blocks(lambda b: zero_block(b).start())

    @pl.when(step == pl.num_programs(0) - 1)
    def _zero_done():
        for_empty_blocks(lambda b: zero_block(b).wait())

    n_slots = hbuf.shape[0]
    slot = step % n_slots

    def load(tile, sl):
        row = pl.multiple_of(tile * tm, tm)
        return pltpu.make_async_copy(h_ref.at[pl.ds(row, tm)], hbuf.at[sl], lsem.at[sl])

    @pl.when(step == 0)
    def _():
        load(0, 0).start()

    @pl.when(step + 1 < pl.num_programs(0))
    def _():
        load(step + 1, (step + 1) % n_slots).start()

    load(step, slot).wait()

    def issue(i, carry):
        for k in range(TOP_K):
            _row_copy(hbuf.at[slot], i, buf_ref, dest_ref[0, 0, TOP_K * i + k], rsem.at[slot]).start()
        return carry

    lax.fori_loop(0, tm, issue, 0, unroll=DMA_UNROLL)

    def wait_tile(sl):
        for _ in range(TOP_K):
            pltpu.make_async_copy(hbuf.at[sl], buf_ref.at[pl.ds(0, tm)], rsem.at[sl]).wait()

    pl.when(step > 0)(lambda: wait_tile((step + n_slots - 1) % n_slots))
    pl.when(step == pl.num_programs(0) - 1)(lambda: wait_tile(slot))


def _dispatch(dest3, tail_row, empty_block, h2):
    r, d = h2.shape
    tm = ROW_TILE
    n_rows = empty_block.shape[0] * MOE_BLOCK
    return pl.pallas_call(
        _dispatch_kernel,
        grid=(r // tm,),
        in_specs=[pl.BlockSpec((1, 1, TOP_K * tm), lambda i: (i, 0, 0), memory_space=pltpu.SMEM),
                  pl.BlockSpec(memory_space=pltpu.SMEM),
                  pl.BlockSpec(memory_space=pltpu.SMEM),
                  pl.BlockSpec(memory_space=pl.ANY)],
        out_specs=pl.BlockSpec(memory_space=pl.ANY),
        out_shape=jax.ShapeDtypeStruct((n_rows, d), h2.dtype),
        scratch_shapes=[pltpu.VMEM((MOE_BLOCK, d), h2.dtype), pltpu.VMEM((3, tm, d), h2.dtype),
                        pltpu.SemaphoreType.DMA, pltpu.SemaphoreType.DMA, pltpu.SemaphoreType.DMA((3,)),
                        pltpu.SemaphoreType.DMA((3,))],
        compiler_params=_params(("arbitrary",)),
        name="moe_dispatch",
    )(dest3, tail_row, empty_block, h2)


def _expert_kernel(bq_ref, lo_ref, hi_ref, el_ref, nq_ref, ub_ref, x_ref, wg_hbm, wu_hbm, wd_hbm, o_ref,
                   wgb, wub, wdb, sg, su, sd, sems):
    b = pl.program_id(0)
    q = bq_ref[b]
    slot = q % 2
    mats = ((wg_hbm, sg, wgb), (wu_hbm, su, wub), (wd_hbm, sd, wdb))
    n_pieces = W_PIECES * len(mats)

    def piece_copy(m, qt, t):
        hbm, stage, _ = mats[m]
        pr = stage.shape[1]
        return pltpu.make_async_copy(hbm.at[el_ref[qt], pl.ds(t * pr, pr), :], stage.at[t % 2],
                                     sems.at[m, t % 2])

    def process(qt, lo, hi):
        for p in range(n_pieces):
            m, t = p % len(mats), p // len(mats)

            @pl.when((lo <= p) & (p < hi))
            def _():
                _, stage, resident = mats[m]
                pr = stage.shape[1]
                piece_copy(m, qt, t).wait()
                resident[qt % 2, pl.ds(t * pr, pr), :] = stage[t % 2].astype(BF16)
                q2 = qt + (t + 2) // W_PIECES

                @pl.when(q2 < nq_ref[0])
                def _():
                    piece_copy(m, q2, (t + 2) % W_PIECES).start()

    @pl.when(b == 0)
    def _first_expert():
        for m in range(len(mats)):
            for t in range(2):
                piece_copy(m, 0, t).start()
        process(0, 0, n_pieces)

    @pl.when(b < ub_ref[0])
    def _compute():
        xb = x_ref[...].astype(BF16)
        hid = wgb.shape[2]
        acts = []
        for h0 in range(0, hid, HID_TILE):
            hs = pl.ds(h0, HID_TILE)
            acts.append((_silu(_bdot(xb, wgb[slot, :, hs])) * _bdot(xb, wub[slot, :, hs])).astype(BF16))
        o_ref[...] = _bdot(jnp.concatenate(acts, axis=1), wdb[slot])

    @pl.when(b >= ub_ref[0])
    def _unused():
        o_ref[...] = jnp.zeros(o_ref.shape, o_ref.dtype)

    process(q + 1, lo_ref[b], hi_ref[b])


def _experts(plan, buf, wg, wu, wd, n_out_rows):
    d = buf.shape[1]
    hid = wg.shape[2]
    assert d % W_PIECES == 0 and hid % W_PIECES == 0 and W_PIECES % 2 == 0
    blk = lambda f: pl.BlockSpec((MOE_BLOCK, d), f)
    return pl.pallas_call(
        _expert_kernel,
        grid_spec=pltpu.PrefetchScalarGridSpec(
            num_scalar_prefetch=len(plan),
            grid=(n_out_rows // MOE_BLOCK,),
            in_specs=[blk(lambda b, bq, lo, hi, el, nq, ub: (jnp.minimum(b, ub[0] - 1), 0)),
                      pl.BlockSpec(memory_space=pl.ANY), pl.BlockSpec(memory_space=pl.ANY),
                      pl.BlockSpec(memory_space=pl.ANY)],
            out_specs=blk(lambda b, bq, lo, hi, el, nq, ub: (b, 0)),
            scratch_shapes=[pltpu.VMEM((2, d, hid), BF16), pltpu.VMEM((2, d, hid), BF16),
                            pltpu.VMEM((2, hid, d), BF16),
                            pltpu.VMEM((2, d // W_PIECES, hid), F32), pltpu.VMEM((2, d // W_PIECES, hid), F32),
                            pltpu.VMEM((2, hid // W_PIECES, d), F32),
                            pltpu.SemaphoreType.DMA((3, 2))],
        ),
        out_shape=jax.ShapeDtypeStruct((n_out_rows, d), F32),
        compiler_params=_params(("arbitrary",)),
        name="moe_experts",
    )(*plan, buf, wg, wu, wd)


def _combine_kernel(dest_ref, dnext_ref, y_ref, rw_ref, x1_ref, mod_ref, lng_ref, lnb_ref, o_ref,
                    g_even, g_odd, sem):
    tm = x1_ref.shape[0]
    i = pl.program_id(0)
    n_groups = tm // DMA_UNROLL
    has_next = i + 1 < pl.num_programs(0)

    def gather(d_ref, g_ref, sl):
        def body(j, carry):
            for r in range(DMA_UNROLL):
                row = j * DMA_UNROLL + r
                for k in range(TOP_K):
                    _row_copy(y_ref, d_ref[0, 0, TOP_K * row + k], g_ref.at[k], row, sem.at[sl]).start()
            return carry
        lax.fori_loop(0, n_groups, body, 0)

    @pl.when(i == 0)
    def _():
        gather(dest_ref, g_even, 0)

    def run(sl, g_cur, g_next, prefetch):
        if prefetch:
            gather(dnext_ref, g_next, 1 - sl)
        for k in range(TOP_K):
            pltpu.make_async_copy(y_ref.at[pl.ds(0, tm)], g_cur.at[k], sem.at[sl]).wait()
        f = rw_ref[:, 0:1] * g_cur[0] + rw_ref[:, 1:2] * g_cur[1]
        o_ref[...] = _ln(ALPHA * x1_ref[...] + mod_ref[5:6, :] * f) * lng_ref[...] + lnb_ref[...]

    for sl, g_cur, g_next in ((0, g_even, g_odd), (1, g_odd, g_even)):
        for prefetch in (True, False):
            pl.when((i % 2 == sl) & (has_next == prefetch))(functools.partial(run, sl, g_cur, g_next, prefetch))


def _combine(dest3, y, rw, x1, mod3, ln_g, ln_b, tiles_per_seq):
    r, d = x1.shape
    tm = ROW_TILE
    n = r // tm
    row = lambda w: pl.BlockSpec((tm, w), lambda i: (i, 0))
    return pl.pallas_call(
        _combine_kernel,
        grid=(n,),
        in_specs=[pl.BlockSpec((1, 1, TOP_K * tm), lambda i: (i, 0, 0), memory_space=pltpu.SMEM),
                  pl.BlockSpec((1, 1, TOP_K * tm), lambda i: (jnp.minimum(i + 1, n - 1), 0, 0),
                               memory_space=pltpu.SMEM),
                  pl.BlockSpec(memory_space=pl.ANY),
                  row(LANES), row(d),
                  pl.BlockSpec((None, 6, d), lambda i: (i // tiles_per_seq, 0, 0)),
                  _resident((1, d)), _resident((1, d))],
        out_specs=row(d),
        out_shape=jax.ShapeDtypeStruct((r, d), F32),
        scratch_shapes=[pltpu.VMEM((TOP_K, tm, d), F32), pltpu.VMEM((TOP_K, tm, d), F32),
                        pltpu.SemaphoreType.DMA((2,))],
        compiler_params=_params(("arbitrary",)),
        name="moe_combine_ln",
    )(dest3, dest3, y, rw, x1, mod3, ln_g.reshape(1, d), ln_b.reshape(1, d))


def _routing_plan(ri, n_tokens):
    eid = ri[:, :TOP_K].reshape(-1)
    m = n_tokens * TOP_K
    onehot = (eid[:, None] == jnp.arange(N_EXPERTS, dtype=jnp.int32)[None, :]).astype(jnp.int32)
    csum = jnp.cumsum(onehot, axis=0)
    counts = csum[-1]
    rank = jnp.sum(csum * onehot, axis=1) - 1
    padded = (counts + MOE_BLOCK - 1) // MOE_BLOCK * MOE_BLOCK
    pad_end = jnp.cumsum(padded)
    pad_start = pad_end - padded
    dest = jnp.sum(onehot * pad_start[None, :], axis=1) + rank
    tail = pad_start + counts
    n_blocks = m // MOE_BLOCK + N_EXPERTS
    used_blocks = pad_end[-1] // MOE_BLOCK
    empty = jnp.arange(n_blocks + 1, dtype=jnp.int32) >= used_blocks

    n_pieces = 3 * W_PIECES
    has = counts > 0
    ordinal = jnp.cumsum(has.astype(jnp.int32)) - 1
    n_ord = ordinal[-1] + 1
    e_ids = jnp.arange(N_EXPERTS, dtype=jnp.int32)
    ord_expert = jnp.sum(jnp.where(has[None, :] & (ordinal[None, :] == e_ids[:, None]), e_ids[None, :], 0), axis=1)
    bid = jnp.arange(n_blocks, dtype=jnp.int32)
    brow = bid[:, None] * MOE_BLOCK
    in_e = ((pad_start[None, :] <= brow) & (brow < pad_end[None, :])).astype(jnp.int32)
    pick = lambda v: jnp.sum(in_e * v[None, :], axis=1)
    used = bid < used_blocks
    blk_q = jnp.where(used, pick(ordinal), n_ord - 1)
    i_in_e = bid - pick(pad_start) // MOE_BLOCK
    k_e = jnp.maximum(pick(padded) // MOE_BLOCK, 1)
    brings = used & (blk_q + 1 < n_ord)
    lo = jnp.where(brings, n_pieces * i_in_e // k_e, 0)
    hi = jnp.where(brings, n_pieces * (i_in_e + 1) // k_e, 0)
    i32 = lambda a: a.astype(jnp.int32)
    plan = (i32(blk_q), i32(lo), i32(hi), i32(ord_expert), i32(n_ord.reshape(1)), i32(used_blocks.reshape(1)))
    return i32(dest), i32(tail), i32(empty), plan, n_blocks * MOE_BLOCK


def _rope_tables(seq):
    n_freq = HEAD_DIM // 4
    inv_freq = ROPE_THETA ** (-jnp.arange(n_freq, dtype=F32) / n_freq)
    rows = seq // GRID_W
    ar = jnp.arange(rows, dtype=F32)[:, None] * inv_freq
    ac = jnp.arange(GRID_W, dtype=F32)[:, None] * inv_freq
    by_row = lambda t: jnp.broadcast_to(t[:, None, :], (rows, GRID_W, n_freq)).reshape(seq, n_freq)
    by_col = lambda t: jnp.broadcast_to(t[None, :, :], (rows, GRID_W, n_freq)).reshape(seq, n_freq)
    cos_r, sin_r, cos_c, sin_c = by_row(jnp.cos(ar)), by_row(jnp.sin(ar)), by_col(jnp.cos(ac)), by_col(jnp.sin(ac))
    zero = jnp.zeros_like(cos_r)
    cos_t = jnp.concatenate([cos_r, cos_r, cos_c, cos_c], axis=1)
    sin_a = jnp.concatenate([-sin_r, zero, -sin_c, zero], axis=1)
    sin_b = jnp.concatenate([zero, sin_r, zero, sin_c], axis=1)
    return cos_t, sin_a, sin_b


def kernel(x, c, ctx, c_ctx, w_ada, b_ada, w_in, w_gate_up, b_gate, attn_sink, gla_norm_w, w_out, ln1_g, ln1_b, w_router_group, b_router_group, w_router_expert, b_router_expert, w_exp_gate, w_exp_up, w_exp_down, ln2_g, ln2_b):
    batch, seq, d = x.shape
    n_ctx = ctx.shape[1]
    assert w_ada.shape[0] == DEPTH and batch < MOD_ROWS
    assert seq % ROW_TILE == 0 and (batch * n_ctx) % ROW_TILE == 0 and seq % GLA_STEP == 0 and n_ctx == GLA_STEP
    n_tok = batch * seq
    tiles_per_seq = seq // ROW_TILE
    a_width = d // 2
    kv_width = a_width // A_GROUP
    b_width = d - a_width
    key_width = b_width // 2
    layer = 0

    cc = jnp.concatenate([c, c_ctx[None, :], jnp.zeros((MOD_ROWS - batch - 1, d), F32)], axis=0)
    mod3 = _adaln(cc, w_ada[layer], b_ada[layer]).reshape(MOD_ROWS, 6, d)

    splits = (a_width, kv_width, kv_width, key_width, key_width, b_width, b_width, 2 * GATE_RANK)
    w_in_b = w_in[layer].astype(BF16)
    zero_up = jnp.zeros((GATE_RANK, key_width), F32)
    wup2 = jnp.concatenate([jnp.concatenate([w_gate_up[layer, 0], zero_up], axis=1),
                            jnp.concatenate([zero_up, w_gate_up[layer, 1]], axis=1)], axis=0).astype(BF16)
    bg2 = b_gate[layer].reshape(1, 2 * key_width)
    tables = _rope_tables(seq)

    xf = x.reshape(n_tok, d)
    qa, ka, va, qb, kb, vb, rb, la_f, la_b = _project(
        xf, mod3, lambda i, per_seq: i // per_seq, tables, w_in_b, splits, wup2, bg2, rope=True, seq=seq)
    _, ka_c, va_c, _, kb_c, vb_c, _, lac_f, lac_b = _project(
        ctx.reshape(batch * n_ctx, d), mod3, lambda i, per_seq: batch, tables, w_in_b, splits, wup2, bg2,
        rope=False, seq=seq, keys_values_only=True)

    out_a = _attention(attn_sink[layer], qa, ka, va, ka_c, va_c, batch, seq, n_ctx)
    o_b = _gla(kb_c, vb_c, lac_b, qb, kb, vb, la_b, batch, seq, n_ctx, reverse=True)
    out_b = _gla(kb_c, vb_c, lac_f, qb, kb, vb, la_f, batch, seq, n_ctx, reverse=False,
                 extra=(o_b, rb, gla_norm_w[layer].reshape(1, b_width)))

    w_out_b = w_out[layer].astype(BF16)
    wr = jnp.concatenate([w_router_expert[layer], w_router_group[layer],
                          jnp.zeros((d, LANES - N_EXPERTS - N_GROUPS), F32)], axis=1).astype(BF16)
    br = jnp.concatenate([b_router_expert[layer], b_router_group[layer],
                          jnp.zeros((LANES - N_EXPERTS - N_GROUPS,), F32)]).reshape(1, LANES)
    x1, h2, rw, ri = _outproj(out_a, out_b, xf, mod3, w_out_b[:a_width], w_out_b[a_width:],
                              ln1_g[layer], ln1_b[layer], wr, br, seq)

    dest, tail_row, empty_block, plan, n_buf_rows = _routing_plan(ri, n_tok)
    dest3 = dest.reshape(n_tok // ROW_TILE, 1, TOP_K * ROW_TILE)
    buf = _dispatch(dest3, tail_row, empty_block, h2)
    y = _experts(plan, buf, w_exp_gate[layer], w_exp_up[layer], w_exp_down[layer], n_buf_rows)
    out = _combine(dest3, y, rw, x1, mod3, ln2_g[layer], ln2_b[layer], tiles_per_seq)
    return out.reshape(batch, seq, d)
```

```python
import functools

import jax
import jax.numpy as jnp
from jax import lax
from jax.experimental import pallas as pl
from jax.experimental.pallas import tpu as pltpu

F32 = jnp.float32
BF16 = jnp.bfloat16

HEAD_DIM = 128
GRID_W = 64
WINDOW = 128
A_BLOCK = 128
A_GROUP = 4
ROPE_THETA = 10000.0
B_HEADS = 4
GATE_RANK = 16
GATE_TAU = 16.0
GLA_CHUNK = 64
N_GROUPS = 4
EXPERTS_PER_GROUP = 8
N_EXPERTS = N_GROUPS * EXPERTS_PER_GROUP
TOP_K = 2
DEPTH = 1
ALPHA = (2.0 * DEPTH) ** 0.25
LN_EPS = 1e-6
LOG2_E = 1.4426950408889634
ATTN_EXP2_SCALE = HEAD_DIM ** -0.5 * LOG2_E

LANES = 128
SUBLANES = 8
MOD_ROWS = 8
VMEM_LIMIT = 56 * 1024 * 1024

DISPATCH_TILE = 1024
COMBINE_TILE = 256
OUT_TILE = 512
SUB_ROWS = 256
PROJ_COLS = 1024
Q_BLOCKS = 4
GLA_STEP = 256
MOE_BLOCK = 256
HID_TILE = 512
W_PIECES = 4
DMA_UNROLL = 8


def _params(sem):
    return pltpu.CompilerParams(dimension_semantics=sem, vmem_limit_bytes=VMEM_LIMIT)


def _resident(shape):
    nd = len(shape)
    return pl.BlockSpec(shape, lambda *_: (0,) * nd, pipeline_mode=pl.Buffered(1))


def _ln(x):
    mu = jnp.mean(x, axis=-1, keepdims=True)
    xc = x - mu
    var = jnp.mean(xc * xc, axis=-1, keepdims=True)
    return xc * lax.rsqrt(var + LN_EPS)


def _silu(x):
    return x * jax.nn.sigmoid(x)


def _bdot(a, b):
    return jnp.dot(a, b, preferred_element_type=F32)


def _bdot_nt(a, b):
    return lax.dot_general(a, b, (((1,), (1,)), ((), ())), preferred_element_type=F32)


def _adaln_kernel(c_ref, w_ref, b_ref, o_ref):
    s = _silu(c_ref[...]).astype(BF16)
    o_ref[...] = _bdot(s, w_ref[...].astype(BF16)) + b_ref[...]


def _adaln(cc, w_ada, b_ada):
    d, n = w_ada.shape
    tn = 1024
    return pl.pallas_call(
        _adaln_kernel,
        grid=(n // tn,),
        in_specs=[pl.BlockSpec((MOD_ROWS, d), lambda j: (0, 0)),
                  pl.BlockSpec((d, tn), lambda j: (0, j)),
                  pl.BlockSpec((1, tn), lambda j: (0, j))],
        out_specs=pl.BlockSpec((MOD_ROWS, tn), lambda j: (0, j)),
        out_shape=jax.ShapeDtypeStruct((MOD_ROWS, n), F32),
        compiler_params=_params(("arbitrary",)),
        name="adaln",
    )(cc, w_ada, b_ada.reshape(1, n))


def _proj_kernel(x_ref, mod_ref, cos_ref, sina_ref, sinb_ref, w_ref, wup, bg, *out_refs, groups, rope):
    for r0 in range(0, x_ref.shape[0], SUB_ROWS):
        _proj_rows(slice(r0, r0 + SUB_ROWS), x_ref, mod_ref, cos_ref, sina_ref, sinb_ref, w_ref, wup, bg,
                   out_refs, groups, rope)


def _proj_rows(rs, x_ref, mod_ref, cos_ref, sina_ref, sinb_ref, w_ref, wup, bg, out_refs, groups, rope):
    h = _ln(x_ref[rs, :]) * (1.0 + mod_ref[1:2, :]) + mod_ref[0:1, :]
    hb = h.astype(BF16)

    def rot(t):
        return (t * cos_ref[rs, :] + pltpu.roll(t, 96, 1) * sina_ref[rs, :]
                + pltpu.roll(t, 32, 1) * sinb_ref[rs, :])

    col = 0
    refs = iter(out_refs)
    for n, wanted, rotary, post in groups:
        if wanted:
            o_ref = next(refs)
            step = min(n, PROJ_COLS)
            for c0 in range(0, n, step):
                t = _bdot(hb, w_ref[:, col + c0:col + c0 + step])
                if rotary and rope:
                    for l0 in range(0, step, LANES):
                        r = rot(t[:, l0:l0 + LANES])
                        o_ref[rs, c0 + l0:c0 + l0 + LANES] = (r if post == 1.0 else r * post).astype(o_ref.dtype)
                else:
                    o_ref[rs, c0:c0 + step] = t.astype(o_ref.dtype)
        col += n
    laf_o, lab_o = refs
    gl = _bdot(hb, w_ref[:, col:]).astype(BF16)
    z = _bdot(gl, wup[...]) + bg[...]
    la = (jnp.minimum(z, 0.0) - jnp.log1p(jnp.exp(-jnp.abs(z)))) / GATE_TAU
    kw = laf_o.shape[1]
    laf_o[rs, :] = la[:, :kw]
    lab_o[rs, :] = la[:, kw:]


def _project(xf, mod3, mod_row, tables, w_in_b, splits, wup2, bg2, *, rope, seq, keys_values_only=False):
    r, d = xf.shape
    tm = OUT_TILE
    assert r % tm == 0 and seq % tm == 0 and tm % SUB_ROWS == 0
    tiles_per_seq = seq // tm
    mod_row_of_tile = lambda i: mod_row(i, tiles_per_seq)
    n_gate = splits[-1]
    assert sum(splits) == w_in_b.shape[1] and (sum(splits) - n_gate) % LANES == 0
    kv = keys_values_only
    spec = ((BF16, not kv, True, ATTN_EXP2_SCALE), (BF16, True, True, 1.0), (BF16, True, False, 1.0),
            (BF16, not kv, False, 1.0), (BF16, True, False, 1.0), (BF16, True, False, 1.0),
            (BF16, not kv, False, 1.0))
    groups = tuple((n, wanted, rotary, post) for n, (_, wanted, rotary, post) in zip(splits, spec))
    key_width = splits[3]
    outs = [(n, dt) for n, (dt, wanted, _, _) in zip(splits, spec) if wanted] + [(key_width, F32)] * 2
    cos_t, sina_t, sinb_t = tables
    row = lambda n: pl.BlockSpec((tm, n), lambda i: (i, 0))
    tab = pl.BlockSpec((tm, LANES), lambda i: (i % tiles_per_seq, 0))
    res = pl.pallas_call(
        functools.partial(_proj_kernel, groups=groups, rope=rope),
        grid=(r // tm,),
        in_specs=[row(d),
                  pl.BlockSpec((None, 6, d), lambda i: (mod_row_of_tile(i), 0, 0)),
                  tab, tab, tab,
                  _resident(w_in_b.shape), _resident(wup2.shape), _resident(bg2.shape)],
        out_specs=[row(n) for n, _ in outs],
        out_shape=[jax.ShapeDtypeStruct((r, n), dt) for n, dt in outs],
        compiler_params=_params(("parallel",)),
        name="in_proj_rope" if rope else "in_proj_ctx",
    )(xf, mod3, cos_t, sina_t, sinb_t, w_in_b, wup2, bg2)
    res = iter(res)
    return tuple(next(res) if wanted else None for _, wanted, _, _ in spec) + tuple(res)


def _attn_kernel(sink_ref, q_ref, *refs):
    n_kv = Q_BLOCKS + 2
    k_refs, v_refs = refs[:n_kv], refs[n_kv:2 * n_kv]
    kx_ref, vx_ref, o_ref = refs[2 * n_kv:]
    n = pl.program_id(1)
    last = pl.num_programs(1) - 1
    blk = A_BLOCK
    rows = A_GROUP * blk
    n_ctx = kx_ref.shape[0]
    qi = lax.broadcasted_iota(jnp.int32, (rows, blk), 0) % blk
    kj = lax.broadcasted_iota(jnp.int32, (rows, blk), 1)
    for u in range(Q_BLOCKS):
        ok_prev = (kj >= qi) & (n > 0) if u == 0 else (kj >= qi)
        ok_next = (kj <= qi) & (n < last) if u == Q_BLOCKS - 1 else (kj <= qi)
        qs = slice(u * blk, (u + 1) * blk)
        kp_ref, kc_ref, kn_ref = k_refs[u:u + 3]
        vp_ref, vc_ref, vn_ref = v_refs[u:u + 3]
        for hk in range(kc_ref.shape[1] // HEAD_DIM):
            ks = slice(hk * HEAD_DIM, (hk + 1) * HEAD_DIM)
            q4 = jnp.concatenate(
                [q_ref[qs, (hk * A_GROUP + g) * HEAD_DIM:(hk * A_GROUP + g + 1) * HEAD_DIM]
                 for g in range(A_GROUP)], axis=0)
            s_ctx = _bdot_nt(q4, kx_ref[:, ks])
            s_prev = jnp.where(ok_prev, _bdot_nt(q4, kp_ref[:, ks]), -jnp.inf)
            s_cur = _bdot_nt(q4, kc_ref[:, ks])
            s_next = jnp.where(ok_next, _bdot_nt(q4, kn_ref[:, ks]), -jnp.inf)
            sink = jnp.concatenate(
                [jnp.full((blk, 1), sink_ref[hk * A_GROUP + g] * LOG2_E, F32) for g in range(A_GROUP)], axis=0)
            m_lanes = jnp.maximum(jnp.maximum(s_prev, s_cur), s_next)
            for l0 in range(0, n_ctx, LANES):
                m_lanes = jnp.maximum(m_lanes, s_ctx[:, l0:l0 + LANES])
            m = jnp.maximum(jnp.max(m_lanes, axis=-1, keepdims=True), sink)
            e_ctx = jnp.exp2(s_ctx - m)
            e_prev = jnp.exp2(s_prev - m)
            e_cur = jnp.exp2(s_cur - m)
            e_next = jnp.exp2(s_next - m)
            e_lanes = e_prev + e_cur + e_next
            for l0 in range(0, n_ctx, LANES):
                e_lanes = e_lanes + e_ctx[:, l0:l0 + LANES]
            den = jnp.exp2(sink - m) + jnp.sum(e_lanes, axis=-1, keepdims=True)
            o = (_bdot(e_ctx.astype(BF16), vx_ref[:, ks])
                 + (_bdot(e_prev.astype(BF16), vp_ref[:, ks])
                    + _bdot(e_cur.astype(BF16), vc_ref[:, ks])
                    + _bdot(e_next.astype(BF16), vn_ref[:, ks]))) * (1.0 / den)
            for g in range(A_GROUP):
                hq = hk * A_GROUP + g
                o_ref[qs, hq * HEAD_DIM:(hq + 1) * HEAD_DIM] = o[g * blk:(g + 1) * blk, :].astype(o_ref.dtype)


def _attention(sink, qa, ka, va, ka_c, va_c, batch, seq, n_ctx):
    blk = A_BLOCK
    nb = seq // blk
    assert nb % Q_BLOCKS == 0
    steps = nb // Q_BLOCKS
    aw = qa.shape[1]
    kvw = ka.shape[1]
    kv = lambda j: pl.BlockSpec(
        (blk, kvw), lambda b, n: (b * nb + jnp.clip(n * Q_BLOCKS - 1 + j, 0, nb - 1), 0))
    kv_specs = [kv(j) for j in range(Q_BLOCKS + 2)]
    qo = pl.BlockSpec((Q_BLOCKS * blk, aw), lambda b, n: (b * steps + n, 0))
    ctx = pl.BlockSpec((n_ctx, kvw), lambda b, n: (b, 0))
    return pl.pallas_call(
        _attn_kernel,
        grid=(batch, steps),
        in_specs=[pl.BlockSpec(memory_space=pltpu.SMEM), qo] + kv_specs + kv_specs + [ctx, ctx],
        out_specs=qo,
        out_shape=jax.ShapeDtypeStruct((batch * seq, aw), BF16),
        compiler_params=_params(("parallel", "parallel")),
        name="window_gqa",
    )(sink, qa, *([ka] * (Q_BLOCKS + 2)), *([va] * (Q_BLOCKS + 2)), ka_c, va_c)


def _chunk_cumsum(g, *, reverse):
    rows = g.shape[0]
    p = lax.broadcasted_iota(jnp.int32, g.shape, 0) % GLA_CHUNK
    s = 1
    while s < GLA_CHUNK:
        if reverse:
            g = g + jnp.where(p < GLA_CHUNK - s, pltpu.roll(g, rows - s, 0), 0.0)
        else:
            g = g + jnp.where(p >= s, pltpu.roll(g, s, 0), 0.0)
        s *= 2
    return g


def _per_chunk_row(x, i):
    c = GLA_CHUNK
    return jnp.concatenate(
        [jnp.broadcast_to(x[j * c + i:j * c + i + 1, :], (c, x.shape[1])) for j in range(x.shape[0] // c)], axis=0)


def _gla_block(q, k, v, cum, state_t, mask, *, reverse, need_o):
    c = GLA_CHUNK
    rows, dk = k.shape
    n_chunks = rows // c
    k = k.astype(F32)
    i_last = 0 if reverse else c - 1
    i_mid = c // 2 if reverse else c // 2 - 1
    b_last = _per_chunk_row(cum, i_last)
    kdec = (k * jnp.exp(b_last - cum)).astype(BF16)
    if need_o:
        b_mid = _per_chunk_row(cum, i_mid)
        qc = q.astype(F32) * dk ** -0.5
        qm = (qc * jnp.exp(cum - b_mid)).astype(BF16)
        km = (k * jnp.exp(b_mid - cum)).astype(BF16)
        a = jnp.where(mask, _bdot_nt(qm, km), 0.0)
        o_intra = _bdot(a.astype(BF16), v)
        qe = (qc * jnp.exp(cum)).astype(BF16)
    o_inter = [None] * n_chunks
    for j in (reversed(range(n_chunks)) if reverse else range(n_chunks)):
        rs = slice(j * c, (j + 1) * c)
        if need_o:
            o_inter[j] = _bdot_nt(qe[rs, :], state_t.astype(BF16))
        kv_t = lax.dot_general(v[rs, :], kdec[rs, :], (((0,), (0,)), ((), ())), preferred_element_type=F32)
        state_t = state_t * jnp.exp(cum[j * c + i_last:j * c + i_last + 1, :]) + kv_t
    o = o_intra + jnp.concatenate(o_inter, axis=0) if need_o else None
    return o, state_t


def _gla_kernel(*refs, reverse, final):
    if final:
        (kx_ref, vx_ref, gx_ref, q_ref, k_ref, v_ref, g_ref, ob_ref, r_ref, nw_ref, o_ref, st_ref) = refs
    else:
        (kx_ref, vx_ref, gx_ref, q_ref, k_ref, v_ref, g_ref, o_ref, st_ref) = refs
    t = pl.program_id(1)
    n_heads = st_ref.shape[0]
    dv, dk = st_ref.shape[1:]

    @pl.when(t == 0)
    def _context():
        cum = _chunk_cumsum(gx_ref[...], reverse=reverse)
        for h in range(n_heads):
            ks, vs = slice(h * dk, (h + 1) * dk), slice(h * dv, (h + 1) * dv)
            _, st = _gla_block(None, kx_ref[:, ks], vx_ref[:, vs], cum[:, ks], jnp.zeros((dv, dk), F32), None,
                               reverse=reverse, need_o=False)
            st_ref[h] = st

    @pl.when(t > 0)
    def _latent():
        rows = k_ref.shape[0]
        cum = _chunk_cumsum(g_ref[...], reverse=reverse)
        r = lax.broadcasted_iota(jnp.int32, (rows, rows), 0)
        s = lax.broadcasted_iota(jnp.int32, (rows, rows), 1)
        causal = (s >= r) if reverse else (s <= r)
        mask = causal & ((r // GLA_CHUNK) == (s // GLA_CHUNK))
        for h in range(n_heads):
            ks, vs = slice(h * dk, (h + 1) * dk), slice(h * dv, (h + 1) * dv)
            o, st = _gla_block(q_ref[:, ks], k_ref[:, ks], v_ref[:, vs], cum[:, ks], st_ref[h], mask,
                               reverse=reverse, need_o=True)
            st_ref[h] = st
            if final:
                o = o + ob_ref[:, vs].astype(F32)
                o = o * lax.rsqrt(jnp.mean(o * o, axis=-1, keepdims=True) + LN_EPS)
                o = o * nw_ref[:, vs]
                o = o * _silu(r_ref[:, vs].astype(F32))
            o_ref[:, vs] = o.astype(o_ref.dtype)


def _gla(kb_c, vb_c, la_c, qb, kb, vb, la, batch, seq, n_ctx, *, reverse, extra=None):
    assert n_ctx % GLA_CHUNK == 0 and GLA_STEP % GLA_CHUNK == 0
    tb = GLA_STEP
    nt = seq // tb
    kw, vw = qb.shape[1], vb.shape[1]
    final = extra is not None

    def lat(b, t):
        i = jnp.maximum(t - 1, 0)
        if reverse:
            i = nt - 1 - i
        return (b * nt + i, 0)

    cx = lambda b, t: (b, 0)
    in_specs = [pl.BlockSpec((n_ctx, kw), cx), pl.BlockSpec((n_ctx, vw), cx), pl.BlockSpec((n_ctx, kw), cx),
                pl.BlockSpec((tb, kw), lat), pl.BlockSpec((tb, kw), lat),
                pl.BlockSpec((tb, vw), lat), pl.BlockSpec((tb, kw), lat)]
    args = [kb_c, vb_c, la_c, qb, kb, vb, la]
    if final:
        o_other, rb, norm_w = extra
        in_specs += [pl.BlockSpec((tb, vw), lat), pl.BlockSpec((tb, vw), lat), _resident((1, vw))]
        args += [o_other, rb, norm_w]
    return pl.pallas_call(
        functools.partial(_gla_kernel, reverse=reverse, final=final),
        grid=(batch, nt + 1),
        in_specs=in_specs,
        out_specs=pl.BlockSpec((tb, vw), lat),
        out_shape=jax.ShapeDtypeStruct((batch * seq, vw), BF16),
        scratch_shapes=[pltpu.VMEM((B_HEADS, vw // B_HEADS, kw // B_HEADS), F32)],
        compiler_params=_params(("parallel", "arbitrary")),
        name="gla_fwd_out" if final else "gla_bwd",
    )(*args)


def _outproj_kernel(oa_ref, ob_ref, x_ref, mod_ref, wt_ref, wb_ref, lng_ref, lnb_ref, wr_ref, br_ref,
                    x1_o, h2_o, rw_o, ri_o):
    for r0 in range(0, x_ref.shape[0], SUB_ROWS):
        _outproj_rows(slice(r0, r0 + SUB_ROWS), oa_ref, ob_ref, x_ref, mod_ref, wt_ref, wb_ref, lng_ref, lnb_ref,
                      wr_ref, br_ref, x1_o, h2_o, rw_o, ri_o)


def _outproj_rows(rs, oa_ref, ob_ref, x_ref, mod_ref, wt_ref, wb_ref, lng_ref, lnb_ref, wr_ref, br_ref,
                  x1_o, h2_o, rw_o, ri_o):
    y = _bdot(oa_ref[rs, :], wt_ref[...]) + _bdot(ob_ref[rs, :], wb_ref[...])
    x1 = _ln(ALPHA * x_ref[rs, :] + mod_ref[2:3, :] * y) * lng_ref[...] + lnb_ref[...]
    x1_o[rs, :] = x1
    h2 = _ln(x1) * (1.0 + mod_ref[4:5, :]) + mod_ref[3:4, :]
    h2_o[rs, :] = h2
    lg = _bdot(h2.astype(BF16), wr_ref[...]) + br_ref[...]
    lane = lax.broadcasted_iota(jnp.int32, lg.shape, 1)
    lanef = lane.astype(F32)
    big = float(LANES)
    is_g = (lane >= N_EXPERTS) & (lane < N_EXPERTS + N_GROUPS)
    gl = jnp.where(is_g, lg, -jnp.inf)
    gmax = jnp.max(gl, axis=-1, keepdims=True)
    pg_top = 1.0 / jnp.sum(jnp.exp(gl - gmax), axis=-1, keepdims=True)
    grp = jnp.min(jnp.where(gl == gmax, lanef, big), axis=-1, keepdims=True) - N_EXPERTS
    in_grp = (lane < N_EXPERTS) & ((lane // EXPERTS_PER_GROUP).astype(F32) == grp)
    el = jnp.where(in_grp, lg, -jnp.inf)
    m1 = jnp.max(el, axis=-1, keepdims=True)
    i1 = jnp.min(jnp.where(el == m1, lanef, big), axis=-1, keepdims=True)
    el2 = jnp.where(lanef == i1, -jnp.inf, el)
    m2 = jnp.max(el2, axis=-1, keepdims=True)
    i2 = jnp.min(jnp.where(el2 == m2, lanef, big), axis=-1, keepdims=True)
    e2 = jnp.exp(m2 - m1)
    w1 = pg_top / (1.0 + e2)
    w2 = pg_top * e2 / (1.0 + e2)
    rw_o[rs, :] = jnp.where(lane == 0, w1, jnp.where(lane == 1, w2, 0.0))
    ri_o[rs, :] = jnp.where(lane == 0, i1, jnp.where(lane == 1, i2, 0.0)).astype(jnp.int32)


def _outproj(out_a, out_b, xf, mod3, w_top, w_bot, ln_g, ln_b, wr, br, seq):
    r, d = xf.shape
    tm = OUT_TILE
    assert seq % tm == 0 and tm % SUB_ROWS == 0
    tiles_per_seq = seq // tm
    row = lambda n: pl.BlockSpec((tm, n), lambda i: (i, 0))
    return pl.pallas_call(
        _outproj_kernel,
        grid=(r // tm,),
        in_specs=[row(out_a.shape[1]), row(out_b.shape[1]), row(d),
                  pl.BlockSpec((None, 6, d), lambda i: (i // tiles_per_seq, 0, 0)),
                  _resident(w_top.shape), _resident(w_bot.shape),
                  _resident((1, d)), _resident((1, d)), _resident(wr.shape), _resident(br.shape)],
        out_specs=[row(d), row(d), row(LANES), row(LANES)],
        out_shape=[jax.ShapeDtypeStruct((r, d), F32), jax.ShapeDtypeStruct((r, d), F32),
                   jax.ShapeDtypeStruct((r, LANES), F32), jax.ShapeDtypeStruct((r, LANES), jnp.int32)],
        compiler_params=_params(("parallel",)),
        name="out_proj_router",
    )(out_a, out_b, xf, mod3, w_top, w_bot, ln_g.reshape(1, d), ln_b.reshape(1, d), wr, br)


def _row_copy(src_ref, src_row, dst_ref, dst_row, sem):
    return pltpu.make_async_copy(src_ref.at[pl.ds(src_row, 1)], dst_ref.at[pl.ds(dst_row, 1)], sem)


def _dispatch_kernel(dest_ref, tail_ref, empty_ref, h_ref, buf_ref, zero_ref, hbuf, sem, zsem, lsem, rsem):
    tm = hbuf.shape[1]
    step = pl.program_id(0)

    def zero_block(b):
        row = pl.multiple_of(b * MOE_BLOCK, MOE_BLOCK)
        return pltpu.make_async_copy(zero_ref, buf_ref.at[pl.ds(row, MOE_BLOCK)], zsem)

    def for_empty_blocks(fn):
        def body(b, carry):
            @pl.when(empty_ref[b] != 0)
            def _():
                fn(b)
            return carry
        lax.fori_loop(0, empty_ref.shape[0], body, 0)

    @pl.when(step == 0)
    def _zero_fill():
        zero_ref[...] = jnp.zeros(zero_ref.shape, zero_ref.dtype)
        for e in range(N_EXPERTS):
            start = pl.multiple_of(tail_ref[e] // SUBLANES * SUBLANES, SUBLANES)
            pltpu.make_async_copy(zero_ref, buf_ref.at[pl.ds(start, MOE_BLOCK)], sem).start()
        for e in range(N_EXPERTS):
            pltpu.make_async_copy(zero_ref, buf_ref.at[pl.ds(0, MOE_BLOCK)], sem).wait()
        for_empty_blocks(lambda b: zero_block(b).start())

    @pl.when(step == pl.num_programs(0) - 1)
    def _zero_done():
        for_empty_blocks(lambda b: zero_block(b).wait())

    n_slots = hbuf.shape[0]
    slot = step % n_slots

    def load(tile, sl):
        row = pl.multiple_of(tile * tm, tm)
        return pltpu.make_async_copy(h_ref.at[pl.ds(row, tm)], hbuf.at[sl], lsem.at[sl])

    @pl.when(step == 0)
    def _():
        load(0, 0).start()

    @pl.when(step + 1 < pl.num_programs(0))
    def _():
        load(step + 1, (step + 1) % n_slots).start()

    load(step, slot).wait()

    def issue(i, carry):
        for k in range(TOP_K):
            _row_copy(hbuf.at[slot], i, buf_ref, dest_ref[0, 0, TOP_K * i + k], rsem.at[slot]).start()
        return carry

    lax.fori_loop(0, tm, issue, 0, unroll=DMA_UNROLL)

    def wait_tile(sl):
        for _ in range(TOP_K):
            pltpu.make_async_copy(hbuf.at[sl], buf_ref.at[pl.ds(0, tm)], rsem.at[sl]).wait()

    pl.when(step > 0)(lambda: wait_tile((step + n_slots - 1) % n_slots))
    pl.when(step == pl.num_programs(0) - 1)(lambda: wait_tile(slot))


def _dispatch(dest, tail_row, empty_block, h2):
    r, d = h2.shape
    tm = DISPATCH_TILE
    assert r % tm == 0
    dest3 = dest.reshape(r // tm, 1, TOP_K * tm)
    n_rows = empty_block.shape[0] * MOE_BLOCK
    return pl.pallas_call(
        _dispatch_kernel,
        grid=(r // tm,),
        in_specs=[pl.BlockSpec((1, 1, TOP_K * tm), lambda i: (i, 0, 0), memory_space=pltpu.SMEM),
                  pl.BlockSpec(memory_space=pltpu.SMEM),
                  pl.BlockSpec(memory_space=pltpu.SMEM),
                  pl.BlockSpec(memory_space=pl.ANY)],
        out_specs=pl.BlockSpec(memory_space=pl.ANY),
        out_shape=jax.ShapeDtypeStruct((n_rows, d), h2.dtype),
        scratch_shapes=[pltpu.VMEM((MOE_BLOCK, d), h2.dtype), pltpu.VMEM((3, tm, d), h2.dtype),
                        pltpu.SemaphoreType.DMA, pltpu.SemaphoreType.DMA, pltpu.SemaphoreType.DMA((3,)),
                        pltpu.SemaphoreType.DMA((3,))],
        compiler_params=_params(("arbitrary",)),
        name="moe_dispatch",
    )(dest3, tail_row, empty_block, h2)


def _expert_kernel(bq_ref, lo_ref, hi_ref, el_ref, nq_ref, ub_ref, x_ref, wg_hbm, wu_hbm, wd_hbm, o_ref,
                   wgb, wub, wdb, sg, su, sd, sems):
    b = pl.program_id(0)
    q = bq_ref[b]
    slot = q % 2
    mats = ((wg_hbm, sg, wgb), (wu_hbm, su, wub), (wd_hbm, sd, wdb))
    n_pieces = W_PIECES * len(mats)

    def piece_copy(m, qt, t):
        hbm, stage, _ = mats[m]
        pr = stage.shape[1]
        return pltpu.make_async_copy(hbm.at[el_ref[qt], pl.ds(t * pr, pr), :], stage.at[t % 2],
                                     sems.at[m, t % 2])

    def process(qt, lo, hi):
        for p in range(n_pieces):
            m, t = p % len(mats), p // len(mats)

            @pl.when((lo <= p) & (p < hi))
            def _():
                _, stage, resident = mats[m]
                pr = stage.shape[1]
                piece_copy(m, qt, t).wait()
                resident[qt % 2, pl.ds(t * pr, pr), :] = stage[t % 2].astype(BF16)
                q2 = qt + (t + 2) // W_PIECES

                @pl.when(q2 < nq_ref[0])
                def _():
                    piece_copy(m, q2, (t + 2) % W_PIECES).start()

    @pl.when(b == 0)
    def _first_expert():
        for m in range(len(mats)):
            for t in range(2):
                piece_copy(m, 0, t).start()
        process(0, 0, n_pieces)

    @pl.when(b < ub_ref[0])
    def _compute():
        xb = x_ref[...].astype(BF16)
        hid = wgb.shape[2]
        acts = []
        for h0 in range(0, hid, HID_TILE):
            hs = pl.ds(h0, HID_TILE)
            acts.append((_silu(_bdot(xb, wgb[slot, :, hs])) * _bdot(xb, wub[slot, :, hs])).astype(BF16))
        o_ref[...] = _bdot(jnp.concatenate(acts, axis=1), wdb[slot])

    @pl.when(b >= ub_ref[0])
    def _unused():
        o_ref[...] = jnp.zeros(o_ref.shape, o_ref.dtype)

    process(q + 1, lo_ref[b], hi_ref[b])


def _experts(plan, buf, wg, wu, wd, n_out_rows):
    d = buf.shape[1]
    hid = wg.shape[2]
    assert d % W_PIECES == 0 and hid % W_PIECES == 0 and W_PIECES % 2 == 0
    blk = lambda f: pl.BlockSpec((MOE_BLOCK, d), f)
    return pl.pallas_call(
        _expert_kernel,
        grid_spec=pltpu.PrefetchScalarGridSpec(
            num_scalar_prefetch=len(plan),
            grid=(n_out_rows // MOE_BLOCK,),
            in_specs=[blk(lambda b, bq, lo, hi, el, nq, ub: (jnp.minimum(b, ub[0] - 1), 0)),
                      pl.BlockSpec(memory_space=pl.ANY), pl.BlockSpec(memory_space=pl.ANY),
                      pl.BlockSpec(memory_space=pl.ANY)],
            out_specs=blk(lambda b, bq, lo, hi, el, nq, ub: (b, 0)),
            scratch_shapes=[pltpu.VMEM((2, d, hid), BF16), pltpu.VMEM((2, d, hid), BF16),
                            pltpu.VMEM((2, hid, d), BF16),
                            pltpu.VMEM((2, d // W_PIECES, hid), F32), pltpu.VMEM((2, d // W_PIECES, hid), F32),
                            pltpu.VMEM((2, hid // W_PIECES, d), F32),
                            pltpu.SemaphoreType.DMA((3, 2))],
        ),
        out_shape=jax.ShapeDtypeStruct((n_out_rows, d), F32),
        compiler_params=_params(("arbitrary",)),
        name="moe_experts",
    )(*plan, buf, wg, wu, wd)


def _combine_kernel(dest_ref, dnext_ref, y_ref, rw_ref, x1_ref, mod_ref, lng_ref, lnb_ref, o_ref,
                    g_even, g_odd, sem):
    tm = x1_ref.shape[0]
    i = pl.program_id(0)
    n_groups = tm // DMA_UNROLL
    has_next = i + 1 < pl.num_programs(0)

    def gather(d_ref, g_ref, sl):
        def body(j, carry):
            for r in range(DMA_UNROLL):
                row = j * DMA_UNROLL + r
                for k in range(TOP_K):
                    _row_copy(y_ref, d_ref[0, 0, TOP_K * row + k], g_ref.at[k], row, sem.at[sl]).start()
            return carry
        lax.fori_loop(0, n_groups, body, 0)

    @pl.when(i == 0)
    def _():
        gather(dest_ref, g_even, 0)

    def run(sl, g_cur, g_next, prefetch):
        if prefetch:
            gather(dnext_ref, g_next, 1 - sl)
        for k in range(TOP_K):
            pltpu.make_async_copy(y_ref.at[pl.ds(0, tm)], g_cur.at[k], sem.at[sl]).wait()
        f = rw_ref[:, 0:1] * g_cur[0] + rw_ref[:, 1:2] * g_cur[1]
        o_ref[...] = _ln(ALPHA * x1_ref[...] + mod_ref[5:6, :] * f) * lng_ref[...] + lnb_ref[...]

    for sl, g_cur, g_next in ((0, g_even, g_odd), (1, g_odd, g_even)):
        for prefetch in (True, False):
            pl.when((i % 2 == sl) & (has_next == prefetch))(functools.partial(run, sl, g_cur, g_next, prefetch))


def _combine(dest, y, rw, x1, mod3, ln_g, ln_b, seq):
    r, d = x1.shape
    tm = COMBINE_TILE
    assert seq % tm == 0
    tiles_per_seq = seq // tm
    dest3 = dest.reshape(r // tm, 1, TOP_K * tm)
    n = r // tm
    row = lambda w: pl.BlockSpec((tm, w), lambda i: (i, 0))
    return pl.pallas_call(
        _combine_kernel,
        grid=(n,),
        in_specs=[pl.BlockSpec((1, 1, TOP_K * tm), lambda i: (i, 0, 0), memory_space=pltpu.SMEM),
                  pl.BlockSpec((1, 1, TOP_K * tm), lambda i: (jnp.minimum(i + 1, n - 1), 0, 0),
                               memory_space=pltpu.SMEM),
                  pl.BlockSpec(memory_space=pl.ANY),
                  row(LANES), row(d),
                  pl.BlockSpec((None, 6, d), lambda i: (i // tiles_per_seq, 0, 0)),
                  _resident((1, d)), _resident((1, d))],
        out_specs=row(d),
        out_shape=jax.ShapeDtypeStruct((r, d), F32),
        scratch_shapes=[pltpu.VMEM((TOP_K, tm, d), F32), pltpu.VMEM((TOP_K, tm, d), F32),
                        pltpu.SemaphoreType.DMA((2,))],
        compiler_params=_params(("arbitrary",)),
        name="moe_combine_ln",
    )(dest3, dest3, y, rw, x1, mod3, ln_g.reshape(1, d), ln_b.reshape(1, d))


def _routing_plan(ri, n_tokens):
    eid = ri[:, :TOP_K].reshape(-1)
    m = n_tokens * TOP_K
    onehot = (eid[:, None] == jnp.arange(N_EXPERTS, dtype=jnp.int32)[None, :]).astype(jnp.int32)
    csum = jnp.cumsum(onehot, axis=0)
    counts = csum[-1]
    rank = jnp.sum(csum * onehot, axis=1) - 1
    padded = (counts + MOE_BLOCK - 1) // MOE_BLOCK * MOE_BLOCK
    pad_end = jnp.cumsum(padded)
    pad_start = pad_end - padded
    dest = jnp.sum(onehot * pad_start[None, :], axis=1) + rank
    tail = pad_start + counts
    n_blocks = m // MOE_BLOCK + N_EXPERTS
    used_blocks = pad_end[-1] // MOE_BLOCK
    empty = jnp.arange(n_blocks + 1, dtype=jnp.int32) >= used_blocks

    n_pieces = 3 * W_PIECES
    has = counts > 0
    ordinal = jnp.cumsum(has.astype(jnp.int32)) - 1
    n_ord = ordinal[-1] + 1
    e_ids = jnp.arange(N_EXPERTS, dtype=jnp.int32)
    ord_expert = jnp.sum(jnp.where(has[None, :] & (ordinal[None, :] == e_ids[:, None]), e_ids[None, :], 0), axis=1)
    bid = jnp.arange(n_blocks, dtype=jnp.int32)
    brow = bid[:, None] * MOE_BLOCK
    in_e = ((pad_start[None, :] <= brow) & (brow < pad_end[None, :])).astype(jnp.int32)
    pick = lambda v: jnp.sum(in_e * v[None, :], axis=1)
    used = bid < used_blocks
    blk_q = jnp.where(used, pick(ordinal), n_ord - 1)
    i_in_e = bid - pick(pad_start) // MOE_BLOCK
    k_e = jnp.maximum(pick(padded) // MOE_BLOCK, 1)
    brings = used & (blk_q + 1 < n_ord)
    lo = jnp.where(brings, n_pieces * i_in_e // k_e, 0)
    hi = jnp.where(brings, n_pieces * (i_in_e + 1) // k_e, 0)
    i32 = lambda a: a.astype(jnp.int32)
    plan = (i32(blk_q), i32(lo), i32(hi), i32(ord_expert), i32(n_ord.reshape(1)), i32(used_blocks.reshape(1)))
    return i32(dest), i32(tail), i32(empty), plan, n_blocks * MOE_BLOCK


def _rope_tables(seq):
    n_freq = HEAD_DIM // 4
    inv_freq = ROPE_THETA ** (-jnp.arange(n_freq, dtype=F32) / n_freq)
    rows = seq // GRID_W
    ar = jnp.arange(rows, dtype=F32)[:, None] * inv_freq
    ac = jnp.arange(GRID_W, dtype=F32)[:, None] * inv_freq
    by_row = lambda t: jnp.broadcast_to(t[:, None, :], (rows, GRID_W, n_freq)).reshape(seq, n_freq)
    by_col = lambda t: jnp.broadcast_to(t[None, :, :], (rows, GRID_W, n_freq)).reshape(seq, n_freq)
    cos_r, sin_r, cos_c, sin_c = by_row(jnp.cos(ar)), by_row(jnp.sin(ar)), by_col(jnp.cos(ac)), by_col(jnp.sin(ac))
    zero = jnp.zeros_like(cos_r)
    cos_t = jnp.concatenate([cos_r, cos_r, cos_c, cos_c], axis=1)
    sin_a = jnp.concatenate([-sin_r, zero, -sin_c, zero], axis=1)
    sin_b = jnp.concatenate([zero, sin_r, zero, sin_c], axis=1)
    return cos_t, sin_a, sin_b


def kernel(x, c, ctx, c_ctx, w_ada, b_ada, w_in, w_gate_up, b_gate, attn_sink, gla_norm_w, w_out, ln1_g, ln1_b, w_router_group, b_router_group, w_router_expert, b_router_expert, w_exp_gate, w_exp_up, w_exp_down, ln2_g, ln2_b):
    batch, seq, d = x.shape
    n_ctx = ctx.shape[1]
    assert w_ada.shape[0] == DEPTH and batch < MOD_ROWS
    assert seq % GLA_STEP == 0 and n_ctx == GLA_STEP
    n_tok = batch * seq
    a_width = d // 2
    kv_width = a_width // A_GROUP
    b_width = d - a_width
    key_width = b_width // 2
    layer = 0

    cc = jnp.concatenate([c, c_ctx[None, :], jnp.zeros((MOD_ROWS - batch - 1, d), F32)], axis=0)
    mod3 = _adaln(cc, w_ada[layer], b_ada[layer]).reshape(MOD_ROWS, 6, d)

    splits = (a_width, kv_width, kv_width, key_width, key_width, b_width, b_width, 2 * GATE_RANK)
    w_in_b = w_in[layer].astype(BF16)
    zero_up = jnp.zeros((GATE_RANK, key_width), F32)
    wup2 = jnp.concatenate([jnp.concatenate([w_gate_up[layer, 0], zero_up], axis=1),
                            jnp.concatenate([zero_up, w_gate_up[layer, 1]], axis=1)], axis=0).astype(BF16)
    bg2 = b_gate[layer].reshape(1, 2 * key_width)
    tables = _rope_tables(seq)

    xf = x.reshape(n_tok, d)
    qa, ka, va, qb, kb, vb, rb, la_f, la_b = _project(
        xf, mod3, lambda i, per_seq: i // per_seq, tables, w_in_b, splits, wup2, bg2, rope=True, seq=seq)
    _, ka_c, va_c, _, kb_c, vb_c, _, lac_f, lac_b = _project(
        ctx.reshape(batch * n_ctx, d), mod3, lambda i, per_seq: batch, tables, w_in_b, splits, wup2, bg2,
        rope=False, seq=seq, keys_values_only=True)

    out_a = _attention(attn_sink[layer], qa, ka, va, ka_c, va_c, batch, seq, n_ctx)
    o_b = _gla(kb_c, vb_c, lac_b, qb, kb, vb, la_b, batch, seq, n_ctx, reverse=True)
    out_b = _gla(kb_c, vb_c, lac_f, qb, kb, vb, la_f, batch, seq, n_ctx, reverse=False,
                 extra=(o_b, rb, gla_norm_w[layer].reshape(1, b_width)))

    w_out_b = w_out[layer].astype(BF16)
    wr = jnp.concatenate([w_router_expert[layer], w_router_group[layer],
                          jnp.zeros((d, LANES - N_EXPERTS - N_GROUPS), F32)], axis=1).astype(BF16)
    br = jnp.concatenate([b_router_expert[layer], b_router_group[layer],
                          jnp.zeros((LANES - N_EXPERTS - N_GROUPS,), F32)]).reshape(1, LANES)
    x1, h2, rw, ri = _outproj(out_a, out_b, xf, mod3, w_out_b[:a_width], w_out_b[a_width:],
                              ln1_g[layer], ln1_b[layer], wr, br, seq)

    dest, tail_row, empty_block, plan, n_buf_rows = _routing_plan(ri, n_tok)
    buf = _dispatch(dest, tail_row, empty_block, h2)
    y = _experts(plan, buf, w_exp_gate[layer], w_exp_up[layer], w_exp_down[layer], n_buf_rows)
    out = _combine(dest, y, rw, x1, mod3, ln2_g[layer], ln2_b[layer], seq)
    return out.reshape(batch, seq, d)
```

```python
import functools

import jax
import jax.numpy as jnp
from jax import lax
from jax.experimental import pallas as pl
from jax.experimental.pallas import tpu as pltpu

F32 = jnp.float32
BF16 = jnp.bfloat16

HEAD_DIM = 128
GRID_W = 64
WINDOW = 128
A_BLOCK = 128
A_GROUP = 4
ROPE_THETA = 10000.0
B_HEADS = 4
GATE_RANK = 16
GATE_TAU = 16.0
GLA_CHUNK = 64
N_GROUPS = 4
EXPERTS_PER_GROUP = 8
N_EXPERTS = N_GROUPS * EXPERTS_PER_GROUP
TOP_K = 2
DEPTH = 1
ALPHA = (2.0 * DEPTH) ** 0.25
LN_EPS = 1e-6
LOG2_E = 1.4426950408889634
ATTN_EXP2_SCALE = HEAD_DIM ** -0.5 * LOG2_E

LANES = 128
SUBLANES = 8
MOD_ROWS = 8
VMEM_LIMIT = 56 * 1024 * 1024

DISPATCH_TILE = 1024
COMBINE_TILE = 256
OUT_TILE = 512
SUB_ROWS = 256
PROJ_COLS = 1024
Q_BLOCKS = 4
GLA_STEP = 256
MOE_BLOCK = 256
HID_TILE = 512
BLOCK_ROW_MODES = (128, 256)
W_PIECES = 4
DMA_UNROLL = 8


def _params(sem):
    return pltpu.CompilerParams(dimension_semantics=sem, vmem_limit_bytes=VMEM_LIMIT)


def _resident(shape):
    nd = len(shape)
    return pl.BlockSpec(shape, lambda *_: (0,) * nd, pipeline_mode=pl.Buffered(1))


def _ln(x):
    mu = jnp.mean(x, axis=-1, keepdims=True)
    xc = x - mu
    var = jnp.mean(xc * xc, axis=-1, keepdims=True)
    return xc * lax.rsqrt(var + LN_EPS)


def _silu(x):
    return x * jax.nn.sigmoid(x)


def _bdot(a, b):
    return jnp.dot(a, b, preferred_element_type=F32)


def _bdot_nt(a, b):
    return lax.dot_general(a, b, (((1,), (1,)), ((), ())), preferred_element_type=F32)


def _adaln_kernel(c_ref, w_ref, b_ref, o_ref):
    s = _silu(c_ref[...]).astype(BF16)
    o_ref[...] = _bdot(s, w_ref[...].astype(BF16)) + b_ref[...]


def _adaln(cc, w_ada, b_ada):
    d, n = w_ada.shape
    tn = 1024
    return pl.pallas_call(
        _adaln_kernel,
        grid=(n // tn,),
        in_specs=[pl.BlockSpec((MOD_ROWS, d), lambda j: (0, 0)),
                  pl.BlockSpec((d, tn), lambda j: (0, j)),
                  pl.BlockSpec((1, tn), lambda j: (0, j))],
        out_specs=pl.BlockSpec((MOD_ROWS, tn), lambda j: (0, j)),
        out_shape=jax.ShapeDtypeStruct((MOD_ROWS, n), F32),
        compiler_params=_params(("arbitrary",)),
        name="adaln",
    )(cc, w_ada, b_ada.reshape(1, n))


def _proj_kernel(x_ref, mod_ref, cos_ref, sina_ref, sinb_ref, w_ref, wup, bg, *out_refs, groups, rope):
    for r0 in range(0, x_ref.shape[0], SUB_ROWS):
        _proj_rows(slice(r0, r0 + SUB_ROWS), x_ref, mod_ref, cos_ref, sina_ref, sinb_ref, w_ref, wup, bg,
                   out_refs, groups, rope)


def _proj_rows(rs, x_ref, mod_ref, cos_ref, sina_ref, sinb_ref, w_ref, wup, bg, out_refs, groups, rope):
    h = _ln(x_ref[rs, :]) * (1.0 + mod_ref[1:2, :]) + mod_ref[0:1, :]
    hb = h.astype(BF16)

    def rot(t):
        return (t * cos_ref[rs, :] + pltpu.roll(t, 96, 1) * sina_ref[rs, :]
                + pltpu.roll(t, 32, 1) * sinb_ref[rs, :])

    col = 0
    refs = iter(out_refs)
    for n, wanted, rotary, post in groups:
        if wanted:
            o_ref = next(refs)
            step = min(n, PROJ_COLS)
            for c0 in range(0, n, step):
                t = _bdot(hb, w_ref[:, col + c0:col + c0 + step])
                if rotary and rope:
                    for l0 in range(0, step, LANES):
                        r = rot(t[:, l0:l0 + LANES])
                        o_ref[rs, c0 + l0:c0 + l0 + LANES] = (r if post == 1.0 else r * post).astype(o_ref.dtype)
                else:
                    o_ref[rs, c0:c0 + step] = t.astype(o_ref.dtype)
        col += n
    laf_o, lab_o = refs
    gl = _bdot(hb, w_ref[:, col:]).astype(BF16)
    z = _bdot(gl, wup[...]) + bg[...]
    la = (jnp.minimum(z, 0.0) - jnp.log1p(jnp.exp(-jnp.abs(z)))) / GATE_TAU
    kw = laf_o.shape[1]
    laf_o[rs, :] = la[:, :kw]
    lab_o[rs, :] = la[:, kw:]


def _project(xf, mod3, mod_row, tables, w_in_b, splits, wup2, bg2, *, rope, seq, keys_values_only=False):
    r, d = xf.shape
    tm = OUT_TILE
    assert r % tm == 0 and seq % tm == 0 and tm % SUB_ROWS == 0
    tiles_per_seq = seq // tm
    mod_row_of_tile = lambda i: mod_row(i, tiles_per_seq)
    n_gate = splits[-1]
    assert sum(splits) == w_in_b.shape[1] and (sum(splits) - n_gate) % LANES == 0
    kv = keys_values_only
    spec = ((BF16, not kv, True, ATTN_EXP2_SCALE), (BF16, True, True, 1.0), (BF16, True, False, 1.0),
            (BF16, not kv, False, 1.0), (BF16, True, False, 1.0), (BF16, True, False, 1.0),
            (BF16, not kv, False, 1.0))
    groups = tuple((n, wanted, rotary, post) for n, (_, wanted, rotary, post) in zip(splits, spec))
    key_width = splits[3]
    outs = [(n, dt) for n, (dt, wanted, _, _) in zip(splits, spec) if wanted] + [(key_width, F32)] * 2
    cos_t, sina_t, sinb_t = tables
    row = lambda n: pl.BlockSpec((tm, n), lambda i: (i, 0))
    tab = pl.BlockSpec((tm, LANES), lambda i: (i % tiles_per_seq, 0))
    res = pl.pallas_call(
        functools.partial(_proj_kernel, groups=groups, rope=rope),
        grid=(r // tm,),
        in_specs=[row(d),
                  pl.BlockSpec((None, 6, d), lambda i: (mod_row_of_tile(i), 0, 0)),
                  tab, tab, tab,
                  _resident(w_in_b.shape), _resident(wup2.shape), _resident(bg2.shape)],
        out_specs=[row(n) for n, _ in outs],
        out_shape=[jax.ShapeDtypeStruct((r, n), dt) for n, dt in outs],
        compiler_params=_params(("parallel",)),
        name="in_proj_rope" if rope else "in_proj_ctx",
    )(xf, mod3, cos_t, sina_t, sinb_t, w_in_b, wup2, bg2)
    res = iter(res)
    return tuple(next(res) if wanted else None for _, wanted, _, _ in spec) + tuple(res)


def _attn_kernel(sink_ref, q_ref, *refs):
    n_kv = Q_BLOCKS + 2
    k_refs, v_refs = refs[:n_kv], refs[n_kv:2 * n_kv]
    kx_ref, vx_ref, o_ref = refs[2 * n_kv:]
    n = pl.program_id(1)
    last = pl.num_programs(1) - 1
    blk = A_BLOCK
    rows = A_GROUP * blk
    n_ctx = kx_ref.shape[0]
    qi = lax.broadcasted_iota(jnp.int32, (rows, blk), 0) % blk
    kj = lax.broadcasted_iota(jnp.int32, (rows, blk), 1)
    for u in range(Q_BLOCKS):
        ok_prev = (kj >= qi) & (n > 0) if u == 0 else (kj >= qi)
        ok_next = (kj <= qi) & (n < last) if u == Q_BLOCKS - 1 else (kj <= qi)
        qs = slice(u * blk, (u + 1) * blk)
        kp_ref, kc_ref, kn_ref = k_refs[u:u + 3]
        vp_ref, vc_ref, vn_ref = v_refs[u:u + 3]
        for hk in range(kc_ref.shape[1] // HEAD_DIM):
            ks = slice(hk * HEAD_DIM, (hk + 1) * HEAD_DIM)
            q4 = jnp.concatenate(
                [q_ref[qs, (hk * A_GROUP + g) * HEAD_DIM:(hk * A_GROUP + g + 1) * HEAD_DIM]
                 for g in range(A_GROUP)], axis=0)
            s_ctx = _bdot_nt(q4, kx_ref[:, ks])
            s_prev = jnp.where(ok_prev, _bdot_nt(q4, kp_ref[:, ks]), -jnp.inf)
            s_cur = _bdot_nt(q4, kc_ref[:, ks])
            s_next = jnp.where(ok_next, _bdot_nt(q4, kn_ref[:, ks]), -jnp.inf)
            sink = jnp.concatenate(
                [jnp.full((blk, 1), sink_ref[hk * A_GROUP + g] * LOG2_E, F32) for g in range(A_GROUP)], axis=0)
            m_lanes = jnp.maximum(jnp.maximum(s_prev, s_cur), s_next)
            for l0 in range(0, n_ctx, LANES):
                m_lanes = jnp.maximum(m_lanes, s_ctx[:, l0:l0 + LANES])
            m = jnp.maximum(jnp.max(m_lanes, axis=-1, keepdims=True), sink)
            e_ctx = jnp.exp2(s_ctx - m)
            e_prev = jnp.exp2(s_prev - m)
            e_cur = jnp.exp2(s_cur - m)
            e_next = jnp.exp2(s_next - m)
            e_lanes = e_prev + e_cur + e_next
            for l0 in range(0, n_ctx, LANES):
                e_lanes = e_lanes + e_ctx[:, l0:l0 + LANES]
            den = jnp.exp2(sink - m) + jnp.sum(e_lanes, axis=-1, keepdims=True)
            o = (_bdot(e_ctx.astype(BF16), vx_ref[:, ks])
                 + (_bdot(e_prev.astype(BF16), vp_ref[:, ks])
                    + _bdot(e_cur.astype(BF16), vc_ref[:, ks])
                    + _bdot(e_next.astype(BF16), vn_ref[:, ks]))) * (1.0 / den)
            for g in range(A_GROUP):
                hq = hk * A_GROUP + g
                o_ref[qs, hq * HEAD_DIM:(hq + 1) * HEAD_DIM] = o[g * blk:(g + 1) * blk, :].astype(o_ref.dtype)


def _attention(sink, qa, ka, va, ka_c, va_c, batch, seq, n_ctx):
    blk = A_BLOCK
    nb = seq // blk
    assert nb % Q_BLOCKS == 0
    steps = nb // Q_BLOCKS
    aw = qa.shape[1]
    kvw = ka.shape[1]
    kv = lambda j: pl.BlockSpec(
        (blk, kvw), lambda b, n: (b * nb + jnp.clip(n * Q_BLOCKS - 1 + j, 0, nb - 1), 0))
    kv_specs = [kv(j) for j in range(Q_BLOCKS + 2)]
    qo = pl.BlockSpec((Q_BLOCKS * blk, aw), lambda b, n: (b * steps + n, 0))
    ctx = pl.BlockSpec((n_ctx, kvw), lambda b, n: (b, 0))
    return pl.pallas_call(
        _attn_kernel,
        grid=(batch, steps),
        in_specs=[pl.BlockSpec(memory_space=pltpu.SMEM), qo] + kv_specs + kv_specs + [ctx, ctx],
        out_specs=qo,
        out_shape=jax.ShapeDtypeStruct((batch * seq, aw), BF16),
        compiler_params=_params(("parallel", "parallel")),
        name="window_gqa",
    )(sink, qa, *([ka] * (Q_BLOCKS + 2)), *([va] * (Q_BLOCKS + 2)), ka_c, va_c)


def _chunk_cumsum(g, *, reverse):
    rows = g.shape[0]
    p = lax.broadcasted_iota(jnp.int32, g.shape, 0) % GLA_CHUNK
    s = 1
    while s < GLA_CHUNK:
        if reverse:
            g = g + jnp.where(p < GLA_CHUNK - s, pltpu.roll(g, rows - s, 0), 0.0)
        else:
            g = g + jnp.where(p >= s, pltpu.roll(g, s, 0), 0.0)
        s *= 2
    return g


def _per_chunk_row(x, i):
    c = GLA_CHUNK
    return jnp.concatenate(
        [jnp.broadcast_to(x[j * c + i:j * c + i + 1, :], (c, x.shape[1])) for j in range(x.shape[0] // c)], axis=0)


def _gla_block(q, k, v, cum, state_t, mask, *, reverse, need_o):
    c = GLA_CHUNK
    rows, dk = k.shape
    n_chunks = rows // c
    k = k.astype(F32)
    i_last = 0 if reverse else c - 1
    i_mid = c // 2 if reverse else c // 2 - 1
    b_last = _per_chunk_row(cum, i_last)
    kdec = (k * jnp.exp(b_last - cum)).astype(BF16)
    if need_o:
        b_mid = _per_chunk_row(cum, i_mid)
        qc = q.astype(F32) * dk ** -0.5
        qm = (qc * jnp.exp(cum - b_mid)).astype(BF16)
        km = (k * jnp.exp(b_mid - cum)).astype(BF16)
        a = jnp.where(mask, _bdot_nt(qm, km), 0.0)
        o_intra = _bdot(a.astype(BF16), v)
        qe = (qc * jnp.exp(cum)).astype(BF16)
    o_inter = [None] * n_chunks
    for j in (reversed(range(n_chunks)) if reverse else range(n_chunks)):
        rs = slice(j * c, (j + 1) * c)
        if need_o:
            o_inter[j] = _bdot_nt(qe[rs, :], state_t.astype(BF16))
        kv_t = lax.dot_general(v[rs, :], kdec[rs, :], (((0,), (0,)), ((), ())), preferred_element_type=F32)
        state_t = state_t * jnp.exp(cum[j * c + i_last:j * c + i_last + 1, :]) + kv_t
    o = o_intra + jnp.concatenate(o_inter, axis=0) if need_o else None
    return o, state_t


def _gla_kernel(*refs, reverse, final):
    if final:
        (kx_ref, vx_ref, gx_ref, q_ref, k_ref, v_ref, g_ref, ob_ref, r_ref, nw_ref, o_ref, st_ref) = refs
    else:
        (kx_ref, vx_ref, gx_ref, q_ref, k_ref, v_ref, g_ref, o_ref, st_ref) = refs
    t = pl.program_id(1)
    n_heads = st_ref.shape[0]
    dv, dk = st_ref.shape[1:]

    @pl.when(t == 0)
    def _context():
        cum = _chunk_cumsum(gx_ref[...], reverse=reverse)
        for h in range(n_heads):
            ks, vs = slice(h * dk, (h + 1) * dk), slice(h * dv, (h + 1) * dv)
            _, st = _gla_block(None, kx_ref[:, ks], vx_ref[:, vs], cum[:, ks], jnp.zeros((dv, dk), F32), None,
                               reverse=reverse, need_o=False)
            st_ref[h] = st

    @pl.when(t > 0)
    def _latent():
        rows = k_ref.shape[0]
        cum = _chunk_cumsum(g_ref[...], reverse=reverse)
        r = lax.broadcasted_iota(jnp.int32, (rows, rows), 0)
        s = lax.broadcasted_iota(jnp.int32, (rows, rows), 1)
        causal = (s >= r) if reverse else (s <= r)
        mask = causal & ((r // GLA_CHUNK) == (s // GLA_CHUNK))
        for h in range(n_heads):
            ks, vs = slice(h * dk, (h + 1) * dk), slice(h * dv, (h + 1) * dv)
            o, st = _gla_block(q_ref[:, ks], k_ref[:, ks], v_ref[:, vs], cum[:, ks], st_ref[h], mask,
                               reverse=reverse, need_o=True)
            st_ref[h] = st
            if final:
                o = o + ob_ref[:, vs].astype(F32)
                o = o * lax.rsqrt(jnp.mean(o * o, axis=-1, keepdims=True) + LN_EPS)
                o = o * nw_ref[:, vs]
                o = o * _silu(r_ref[:, vs].astype(F32))
            o_ref[:, vs] = o.astype(o_ref.dtype)


def _gla(kb_c, vb_c, la_c, qb, kb, vb, la, batch, seq, n_ctx, *, reverse, extra=None):
    assert n_ctx % GLA_CHUNK == 0 and GLA_STEP % GLA_CHUNK == 0
    tb = GLA_STEP
    nt = seq // tb
    kw, vw = qb.shape[1], vb.shape[1]
    final = extra is not None

    def lat(b, t):
        i = jnp.maximum(t - 1, 0)
        if reverse:
            i = nt - 1 - i
        return (b * nt + i, 0)

    cx = lambda b, t: (b, 0)
    in_specs = [pl.BlockSpec((n_ctx, kw), cx), pl.BlockSpec((n_ctx, vw), cx), pl.BlockSpec((n_ctx, kw), cx),
                pl.BlockSpec((tb, kw), lat), pl.BlockSpec((tb, kw), lat),
                pl.BlockSpec((tb, vw), lat), pl.BlockSpec((tb, kw), lat)]
    args = [kb_c, vb_c, la_c, qb, kb, vb, la]
    if final:
        o_other, rb, norm_w = extra
        in_specs += [pl.BlockSpec((tb, vw), lat), pl.BlockSpec((tb, vw), lat), _resident((1, vw))]
        args += [o_other, rb, norm_w]
    return pl.pallas_call(
        functools.partial(_gla_kernel, reverse=reverse, final=final),
        grid=(batch, nt + 1),
        in_specs=in_specs,
        out_specs=pl.BlockSpec((tb, vw), lat),
        out_shape=jax.ShapeDtypeStruct((batch * seq, vw), BF16),
        scratch_shapes=[pltpu.VMEM((B_HEADS, vw // B_HEADS, kw // B_HEADS), F32)],
        compiler_params=_params(("parallel", "arbitrary")),
        name="gla_fwd_out" if final else "gla_bwd",
    )(*args)


def _outproj_kernel(oa_ref, ob_ref, x_ref, mod_ref, wt_ref, wb_ref, lng_ref, lnb_ref, wr_ref, br_ref,
                    x1_o, h2_o, rw_o, ri_o):
    for r0 in range(0, x_ref.shape[0], SUB_ROWS):
        _outproj_rows(slice(r0, r0 + SUB_ROWS), oa_ref, ob_ref, x_ref, mod_ref, wt_ref, wb_ref, lng_ref, lnb_ref,
                      wr_ref, br_ref, x1_o, h2_o, rw_o, ri_o)


def _outproj_rows(rs, oa_ref, ob_ref, x_ref, mod_ref, wt_ref, wb_ref, lng_ref, lnb_ref, wr_ref, br_ref,
                  x1_o, h2_o, rw_o, ri_o):
    y = _bdot(oa_ref[rs, :], wt_ref[...]) + _bdot(ob_ref[rs, :], wb_ref[...])
    x1 = _ln(ALPHA * x_ref[rs, :] + mod_ref[2:3, :] * y) * lng_ref[...] + lnb_ref[...]
    x1_o[rs, :] = x1
    h2 = _ln(x1) * (1.0 + mod_ref[4:5, :]) + mod_ref[3:4, :]
    h2_o[rs, :] = h2
    lg = _bdot(h2.astype(BF16), wr_ref[...]) + br_ref[...]
    lane = lax.broadcasted_iota(jnp.int32, lg.shape, 1)
    lanef = lane.astype(F32)
    big = float(LANES)
    is_g = (lane >= N_EXPERTS) & (lane < N_EXPERTS + N_GROUPS)
    gl = jnp.where(is_g, lg, -jnp.inf)
    gmax = jnp.max(gl, axis=-1, keepdims=True)
    pg_top = 1.0 / jnp.sum(jnp.exp(gl - gmax), axis=-1, keepdims=True)
    grp = jnp.min(jnp.where(gl == gmax, lanef, big), axis=-1, keepdims=True) - N_EXPERTS
    in_grp = (lane < N_EXPERTS) & ((lane // EXPERTS_PER_GROUP).astype(F32) == grp)
    el = jnp.where(in_grp, lg, -jnp.inf)
    m1 = jnp.max(el, axis=-1, keepdims=True)
    i1 = jnp.min(jnp.where(el == m1, lanef, big), axis=-1, keepdims=True)
    el2 = jnp.where(lanef == i1, -jnp.inf, el)
    m2 = jnp.max(el2, axis=-1, keepdims=True)
    i2 = jnp.min(jnp.where(el2 == m2, lanef, big), axis=-1, keepdims=True)
    e2 = jnp.exp(m2 - m1)
    w1 = pg_top / (1.0 + e2)
    w2 = pg_top * e2 / (1.0 + e2)
    rw_o[rs, :] = jnp.where(lane == 0, w1, jnp.where(lane == 1, w2, 0.0))
    ri_o[rs, :] = jnp.where(lane == 0, i1, jnp.where(lane == 1, i2, 0.0)).astype(jnp.int32)


def _outproj(out_a, out_b, xf, mod3, w_top, w_bot, ln_g, ln_b, wr, br, seq):
    r, d = xf.shape
    tm = OUT_TILE
    assert seq % tm == 0 and tm % SUB_ROWS == 0
    tiles_per_seq = seq // tm
    row = lambda n: pl.BlockSpec((tm, n), lambda i: (i, 0))
    return pl.pallas_call(
        _outproj_kernel,
        grid=(r // tm,),
        in_specs=[row(out_a.shape[1]), row(out_b.shape[1]), row(d),
                  pl.BlockSpec((None, 6, d), lambda i: (i // tiles_per_seq, 0, 0)),
                  _resident(w_top.shape), _resident(w_bot.shape),
                  _resident((1, d)), _resident((1, d)), _resident(wr.shape), _resident(br.shape)],
        out_specs=[row(d), row(d), row(LANES), row(LANES)],
        out_shape=[jax.ShapeDtypeStruct((r, d), F32), jax.ShapeDtypeStruct((r, d), F32),
                   jax.ShapeDtypeStruct((r, LANES), F32), jax.ShapeDtypeStruct((r, LANES), jnp.int32)],
        compiler_params=_params(("parallel",)),
        name="out_proj_router",
    )(out_a, out_b, xf, mod3, w_top, w_bot, ln_g.reshape(1, d), ln_b.reshape(1, d), wr, br)


def _row_copy(src_ref, src_row, dst_ref, dst_row, sem):
    return pltpu.make_async_copy(src_ref.at[pl.ds(src_row, 1)], dst_ref.at[pl.ds(dst_row, 1)], sem)


def _dispatch_kernel(dest_ref, tail_ref, empty_ref, h_ref, buf_ref, zero_ref, hbuf, sem, zsem, lsem, rsem):
    tm = hbuf.shape[1]
    step = pl.program_id(0)

    def zero_block(b):
        row = pl.multiple_of(b * MOE_BLOCK, MOE_BLOCK)
        return pltpu.make_async_copy(zero_ref, buf_ref.at[pl.ds(row, MOE_BLOCK)], zsem)

    def for_empty_blocks(fn):
        def body(b, carry):
            @pl.when(empty_ref[b] != 0)
            def _():
                fn(b)
            return carry
        lax.fori_loop(0, empty_ref.shape[0], body, 0)

    @pl.when(step == 0)
    def _zero_fill():
        zero_ref[...] = jnp.zeros(zero_ref.shape, zero_ref.dtype)
        for e in range(N_EXPERTS):
            start = pl.multiple_of(tail_ref[e] // SUBLANES * SUBLANES, SUBLANES)
            pltpu.make_async_copy(zero_ref, buf_ref.at[pl.ds(start, MOE_BLOCK)], sem).start()
        for e in range(N_EXPERTS):
            pltpu.make_async_copy(zero_ref, buf_ref.at[pl.ds(0, MOE_BLOCK)], sem).wait()
        for_empty_blocks(lambda b: zero_block(b).start())

    @pl.when(step == pl.num_programs(0) - 1)
    def _zero_done():
        for_empty_blocks(lambda b: zero_block(b).wait())

    n_slots = hbuf.shape[0]
    slot = step % n_slots

    def load(tile, sl):
        row = pl.multiple_of(tile * tm, tm)
        return pltpu.make_async_copy(h_ref.at[pl.ds(row, tm)], hbuf.at[sl], lsem.at[sl])

    @pl.when(step == 0)
    def _():
        load(0, 0).start()

    @pl.when(step + 1 < pl.num_programs(0))
    def _():
        load(step + 1, (step + 1) % n_slots).start()

    load(step, slot).wait()

    def issue(i, carry):
        for k in range(TOP_K):
            _row_copy(hbuf.at[slot], i, buf_ref, dest_ref[0, 0, TOP_K * i + k], rsem.at[slot]).start()
        return carry

    lax.fori_loop(0, tm, issue, 0, unroll=DMA_UNROLL)

    def wait_tile(sl):
        for _ in range(TOP_K):
            pltpu.make_async_copy(hbuf.at[sl], buf_ref.at[pl.ds(0, tm)], rsem.at[sl]).wait()

    pl.when(step > 0)(lambda: wait_tile((step + n_slots - 1) % n_slots))
    pl.when(step == pl.num_programs(0) - 1)(lambda: wait_tile(slot))


def _dispatch(dest, tail_row, empty_block, h2):
    r, d = h2.shape
    tm = DISPATCH_TILE
    assert r % tm == 0
    dest3 = dest.reshape(r // tm, 1, TOP_K * tm)
    n_rows = empty_block.shape[0] * MOE_BLOCK
    return pl.pallas_call(
        _dispatch_kernel,
        grid=(r // tm,),
        in_specs=[pl.BlockSpec((1, 1, TOP_K * tm), lambda i: (i, 0, 0), memory_space=pltpu.SMEM),
                  pl.BlockSpec(memory_space=pltpu.SMEM),
                  pl.BlockSpec(memory_space=pltpu.SMEM),
                  pl.BlockSpec(memory_space=pl.ANY)],
        out_specs=pl.BlockSpec(memory_space=pl.ANY),
        out_shape=jax.ShapeDtypeStruct((n_rows, d), h2.dtype),
        scratch_shapes=[pltpu.VMEM((MOE_BLOCK, d), h2.dtype), pltpu.VMEM((3, tm, d), h2.dtype),
                        pltpu.SemaphoreType.DMA, pltpu.SemaphoreType.DMA, pltpu.SemaphoreType.DMA((3,)),
                        pltpu.SemaphoreType.DMA((3,))],
        compiler_params=_params(("arbitrary",)),
        name="moe_dispatch",
    )(dest3, tail_row, empty_block, h2)


def _expert_kernel(bq_ref, lo_ref, hi_ref, mode_ref, el_ref, nq_ref, ub_ref, x_ref, wg_hbm, wu_hbm, wd_hbm, o_ref,
                   wgb, wub, wdb, sg, su, sd, sems):
    b = pl.program_id(0)
    q = bq_ref[b]
    slot = q % 2
    mats = ((wg_hbm, sg, wgb), (wu_hbm, su, wub), (wd_hbm, sd, wdb))
    n_pieces = W_PIECES * len(mats)

    def piece_copy(m, qt, t):
        hbm, stage, _ = mats[m]
        pr = stage.shape[1]
        return pltpu.make_async_copy(hbm.at[el_ref[qt], pl.ds(t * pr, pr), :], stage.at[t % 2],
                                     sems.at[m, t % 2])

    def process(qt, lo, hi):
        for p in range(n_pieces):
            m, t = p % len(mats), p // len(mats)

            @pl.when((lo <= p) & (p < hi))
            def _():
                _, stage, resident = mats[m]
                pr = stage.shape[1]
                piece_copy(m, qt, t).wait()
                resident[qt % 2, pl.ds(t * pr, pr), :] = stage[t % 2].astype(BF16)
                q2 = qt + (t + 2) // W_PIECES

                @pl.when(q2 < nq_ref[0])
                def _():
                    piece_copy(m, q2, (t + 2) % W_PIECES).start()

    @pl.when(b == 0)
    def _first_expert():
        for m in range(len(mats)):
            for t in range(2):
                piece_copy(m, 0, t).start()
        process(0, 0, n_pieces)

    def compute(rows):
        xb = x_ref[:rows, :].astype(BF16)
        hid = wgb.shape[2]
        acts = []
        for h0 in range(0, hid, HID_TILE):
            hs = pl.ds(h0, HID_TILE)
            acts.append((_silu(_bdot(xb, wgb[slot, :, hs])) * _bdot(xb, wub[slot, :, hs])).astype(BF16))
        o_ref[:rows, :] = _bdot(jnp.concatenate(acts, axis=1), wdb[slot])
        if rows < o_ref.shape[0]:
            o_ref[rows:, :] = jnp.zeros((o_ref.shape[0] - rows, o_ref.shape[1]), o_ref.dtype)

    n_modes = len(BLOCK_ROW_MODES)
    for mode, rows in enumerate(BLOCK_ROW_MODES):
        pl.when(mode_ref[b] == mode)(functools.partial(compute, rows))

    @pl.when(mode_ref[b] == n_modes)
    def _unused():
        o_ref[...] = jnp.zeros(o_ref.shape, o_ref.dtype)

    process(q + 1, lo_ref[b], hi_ref[b])


def _experts(plan, buf, wg, wu, wd, n_out_rows):
    d = buf.shape[1]
    hid = wg.shape[2]
    assert d % W_PIECES == 0 and hid % W_PIECES == 0 and W_PIECES % 2 == 0
    blk = lambda f: pl.BlockSpec((MOE_BLOCK, d), f)
    return pl.pallas_call(
        _expert_kernel,
        grid_spec=pltpu.PrefetchScalarGridSpec(
            num_scalar_prefetch=len(plan),
            grid=(n_out_rows // MOE_BLOCK,),
            in_specs=[blk(lambda b, bq, lo, hi, md, el, nq, ub: (jnp.minimum(b, ub[0] - 1), 0)),
                      pl.BlockSpec(memory_space=pl.ANY), pl.BlockSpec(memory_space=pl.ANY),
                      pl.BlockSpec(memory_space=pl.ANY)],
            out_specs=blk(lambda b, bq, lo, hi, md, el, nq, ub: (b, 0)),
            scratch_shapes=[pltpu.VMEM((2, d, hid), BF16), pltpu.VMEM((2, d, hid), BF16),
                            pltpu.VMEM((2, hid, d), BF16),
                            pltpu.VMEM((2, d // W_PIECES, hid), F32), pltpu.VMEM((2, d // W_PIECES, hid), F32),
                            pltpu.VMEM((2, hid // W_PIECES, d), F32),
                            pltpu.SemaphoreType.DMA((3, 2))],
        ),
        out_shape=jax.ShapeDtypeStruct((n_out_rows, d), F32),
        compiler_params=_params(("arbitrary",)),
        name="moe_experts",
    )(*plan, buf, wg, wu, wd)


def _combine_kernel(dest_ref, dnext_ref, y_ref, rw_ref, x1_ref, mod_ref, lng_ref, lnb_ref, o_ref,
                    g_even, g_odd, sem):
    tm = x1_ref.shape[0]
    i = pl.program_id(0)
    n_groups = tm // DMA_UNROLL
    has_next = i + 1 < pl.num_programs(0)

    def gather(d_ref, g_ref, sl):
        def body(j, carry):
            for r in range(DMA_UNROLL):
                row = j * DMA_UNROLL + r
                for k in range(TOP_K):
                    _row_copy(y_ref, d_ref[0, 0, TOP_K * row + k], g_ref.at[k], row, sem.at[sl]).start()
            return carry
        lax.fori_loop(0, n_groups, body, 0)

    @pl.when(i == 0)
    def _():
        gather(dest_ref, g_even, 0)

    def run(sl, g_cur, g_next, prefetch):
        if prefetch:
            gather(dnext_ref, g_next, 1 - sl)
        for k in range(TOP_K):
            pltpu.make_async_copy(y_ref.at[pl.ds(0, tm)], g_cur.at[k], sem.at[sl]).wait()
        f = rw_ref[:, 0:1] * g_cur[0] + rw_ref[:, 1:2] * g_cur[1]
        o_ref[...] = _ln(ALPHA * x1_ref[...] + mod_ref[5:6, :] * f) * lng_ref[...] + lnb_ref[...]

    for sl, g_cur, g_next in ((0, g_even, g_odd), (1, g_odd, g_even)):
        for prefetch in (True, False):
            pl.when((i % 2 == sl) & (has_next == prefetch))(functools.partial(run, sl, g_cur, g_next, prefetch))


def _combine(dest, y, rw, x1, mod3, ln_g, ln_b, seq):
    r, d = x1.shape
    tm = COMBINE_TILE
    assert seq % tm == 0
    tiles_per_seq = seq // tm
    dest3 = dest.reshape(r // tm, 1, TOP_K * tm)
    n = r // tm
    row = lambda w: pl.BlockSpec((tm, w), lambda i: (i, 0))
    return pl.pallas_call(
        _combine_kernel,
        grid=(n,),
        in_specs=[pl.BlockSpec((1, 1, TOP_K * tm), lambda i: (i, 0, 0), memory_space=pltpu.SMEM),
                  pl.BlockSpec((1, 1, TOP_K * tm), lambda i: (jnp.minimum(i + 1, n - 1), 0, 0),
                               memory_space=pltpu.SMEM),
                  pl.BlockSpec(memory_space=pl.ANY),
                  row(LANES), row(d),
                  pl.BlockSpec((None, 6, d), lambda i: (i // tiles_per_seq, 0, 0)),
                  _resident((1, d)), _resident((1, d))],
        out_specs=row(d),
        out_shape=jax.ShapeDtypeStruct((r, d), F32),
        scratch_shapes=[pltpu.VMEM((TOP_K, tm, d), F32), pltpu.VMEM((TOP_K, tm, d), F32),
                        pltpu.SemaphoreType.DMA((2,))],
        compiler_params=_params(("arbitrary",)),
        name="moe_combine_ln",
    )(dest3, dest3, y, rw, x1, mod3, ln_g.reshape(1, d), ln_b.reshape(1, d))


def _routing_plan(ri, n_tokens):
    eid = ri[:, :TOP_K].reshape(-1)
    m = n_tokens * TOP_K
    onehot = (eid[:, None] == jnp.arange(N_EXPERTS, dtype=jnp.int32)[None, :]).astype(jnp.int32)
    csum = jnp.cumsum(onehot, axis=0)
    counts = csum[-1]
    rank = jnp.sum(csum * onehot, axis=1) - 1
    padded = (counts + MOE_BLOCK - 1) // MOE_BLOCK * MOE_BLOCK
    pad_end = jnp.cumsum(padded)
    pad_start = pad_end - padded
    dest = jnp.sum(onehot * pad_start[None, :], axis=1) + rank
    tail = pad_start + counts
    n_blocks = m // MOE_BLOCK + N_EXPERTS
    used_blocks = pad_end[-1] // MOE_BLOCK
    empty = jnp.arange(n_blocks + 1, dtype=jnp.int32) >= used_blocks

    n_pieces = 3 * W_PIECES
    has = counts > 0
    ordinal = jnp.cumsum(has.astype(jnp.int32)) - 1
    n_ord = ordinal[-1] + 1
    e_ids = jnp.arange(N_EXPERTS, dtype=jnp.int32)
    ord_expert = jnp.sum(jnp.where(has[None, :] & (ordinal[None, :] == e_ids[:, None]), e_ids[None, :], 0), axis=1)
    bid = jnp.arange(n_blocks, dtype=jnp.int32)
    brow = bid[:, None] * MOE_BLOCK
    in_e = ((pad_start[None, :] <= brow) & (brow < pad_end[None, :])).astype(jnp.int32)
    pick = lambda v: jnp.sum(in_e * v[None, :], axis=1)
    used = bid < used_blocks
    blk_q = jnp.where(used, pick(ordinal), n_ord - 1)
    i_in_e = bid - pick(pad_start) // MOE_BLOCK
    k_e = jnp.maximum(pick(padded) // MOE_BLOCK, 1)
    brings = used & (blk_q + 1 < n_ord)
    lo = jnp.where(brings, n_pieces * i_in_e // k_e, 0)
    hi = jnp.where(brings, n_pieces * (i_in_e + 1) // k_e, 0)
    tok_rows = jnp.clip(pick(counts) - i_in_e * MOE_BLOCK, 0, MOE_BLOCK)
    mode = jnp.sum(tok_rows[:, None] > jnp.asarray(BLOCK_ROW_MODES, jnp.int32)[None, :], axis=1)
    mode = jnp.where(used, mode, len(BLOCK_ROW_MODES))
    i32 = lambda a: a.astype(jnp.int32)
    plan = (i32(blk_q), i32(lo), i32(hi), i32(mode), i32(ord_expert), i32(n_ord.reshape(1)),
            i32(used_blocks.reshape(1)))
    return i32(dest), i32(tail), i32(empty), plan, n_blocks * MOE_BLOCK


def _rope_tables(seq):
    n_freq = HEAD_DIM // 4
    inv_freq = ROPE_THETA ** (-jnp.arange(n_freq, dtype=F32) / n_freq)
    rows = seq // GRID_W
    ar = jnp.arange(rows, dtype=F32)[:, None] * inv_freq
    ac = jnp.arange(GRID_W, dtype=F32)[:, None] * inv_freq
    by_row = lambda t: jnp.broadcast_to(t[:, None, :], (rows, GRID_W, n_freq)).reshape(seq, n_freq)
    by_col = lambda t: jnp.broadcast_to(t[None, :, :], (rows, GRID_W, n_freq)).reshape(seq, n_freq)
    cos_r, sin_r, cos_c, sin_c = by_row(jnp.cos(ar)), by_row(jnp.sin(ar)), by_col(jnp.cos(ac)), by_col(jnp.sin(ac))
    zero = jnp.zeros_like(cos_r)
    cos_t = jnp.concatenate([cos_r, cos_r, cos_c, cos_c], axis=1)
    sin_a = jnp.concatenate([-sin_r, zero, -sin_c, zero], axis=1)
    sin_b = jnp.concatenate([zero, sin_r, zero, sin_c], axis=1)
    return cos_t, sin_a, sin_b


def kernel(x, c, ctx, c_ctx, w_ada, b_ada, w_in, w_gate_up, b_gate, attn_sink, gla_norm_w, w_out, ln1_g, ln1_b, w_router_group, b_router_group, w_router_expert, b_router_expert, w_exp_gate, w_exp_up, w_exp_down, ln2_g, ln2_b):
    batch, seq, d = x.shape
    n_ctx = ctx.shape[1]
    assert w_ada.shape[0] == DEPTH and batch < MOD_ROWS
    assert seq % GLA_STEP == 0 and n_ctx == GLA_STEP
    n_tok = batch * seq
    a_width = d // 2
    kv_width = a_width // A_GROUP
    b_width = d - a_width
    key_width = b_width // 2
    layer = 0

    cc = jnp.concatenate([c, c_ctx[None, :], jnp.zeros((MOD_ROWS - batch - 1, d), F32)], axis=0)
    mod3 = _adaln(cc, w_ada[layer], b_ada[layer]).reshape(MOD_ROWS, 6, d)

    splits = (a_width, kv_width, kv_width, key_width, key_width, b_width, b_width, 2 * GATE_RANK)
    w_in_b = w_in[layer].astype(BF16)
    zero_up = jnp.zeros((GATE_RANK, key_width), F32)
    wup2 = jnp.concatenate([jnp.concatenate([w_gate_up[layer, 0], zero_up], axis=1),
                            jnp.concatenate([zero_up, w_gate_up[layer, 1]], axis=1)], axis=0).astype(BF16)
    bg2 = b_gate[layer].reshape(1, 2 * key_width)
    tables = _rope_tables(seq)

    xf = x.reshape(n_tok, d)
    qa, ka, va, qb, kb, vb, rb, la_f, la_b = _project(
        xf, mod3, lambda i, per_seq: i // per_seq, tables, w_in_b, splits, wup2, bg2, rope=True, seq=seq)
    _, ka_c, va_c, _, kb_c, vb_c, _, lac_f, lac_b = _project(
        ctx.reshape(batch * n_ctx, d), mod3, lambda i, per_seq: batch, tables, w_in_b, splits, wup2, bg2,
        rope=False, seq=seq, keys_values_only=True)

    out_a = _attention(attn_sink[layer], qa, ka, va, ka_c, va_c, batch, seq, n_ctx)
    o_b = _gla(kb_c, vb_c, lac_b, qb, kb, vb, la_b, batch, seq, n_ctx, reverse=True)
    out_b = _gla(kb_c, vb_c, lac_f, qb, kb, vb, la_f, batch, seq, n_ctx, reverse=False,
                 extra=(o_b, rb, gla_norm_w[layer].reshape(1, b_width)))

    w_out_b = w_out[layer].astype(BF16)
    wr = jnp.concatenate([w_router_expert[layer], w_router_group[layer],
                          jnp.zeros((d, LANES - N_EXPERTS - N_GROUPS), F32)], axis=1).astype(BF16)
    br = jnp.concatenate([b_router_expert[layer], b_router_group[layer],
                          jnp.zeros((LANES - N_EXPERTS - N_GROUPS,), F32)]).reshape(1, LANES)
    x1, h2, rw, ri = _outproj(out_a, out_b, xf, mod3, w_out_b[:a_width], w_out_b[a_width:],
                              ln1_g[layer], ln1_b[layer], wr, br, seq)

    dest, tail_row, empty_block, plan, n_buf_rows = _routing_plan(ri, n_tok)
    buf = _dispatch(dest, tail_row, empty_block, h2)
    y = _experts(plan, buf, w_exp_gate[layer], w_exp_up[layer], w_exp_down[layer], n_buf_rows)
    out = _combine(dest, y, rw, x1, mod3, ln2_g[layer], ln2_b[layer], seq)
    return out.reshape(batch, seq, d)
```

```python
import functools

import jax
import jax.numpy as jnp
from jax import lax
from jax.experimental import pallas as pl
from jax.experimental.pallas import tpu as pltpu

F32 = jnp.float32
BF16 = jnp.bfloat16

HEAD_DIM = 128
GRID_W = 64
WINDOW = 128
A_BLOCK = 128
A_GROUP = 4
ROPE_THETA = 10000.0
B_HEADS = 4
GATE_RANK = 16
GATE_TAU = 16.0
GLA_CHUNK = 64
N_GROUPS = 4
EXPERTS_PER_GROUP = 8
N_EXPERTS = N_GROUPS * EXPERTS_PER_GROUP
TOP_K = 2
DEPTH = 1
ALPHA = (2.0 * DEPTH) ** 0.25
LN_EPS = 1e-6
LOG2_E = 1.4426950408889634
ATTN_EXP2_SCALE = HEAD_DIM ** -0.5 * LOG2_E

LANES = 128
SUBLANES = 8
MOD_ROWS = 8
VMEM_LIMIT = 56 * 1024 * 1024

DISPATCH_TILE = 1024
COMBINE_TILE = 256
OUT_TILE = 512
SUB_ROWS = 256
PROJ_COLS = 1024
Q_BLOCKS = 8
GLA_STEP = 256
MOE_BLOCK = 256
HID_TILE = 512
W_PIECES = 4
DMA_UNROLL = 8


def _params(sem):
    return pltpu.CompilerParams(dimension_semantics=sem, vmem_limit_bytes=VMEM_LIMIT)


def _resident(shape):
    nd = len(shape)
    return pl.BlockSpec(shape, lambda *_: (0,) * nd, pipeline_mode=pl.Buffered(1))


def _ln(x):
    mu = jnp.mean(x, axis=-1, keepdims=True)
    xc = x - mu
    var = jnp.mean(xc * xc, axis=-1, keepdims=True)
    return xc * lax.rsqrt(var + LN_EPS)


def _silu(x):
    return x * jax.nn.sigmoid(x)


def _bdot(a, b):
    return jnp.dot(a, b, preferred_element_type=F32)


def _bdot_nt(a, b):
    return lax.dot_general(a, b, (((1,), (1,)), ((), ())), preferred_element_type=F32)


def _adaln_kernel(c_ref, w_ref, b_ref, o_ref):
    s = _silu(c_ref[...]).astype(BF16)
    o_ref[...] = _bdot(s, w_ref[...].astype(BF16)) + b_ref[...]


def _adaln(cc, w_ada, b_ada):
    d, n = w_ada.shape
    tn = 1024
    return pl.pallas_call(
        _adaln_kernel,
        grid=(n // tn,),
        in_specs=[pl.BlockSpec((MOD_ROWS, d), lambda j: (0, 0)),
                  pl.BlockSpec((d, tn), lambda j: (0, j)),
                  pl.BlockSpec((1, tn), lambda j: (0, j))],
        out_specs=pl.BlockSpec((MOD_ROWS, tn), lambda j: (0, j)),
        out_shape=jax.ShapeDtypeStruct((MOD_ROWS, n), F32),
        compiler_params=_params(("arbitrary",)),
        name="adaln",
    )(cc, w_ada, b_ada.reshape(1, n))


def _proj_kernel(x_ref, mod_ref, cos_ref, sina_ref, sinb_ref, w_ref, wup, bg, *out_refs, groups, rope):
    for r0 in range(0, x_ref.shape[0], SUB_ROWS):
        _proj_rows(slice(r0, r0 + SUB_ROWS), x_ref, mod_ref, cos_ref, sina_ref, sinb_ref, w_ref, wup, bg,
                   out_refs, groups, rope)


def _proj_rows(rs, x_ref, mod_ref, cos_ref, sina_ref, sinb_ref, w_ref, wup, bg, out_refs, groups, rope):
    h = _ln(x_ref[rs, :]) * (1.0 + mod_ref[1:2, :]) + mod_ref[0:1, :]
    hb = h.astype(BF16)

    def rot(t):
        return (t * cos_ref[rs, :] + pltpu.roll(t, 96, 1) * sina_ref[rs, :]
                + pltpu.roll(t, 32, 1) * sinb_ref[rs, :])

    col = 0
    refs = iter(out_refs)
    for n, wanted, rotary, post in groups:
        if wanted:
            o_ref = next(refs)
            step = min(n, PROJ_COLS)
            for c0 in range(0, n, step):
                t = _bdot(hb, w_ref[:, col + c0:col + c0 + step])
                if rotary and rope:
                    for l0 in range(0, step, LANES):
                        r = rot(t[:, l0:l0 + LANES])
                        o_ref[rs, c0 + l0:c0 + l0 + LANES] = (r if post == 1.0 else r * post).astype(o_ref.dtype)
                else:
                    o_ref[rs, c0:c0 + step] = t.astype(o_ref.dtype)
        col += n
    laf_o, lab_o = refs
    gl = _bdot(hb, w_ref[:, col:]).astype(BF16)
    z = _bdot(gl, wup[...]) + bg[...]
    la = (jnp.minimum(z, 0.0) - jnp.log1p(jnp.exp(-jnp.abs(z)))) / GATE_TAU
    kw = laf_o.shape[1]
    laf_o[rs, :] = la[:, :kw]
    lab_o[rs, :] = la[:, kw:]


def _project(xf, mod3, mod_row, tables, w_in_b, splits, wup2, bg2, *, rope, seq, keys_values_only=False):
    r, d = xf.shape
    tm = OUT_TILE
    assert r % tm == 0 and seq % tm == 0 and tm % SUB_ROWS == 0
    tiles_per_seq = seq // tm
    mod_row_of_tile = lambda i: mod_row(i, tiles_per_seq)
    n_gate = splits[-1]
    assert sum(splits) == w_in_b.shape[1] and (sum(splits) - n_gate) % LANES == 0
    kv = keys_values_only
    spec = ((BF16, not kv, True, ATTN_EXP2_SCALE), (BF16, True, True, 1.0), (BF16, True, False, 1.0),
            (BF16, not kv, False, 1.0), (BF16, True, False, 1.0), (BF16, True, False, 1.0),
            (BF16, not kv, False, 1.0))
    groups = tuple((n, wanted, rotary, post) for n, (_, wanted, rotary, post) in zip(splits, spec))
    key_width = splits[3]
    outs = [(n, dt) for n, (dt, wanted, _, _) in zip(splits, spec) if wanted] + [(key_width, F32)] * 2
    cos_t, sina_t, sinb_t = tables
    row = lambda n: pl.BlockSpec((tm, n), lambda i: (i, 0))
    tab = pl.BlockSpec((tm, LANES), lambda i: (i % tiles_per_seq, 0))
    res = pl.pallas_call(
        functools.partial(_proj_kernel, groups=groups, rope=rope),
        grid=(r // tm,),
        in_specs=[row(d),
                  pl.BlockSpec((None, 6, d), lambda i: (mod_row_of_tile(i), 0, 0)),
                  tab, tab, tab,
                  _resident(w_in_b.shape), _resident(wup2.shape), _resident(bg2.shape)],
        out_specs=[row(n) for n, _ in outs],
        out_shape=[jax.ShapeDtypeStruct((r, n), dt) for n, dt in outs],
        compiler_params=_params(("parallel",)),
        name="in_proj_rope" if rope else "in_proj_ctx",
    )(xf, mod3, cos_t, sina_t, sinb_t, w_in_b, wup2, bg2)
    res = iter(res)
    return tuple(next(res) if wanted else None for _, wanted, _, _ in spec) + tuple(res)


def _attn_kernel(sink_ref, q_ref, *refs):
    n_kv = Q_BLOCKS + 2
    k_refs, v_refs = refs[:n_kv], refs[n_kv:2 * n_kv]
    kx_ref, vx_ref, o_ref = refs[2 * n_kv:]
    n = pl.program_id(1)
    last = pl.num_programs(1) - 1
    blk = A_BLOCK
    rows = A_GROUP * blk
    n_ctx = kx_ref.shape[0]
    qi = lax.broadcasted_iota(jnp.int32, (rows, blk), 0) % blk
    kj = lax.broadcasted_iota(jnp.int32, (rows, blk), 1)
    for u in range(Q_BLOCKS):
        ok_prev = (kj >= qi) & (n > 0) if u == 0 else (kj >= qi)
        ok_next = (kj <= qi) & (n < last) if u == Q_BLOCKS - 1 else (kj <= qi)
        qs = slice(u * blk, (u + 1) * blk)
        kp_ref, kc_ref, kn_ref = k_refs[u:u + 3]
        vp_ref, vc_ref, vn_ref = v_refs[u:u + 3]
        for hk in range(kc_ref.shape[1] // HEAD_DIM):
            ks = slice(hk * HEAD_DIM, (hk + 1) * HEAD_DIM)
            q4 = jnp.concatenate(
                [q_ref[qs, (hk * A_GROUP + g) * HEAD_DIM:(hk * A_GROUP + g + 1) * HEAD_DIM]
                 for g in range(A_GROUP)], axis=0)
            s_ctx = _bdot_nt(q4, kx_ref[:, ks])
            s_prev = jnp.where(ok_prev, _bdot_nt(q4, kp_ref[:, ks]), -jnp.inf)
            s_cur = _bdot_nt(q4, kc_ref[:, ks])
            s_next = jnp.where(ok_next, _bdot_nt(q4, kn_ref[:, ks]), -jnp.inf)
            sink = jnp.concatenate(
                [jnp.full((blk, 1), sink_ref[hk * A_GROUP + g] * LOG2_E, F32) for g in range(A_GROUP)], axis=0)
            m_lanes = jnp.maximum(jnp.maximum(s_prev, s_cur), s_next)
            for l0 in range(0, n_ctx, LANES):
                m_lanes = jnp.maximum(m_lanes, s_ctx[:, l0:l0 + LANES])
            m = jnp.maximum(jnp.max(m_lanes, axis=-1, keepdims=True), sink)
            e_ctx = jnp.exp2(s_ctx - m)
            e_prev = jnp.exp2(s_prev - m)
            e_cur = jnp.exp2(s_cur - m)
            e_next = jnp.exp2(s_next - m)
            e_lanes = e_prev + e_cur + e_next
            for l0 in range(0, n_ctx, LANES):
                e_lanes = e_lanes + e_ctx[:, l0:l0 + LANES]
            den = jnp.exp2(sink - m) + jnp.sum(e_lanes, axis=-1, keepdims=True)
            o = (_bdot(e_ctx.astype(BF16), vx_ref[:, ks])
                 + (_bdot(e_prev.astype(BF16), vp_ref[:, ks])
                    + _bdot(e_cur.astype(BF16), vc_ref[:, ks])
                    + _bdot(e_next.astype(BF16), vn_ref[:, ks]))) * (1.0 / den)
            for g in range(A_GROUP):
                hq = hk * A_GROUP + g
                o_ref[qs, hq * HEAD_DIM:(hq + 1) * HEAD_DIM] = o[g * blk:(g + 1) * blk, :].astype(o_ref.dtype)


def _attention(sink, qa, ka, va, ka_c, va_c, batch, seq, n_ctx):
    blk = A_BLOCK
    nb = seq // blk
    assert nb % Q_BLOCKS == 0
    steps = nb // Q_BLOCKS
    aw = qa.shape[1]
    kvw = ka.shape[1]
    kv = lambda j: pl.BlockSpec(
        (blk, kvw), lambda b, n: (b * nb + jnp.clip(n * Q_BLOCKS - 1 + j, 0, nb - 1), 0))
    kv_specs = [kv(j) for j in range(Q_BLOCKS + 2)]
    qo = pl.BlockSpec((Q_BLOCKS * blk, aw), lambda b, n: (b * steps + n, 0))
    ctx = pl.BlockSpec((n_ctx, kvw), lambda b, n: (b, 0))
    return pl.pallas_call(
        _attn_kernel,
        grid=(batch, steps),
        in_specs=[pl.BlockSpec(memory_space=pltpu.SMEM), qo] + kv_specs + kv_specs + [ctx, ctx],
        out_specs=qo,
        out_shape=jax.ShapeDtypeStruct((batch * seq, aw), BF16),
        compiler_params=_params(("parallel", "parallel")),
        name="window_gqa",
    )(sink, qa, *([ka] * (Q_BLOCKS + 2)), *([va] * (Q_BLOCKS + 2)), ka_c, va_c)


def _chunk_cumsum(g, *, reverse):
    rows = g.shape[0]
    p = lax.broadcasted_iota(jnp.int32, g.shape, 0) % GLA_CHUNK
    s = 1
    while s < GLA_CHUNK:
        if reverse:
            g = g + jnp.where(p < GLA_CHUNK - s, pltpu.roll(g, rows - s, 0), 0.0)
        else:
            g = g + jnp.where(p >= s, pltpu.roll(g, s, 0), 0.0)
        s *= 2
    return g


def _per_chunk_row(x, i):
    c = GLA_CHUNK
    return jnp.concatenate(
        [jnp.broadcast_to(x[j * c + i:j * c + i + 1, :], (c, x.shape[1])) for j in range(x.shape[0] // c)], axis=0)


def _gla_block(q, k, v, cum, state_t, mask, *, reverse, need_o):
    c = GLA_CHUNK
    rows, dk = k.shape
    n_chunks = rows // c
    k = k.astype(F32)
    i_last = 0 if reverse else c - 1
    i_mid = c // 2 if reverse else c // 2 - 1
    b_last = _per_chunk_row(cum, i_last)
    kdec = (k * jnp.exp(b_last - cum)).astype(BF16)
    if need_o:
        b_mid = _per_chunk_row(cum, i_mid)
        qc = q.astype(F32) * dk ** -0.5
        qm = (qc * jnp.exp(cum - b_mid)).astype(BF16)
        km = (k * jnp.exp(b_mid - cum)).astype(BF16)
        a = jnp.where(mask, _bdot_nt(qm, km), 0.0)
        o_intra = _bdot(a.astype(BF16), v)
        qe = (qc * jnp.exp(cum)).astype(BF16)
    o_inter = [None] * n_chunks
    for j in (reversed(range(n_chunks)) if reverse else range(n_chunks)):
        rs = slice(j * c, (j + 1) * c)
        if need_o:
            o_inter[j] = _bdot_nt(qe[rs, :], state_t.astype(BF16))
        kv_t = lax.dot_general(v[rs, :], kdec[rs, :], (((0,), (0,)), ((), ())), preferred_element_type=F32)
        state_t = state_t * jnp.exp(cum[j * c + i_last:j * c + i_last + 1, :]) + kv_t
    o = o_intra + jnp.concatenate(o_inter, axis=0) if need_o else None
    return o, state_t


def _gla_kernel(*refs, reverse, final):
    if final:
        (kx_ref, vx_ref, gx_ref, q_ref, k_ref, v_ref, g_ref, ob_ref, r_ref, nw_ref, o_ref, st_ref) = refs
    else:
        (kx_ref, vx_ref, gx_ref, q_ref, k_ref, v_ref, g_ref, o_ref, st_ref) = refs
    t = pl.program_id(1)
    n_heads = st_ref.shape[0]
    dv, dk = st_ref.shape[1:]

    @pl.when(t == 0)
    def _context():
        cum = _chunk_cumsum(gx_ref[...], reverse=reverse)
        for h in range(n_heads):
            ks, vs = slice(h * dk, (h + 1) * dk), slice(h * dv, (h + 1) * dv)
            _, st = _gla_block(None, kx_ref[:, ks], vx_ref[:, vs], cum[:, ks], jnp.zeros((dv, dk), F32), None,
                               reverse=reverse, need_o=False)
            st_ref[h] = st

    @pl.when(t > 0)
    def _latent():
        rows = k_ref.shape[0]
        cum = _chunk_cumsum(g_ref[...], reverse=reverse)
        r = lax.broadcasted_iota(jnp.int32, (rows, rows), 0)
        s = lax.broadcasted_iota(jnp.int32, (rows, rows), 1)
        causal = (s >= r) if reverse else (s <= r)
        mask = causal & ((r // GLA_CHUNK) == (s // GLA_CHUNK))
        for h in range(n_heads):
            ks, vs = slice(h * dk, (h + 1) * dk), slice(h * dv, (h + 1) * dv)
            o, st = _gla_block(q_ref[:, ks], k_ref[:, ks], v_ref[:, vs], cum[:, ks], st_ref[h], mask,
                               reverse=reverse, need_o=True)
            st_ref[h] = st
            if final:
                o = o + ob_ref[:, vs].astype(F32)
                o = o * lax.rsqrt(jnp.mean(o * o, axis=-1, keepdims=True) + LN_EPS)
                o = o * nw_ref[:, vs]
                o = o * _silu(r_ref[:, vs].astype(F32))
            o_ref[:, vs] = o.astype(o_ref.dtype)


def _gla(kb_c, vb_c, la_c, qb, kb, vb, la, batch, seq, n_ctx, *, reverse, extra=None):
    assert n_ctx % GLA_CHUNK == 0 and GLA_STEP % GLA_CHUNK == 0
    tb = GLA_STEP
    nt = seq // tb
    kw, vw = qb.shape[1], vb.shape[1]
    final = extra is not None

    def lat(b, t):
        i = jnp.maximum(t - 1, 0)
        if reverse:
            i = nt - 1 - i
        return (b * nt + i, 0)

    cx = lambda b, t: (b, 0)
    in_specs = [pl.BlockSpec((n_ctx, kw), cx), pl.BlockSpec((n_ctx, vw), cx), pl.BlockSpec((n_ctx, kw), cx),
                pl.BlockSpec((tb, kw), lat), pl.BlockSpec((tb, kw), lat),
                pl.BlockSpec((tb, vw), lat), pl.BlockSpec((tb, kw), lat)]
    args = [kb_c, vb_c, la_c, qb, kb, vb, la]
    if final:
        o_other, rb, norm_w = extra
        in_specs += [pl.BlockSpec((tb, vw), lat), pl.BlockSpec((tb, vw), lat), _resident((1, vw))]
        args += [o_other, rb, norm_w]
    return pl.pallas_call(
        functools.partial(_gla_kernel, reverse=reverse, final=final),
        grid=(batch, nt + 1),
        in_specs=in_specs,
        out_specs=pl.BlockSpec((tb, vw), lat),
        out_shape=jax.ShapeDtypeStruct((batch * seq, vw), BF16),
        scratch_shapes=[pltpu.VMEM((B_HEADS, vw // B_HEADS, kw // B_HEADS), F32)],
        compiler_params=_params(("parallel", "arbitrary")),
        name="gla_fwd_out" if final else "gla_bwd",
    )(*args)


def _outproj_kernel(oa_ref, ob_ref, x_ref, mod_ref, wt_ref, wb_ref, lng_ref, lnb_ref, wr_ref, br_ref,
                    x1_o, h2_o, rw_o, ri_o):
    for r0 in range(0, x_ref.shape[0], SUB_ROWS):
        _outproj_rows(slice(r0, r0 + SUB_ROWS), oa_ref, ob_ref, x_ref, mod_ref, wt_ref, wb_ref, lng_ref, lnb_ref,
                      wr_ref, br_ref, x1_o, h2_o, rw_o, ri_o)


def _outproj_rows(rs, oa_ref, ob_ref, x_ref, mod_ref, wt_ref, wb_ref, lng_ref, lnb_ref, wr_ref, br_ref,
                  x1_o, h2_o, rw_o, ri_o):
    y = _bdot(oa_ref[rs, :], wt_ref[...]) + _bdot(ob_ref[rs, :], wb_ref[...])
    x1 = _ln(ALPHA * x_ref[rs, :] + mod_ref[2:3, :] * y) * lng_ref[...] + lnb_ref[...]
    x1_o[rs, :] = x1
    h2 = _ln(x1) * (1.0 + mod_ref[4:5, :]) + mod_ref[3:4, :]
    h2_o[rs, :] = h2
    lg = _bdot(h2.astype(BF16), wr_ref[...]) + br_ref[...]
    lane = lax.broadcasted_iota(jnp.int32, lg.shape, 1)
    lanef = lane.astype(F32)
    big = float(LANES)
    is_g = (lane >= N_EXPERTS) & (lane < N_EXPERTS + N_GROUPS)
    gl = jnp.where(is_g, lg, -jnp.inf)
    gmax = jnp.max(gl, axis=-1, keepdims=True)
    pg_top = 1.0 / jnp.sum(jnp.exp(gl - gmax), axis=-1, keepdims=True)
    grp = jnp.min(jnp.where(gl == gmax, lanef, big), axis=-1, keepdims=True) - N_EXPERTS
    in_grp = (lane < N_EXPERTS) & ((lane // EXPERTS_PER_GROUP).astype(F32) == grp)
    el = jnp.where(in_grp, lg, -jnp.inf)
    m1 = jnp.max(el, axis=-1, keepdims=True)
    i1 = jnp.min(jnp.where(el == m1, lanef, big), axis=-1, keepdims=True)
    el2 = jnp.where(lanef == i1, -jnp.inf, el)
    m2 = jnp.max(el2, axis=-1, keepdims=True)
    i2 = jnp.min(jnp.where(el2 == m2, lanef, big), axis=-1, keepdims=True)
    e2 = jnp.exp(m2 - m1)
    w1 = pg_top / (1.0 + e2)
    w2 = pg_top * e2 / (1.0 + e2)
    rw_o[rs, :] = jnp.where(lane == 0, w1, jnp.where(lane == 1, w2, 0.0))
    ri_o[rs, :] = jnp.where(lane == 0, i1, jnp.where(lane == 1, i2, 0.0)).astype(jnp.int32)


def _outproj(out_a, out_b, xf, mod3, w_top, w_bot, ln_g, ln_b, wr, br, seq):
    r, d = xf.shape
    tm = OUT_TILE
    assert seq % tm == 0 and tm % SUB_ROWS == 0
    tiles_per_seq = seq // tm
    row = lambda n: pl.BlockSpec((tm, n), lambda i: (i, 0))
    return pl.pallas_call(
        _outproj_kernel,
        grid=(r // tm,),
        in_specs=[row(out_a.shape[1]), row(out_b.shape[1]), row(d),
                  pl.BlockSpec((None, 6, d), lambda i: (i // tiles_per_seq, 0, 0)),
                  _resident(w_top.shape), _resident(w_bot.shape),
                  _resident((1, d)), _resident((1, d)), _resident(wr.shape), _resident(br.shape)],
        out_specs=[row(d), row(d), row(LANES), row(LANES)],
        out_shape=[jax.ShapeDtypeStruct((r, d), F32), jax.ShapeDtypeStruct((r, d), F32),
                   jax.ShapeDtypeStruct((r, LANES), F32), jax.ShapeDtypeStruct((r, LANES), jnp.int32)],
        compiler_params=_params(("parallel",)),
        name="out_proj_router",
    )(out_a, out_b, xf, mod3, w_top, w_bot, ln_g.reshape(1, d), ln_b.reshape(1, d), wr, br)


def _row_copy(src_ref, src_row, dst_ref, dst_row, sem):
    return pltpu.make_async_copy(src_ref.at[pl.ds(src_row, 1)], dst_ref.at[pl.ds(dst_row, 1)], sem)


def _dispatch_kernel(dest_ref, tail_ref, empty_ref, h_ref, buf_ref, zero_ref, hbuf, sem, zsem, lsem, rsem):
    tm = hbuf.shape[1]
    step = pl.program_id(0)

    def zero_block(b):
        row = pl.multiple_of(b * MOE_BLOCK, MOE_BLOCK)
        return pltpu.make_async_copy(zero_ref, buf_ref.at[pl.ds(row, MOE_BLOCK)], zsem)

    def for_empty_blocks(fn):
        def body(b, carry):
            @pl.when(empty_ref[b] != 0)
            def _():
                fn(b)
            return carry
        lax.fori_loop(0, empty_ref.shape[0], body, 0)

    @pl.when(step == 0)
    def _zero_fill():
        zero_ref[...] = jnp.zeros(zero_ref.shape, zero_ref.dtype)
        for e in range(N_EXPERTS):
            start = pl.multiple_of(tail_ref[e] // SUBLANES * SUBLANES, SUBLANES)
            pltpu.make_async_copy(zero_ref, buf_ref.at[pl.ds(start, MOE_BLOCK)], sem).start()
        for e in range(N_EXPERTS):
            pltpu.make_async_copy(zero_ref, buf_ref.at[pl.ds(0, MOE_BLOCK)], sem).wait()
        for_empty_blocks(lambda b: zero_block(b).start())

    @pl.when(step == pl.num_programs(0) - 1)
    def _zero_done():
        for_empty_blocks(lambda b: zero_block(b).wait())

    n_slots = hbuf.shape[0]
    slot = step % n_slots

    def load(tile, sl):
        row = pl.multiple_of(tile * tm, tm)
        return pltpu.make_async_copy(h_ref.at[pl.ds(row, tm)], hbuf.at[sl], lsem.at[sl])

    @pl.when(step == 0)
    def _():
        load(0, 0).start()

    @pl.when(step + 1 < pl.num_programs(0))
    def _():
        load(step + 1, (step + 1) % n_slots).start()

    load(step, slot).wait()

    def issue(i, carry):
        for k in range(TOP_K):
            _row_copy(hbuf.at[slot], i, buf_ref, dest_ref[0, 0, TOP_K * i + k], rsem.at[slot]).start()
        return carry

    lax.fori_loop(0, tm, issue, 0, unroll=DMA_UNROLL)

    def wait_tile(sl):
        for _ in range(TOP_K):
            pltpu.make_async_copy(hbuf.at[sl], buf_ref.at[pl.ds(0, tm)], rsem.at[sl]).wait()

    pl.when(step > 0)(lambda: wait_tile((step + n_slots - 1) % n_slots))
    pl.when(step == pl.num_programs(0) - 1)(lambda: wait_tile(slot))


def _dispatch(dest, tail_row, empty_block, h2):
    r, d = h2.shape
    tm = DISPATCH_TILE
    assert r % tm == 0
    dest3 = dest.reshape(r // tm, 1, TOP_K * tm)
    n_rows = empty_block.shape[0] * MOE_BLOCK
    return pl.pallas_call(
        _dispatch_kernel,
        grid=(r // tm,),
        in_specs=[pl.BlockSpec((1, 1, TOP_K * tm), lambda i: (i, 0, 0), memory_space=pltpu.SMEM),
                  pl.BlockSpec(memory_space=pltpu.SMEM),
                  pl.BlockSpec(memory_space=pltpu.SMEM),
                  pl.BlockSpec(memory_space=pl.ANY)],
        out_specs=pl.BlockSpec(memory_space=pl.ANY),
        out_shape=jax.ShapeDtypeStruct((n_rows, d), h2.dtype),
        scratch_shapes=[pltpu.VMEM((MOE_BLOCK, d), h2.dtype), pltpu.VMEM((3, tm, d), h2.dtype),
                        pltpu.SemaphoreType.DMA, pltpu.SemaphoreType.DMA, pltpu.SemaphoreType.DMA((3,)),
                        pltpu.SemaphoreType.DMA((3,))],
        compiler_params=_params(("arbitrary",)),
        name="moe_dispatch",
    )(dest3, tail_row, empty_block, h2)


def _expert_kernel(bq_ref, lo_ref, hi_ref, el_ref, nq_ref, ub_ref, x_ref, wg_hbm, wu_hbm, wd_hbm, o_ref,
                   wgb, wub, wdb, sg, su, sd, sems):
    b = pl.program_id(0)
    q = bq_ref[b]
    slot = q % 2
    mats = ((wg_hbm, sg, wgb), (wu_hbm, su, wub), (wd_hbm, sd, wdb))
    n_pieces = W_PIECES * len(mats)

    def piece_copy(m, qt, t):
        hbm, stage, _ = mats[m]
        pr = stage.shape[1]
        return pltpu.make_async_copy(hbm.at[el_ref[qt], pl.ds(t * pr, pr), :], stage.at[t % 2],
                                     sems.at[m, t % 2])

    def process(qt, lo, hi):
        for p in range(n_pieces):
            m, t = p % len(mats), p // len(mats)

            @pl.when((lo <= p) & (p < hi))
            def _():
                _, stage, resident = mats[m]
                pr = stage.shape[1]
                piece_copy(m, qt, t).wait()
                resident[qt % 2, pl.ds(t * pr, pr), :] = stage[t % 2].astype(BF16)
                q2 = qt + (t + 2) // W_PIECES

                @pl.when(q2 < nq_ref[0])
                def _():
                    piece_copy(m, q2, (t + 2) % W_PIECES).start()

    @pl.when(b == 0)
    def _first_expert():
        for m in range(len(mats)):
            for t in range(2):
                piece_copy(m, 0, t).start()
        process(0, 0, n_pieces)

    @pl.when(b < ub_ref[0])
    def _compute():
        xb = x_ref[...].astype(BF16)
        hid = wgb.shape[2]
        acts = []
        for h0 in range(0, hid, HID_TILE):
            hs = pl.ds(h0, HID_TILE)
            acts.append((_silu(_bdot(xb, wgb[slot, :, hs])) * _bdot(xb, wub[slot, :, hs])).astype(BF16))
        o_ref[...] = _bdot(jnp.concatenate(acts, axis=1), wdb[slot])

    @pl.when(b >= ub_ref[0])
    def _unused():
        o_ref[...] = jnp.zeros(o_ref.shape, o_ref.dtype)

    process(q + 1, lo_ref[b], hi_ref[b])


def _experts(plan, buf, wg, wu, wd, n_out_rows):
    d = buf.shape[1]
    hid = wg.shape[2]
    assert d % W_PIECES == 0 and hid % W_PIECES == 0 and W_PIECES % 2 == 0
    blk = lambda f: pl.BlockSpec((MOE_BLOCK, d), f)
    return pl.pallas_call(
        _expert_kernel,
        grid_spec=pltpu.PrefetchScalarGridSpec(
            num_scalar_prefetch=len(plan),
            grid=(n_out_rows // MOE_BLOCK,),
            in_specs=[blk(lambda b, bq, lo, hi, el, nq, ub: (jnp.minimum(b, ub[0] - 1), 0)),
                      pl.BlockSpec(memory_space=pl.ANY), pl.BlockSpec(memory_space=pl.ANY),
                      pl.BlockSpec(memory_space=pl.ANY)],
            out_specs=blk(lambda b, bq, lo, hi, el, nq, ub: (b, 0)),
            scratch_shapes=[pltpu.VMEM((2, d, hid), BF16), pltpu.VMEM((2, d, hid), BF16),
                            pltpu.VMEM((2, hid, d), BF16),
                            pltpu.VMEM((2, d // W_PIECES, hid), F32), pltpu.VMEM((2, d // W_PIECES, hid), F32),
                            pltpu.VMEM((2, hid // W_PIECES, d), F32),
                            pltpu.SemaphoreType.DMA((3, 2))],
        ),
        out_shape=jax.ShapeDtypeStruct((n_out_rows, d), F32),
        compiler_params=_params(("arbitrary",)),
        name="moe_experts",
    )(*plan, buf, wg, wu, wd)


def _combine_kernel(dest_ref, dnext_ref, y_ref, rw_ref, x1_ref, mod_ref, lng_ref, lnb_ref, o_ref,
                    g_even, g_odd, sem):
    tm = x1_ref.shape[0]
    i = pl.program_id(0)
    n_groups = tm // DMA_UNROLL
    has_next = i + 1 < pl.num_programs(0)

    def gather(d_ref, g_ref, sl):
        def body(j, carry):
            for r in range(DMA_UNROLL):
                row = j * DMA_UNROLL + r
                for k in range(TOP_K):
                    _row_copy(y_ref, d_ref[0, 0, TOP_K * row + k], g_ref.at[k], row, sem.at[sl]).start()
            return carry
        lax.fori_loop(0, n_groups, body, 0)

    @pl.when(i == 0)
    def _():
        gather(dest_ref, g_even, 0)

    def run(sl, g_cur, g_next, prefetch):
        if prefetch:
            gather(dnext_ref, g_next, 1 - sl)
        for k in range(TOP_K):
            pltpu.make_async_copy(y_ref.at[pl.ds(0, tm)], g_cur.at[k], sem.at[sl]).wait()
        f = rw_ref[:, 0:1] * g_cur[0] + rw_ref[:, 1:2] * g_cur[1]
        o_ref[...] = _ln(ALPHA * x1_ref[...] + mod_ref[5:6, :] * f) * lng_ref[...] + lnb_ref[...]

    for sl, g_cur, g_next in ((0, g_even, g_odd), (1, g_odd, g_even)):
        for prefetch in (True, False):
            pl.when((i % 2 == sl) & (has_next == prefetch))(functools.partial(run, sl, g_cur, g_next, prefetch))


def _combine(dest, y, rw, x1, mod3, ln_g, ln_b, seq):
    r, d = x1.shape
    tm = COMBINE_TILE
    assert seq % tm == 0
    tiles_per_seq = seq // tm
    dest3 = dest.reshape(r // tm, 1, TOP_K * tm)
    n = r // tm
    row = lambda w: pl.BlockSpec((tm, w), lambda i: (i, 0))
    return pl.pallas_call(
        _combine_kernel,
        grid=(n,),
        in_specs=[pl.BlockSpec((1, 1, TOP_K * tm), lambda i: (i, 0, 0), memory_space=pltpu.SMEM),
                  pl.BlockSpec((1, 1, TOP_K * tm), lambda i: (jnp.minimum(i + 1, n - 1), 0, 0),
                               memory_space=pltpu.SMEM),
                  pl.BlockSpec(memory_space=pl.ANY),
                  row(LANES), row(d),
                  pl.BlockSpec((None, 6, d), lambda i: (i // tiles_per_seq, 0, 0)),
                  _resident((1, d)), _resident((1, d))],
        out_specs=row(d),
        out_shape=jax.ShapeDtypeStruct((r, d), F32),
        scratch_shapes=[pltpu.VMEM((TOP_K, tm, d), F32), pltpu.VMEM((TOP_K, tm, d), F32),
                        pltpu.SemaphoreType.DMA((2,))],
        compiler_params=_params(("arbitrary",)),
        name="moe_combine_ln",
    )(dest3, dest3, y, rw, x1, mod3, ln_g.reshape(1, d), ln_b.reshape(1, d))


def _routing_plan(ri, n_tokens):
    eid = ri[:, :TOP_K].reshape(-1)
    m = n_tokens * TOP_K
    onehot = (eid[:, None] == jnp.arange(N_EXPERTS, dtype=jnp.int32)[None, :]).astype(jnp.int32)
    csum = jnp.cumsum(onehot, axis=0)
    counts = csum[-1]
    rank = jnp.sum(csum * onehot, axis=1) - 1
    padded = (counts + MOE_BLOCK - 1) // MOE_BLOCK * MOE_BLOCK
    pad_end = jnp.cumsum(padded)
    pad_start = pad_end - padded
    dest = jnp.sum(onehot * pad_start[None, :], axis=1) + rank
    tail = pad_start + counts
    n_blocks = m // MOE_BLOCK + N_EXPERTS
    used_blocks = pad_end[-1] // MOE_BLOCK
    empty = jnp.arange(n_blocks + 1, dtype=jnp.int32) >= used_blocks

    n_pieces = 3 * W_PIECES
    has = counts > 0
    ordinal = jnp.cumsum(has.astype(jnp.int32)) - 1
    n_ord = ordinal[-1] + 1
    e_ids = jnp.arange(N_EXPERTS, dtype=jnp.int32)
    ord_expert = jnp.sum(jnp.where(has[None, :] & (ordinal[None, :] == e_ids[:, None]), e_ids[None, :], 0), axis=1)
    bid = jnp.arange(n_blocks, dtype=jnp.int32)
    brow = bid[:, None] * MOE_BLOCK
    in_e = ((pad_start[None, :] <= brow) & (brow < pad_end[None, :])).astype(jnp.int32)
    pick = lambda v: jnp.sum(in_e * v[None, :], axis=1)
    used = bid < used_blocks
    blk_q = jnp.where(used, pick(ordinal), n_ord - 1)
    i_in_e = bid - pick(pad_start) // MOE_BLOCK
    k_e = jnp.maximum(pick(padded) // MOE_BLOCK, 1)
    brings = used & (blk_q + 1 < n_ord)
    lo = jnp.where(brings, n_pieces * i_in_e // k_e, 0)
    hi = jnp.where(brings, n_pieces * (i_in_e + 1) // k_e, 0)
    i32 = lambda a: a.astype(jnp.int32)
    plan = (i32(blk_q), i32(lo), i32(hi), i32(ord_expert), i32(n_ord.reshape(1)), i32(used_blocks.reshape(1)))
    return i32(dest), i32(tail), i32(empty), plan, n_blocks * MOE_BLOCK


def _rope_tables(seq):
    n_freq = HEAD_DIM // 4
    inv_freq = ROPE_THETA ** (-jnp.arange(n_freq, dtype=F32) / n_freq)
    rows = seq // GRID_W
    ar = jnp.arange(rows, dtype=F32)[:, None] * inv_freq
    ac = jnp.arange(GRID_W, dtype=F32)[:, None] * inv_freq
    by_row = lambda t: jnp.broadcast_to(t[:, None, :], (rows, GRID_W, n_freq)).reshape(seq, n_freq)
    by_col = lambda t: jnp.broadcast_to(t[None, :, :], (rows, GRID_W, n_freq)).reshape(seq, n_freq)
    cos_r, sin_r, cos_c, sin_c = by_row(jnp.cos(ar)), by_row(jnp.sin(ar)), by_col(jnp.cos(ac)), by_col(jnp.sin(ac))
    zero = jnp.zeros_like(cos_r)
    cos_t = jnp.concatenate([cos_r, cos_r, cos_c, cos_c], axis=1)
    sin_a = jnp.concatenate([-sin_r, zero, -sin_c, zero], axis=1)
    sin_b = jnp.concatenate([zero, sin_r, zero, sin_c], axis=1)
    return cos_t, sin_a, sin_b


def kernel(x, c, ctx, c_ctx, w_ada, b_ada, w_in, w_gate_up, b_gate, attn_sink, gla_norm_w, w_out, ln1_g, ln1_b, w_router_group, b_router_group, w_router_expert, b_router_expert, w_exp_gate, w_exp_up, w_exp_down, ln2_g, ln2_b):
    batch, seq, d = x.shape
    n_ctx = ctx.shape[1]
    assert w_ada.shape[0] == DEPTH and batch < MOD_ROWS
    assert seq % GLA_STEP == 0 and n_ctx == GLA_STEP
    n_tok = batch * seq
    a_width = d // 2
    kv_width = a_width // A_GROUP
    b_width = d - a_width
    key_width = b_width // 2
    layer = 0

    cc = jnp.concatenate([c, c_ctx[None, :], jnp.zeros((MOD_ROWS - batch - 1, d), F32)], axis=0)
    mod3 = _adaln(cc, w_ada[layer], b_ada[layer]).reshape(MOD_ROWS, 6, d)

    splits = (a_width, kv_width, kv_width, key_width, key_width, b_width, b_width, 2 * GATE_RANK)
    w_in_b = w_in[layer].astype(BF16)
    zero_up = jnp.zeros((GATE_RANK, key_width), F32)
    wup2 = jnp.concatenate([jnp.concatenate([w_gate_up[layer, 0], zero_up], axis=1),
                            jnp.concatenate([zero_up, w_gate_up[layer, 1]], axis=1)], axis=0).astype(BF16)
    bg2 = b_gate[layer].reshape(1, 2 * key_width)
    tables = _rope_tables(seq)

    xf = x.reshape(n_tok, d)
    qa, ka, va, qb, kb, vb, rb, la_f, la_b = _project(
        xf, mod3, lambda i, per_seq: i // per_seq, tables, w_in_b, splits, wup2, bg2, rope=True, seq=seq)
    _, ka_c, va_c, _, kb_c, vb_c, _, lac_f, lac_b = _project(
        ctx.reshape(batch * n_ctx, d), mod3, lambda i, per_seq: batch, tables, w_in_b, splits, wup2, bg2,
        rope=False, seq=seq, keys_values_only=True)

    out_a = _attention(attn_sink[layer], qa, ka, va, ka_c, va_c, batch, seq, n_ctx)
    o_b = _gla(kb_c, vb_c, lac_b, qb, kb, vb, la_b, batch, seq, n_ctx, reverse=True)
    out_b = _gla(kb_c, vb_c, lac_f, qb, kb, vb, la_f, batch, seq, n_ctx, reverse=False,
                 extra=(o_b, rb, gla_norm_w[layer].reshape(1, b_width)))

    w_out_b = w_out[layer].astype(BF16)
    wr = jnp.concatenate([w_router_expert[layer], w_router_group[layer],
                          jnp.zeros((d, LANES - N_EXPERTS - N_GROUPS), F32)], axis=1).astype(BF16)
    br = jnp.concatenate([b_router_expert[layer], b_router_group[layer],
                          jnp.zeros((LANES - N_EXPERTS - N_GROUPS,), F32)]).reshape(1, LANES)
    x1, h2, rw, ri = _outproj(out_a, out_b, xf, mod3, w_out_b[:a_width], w_out_b[a_width:],
                              ln1_g[layer], ln1_b[layer], wr, br, seq)

    dest, tail_row, empty_block, plan, n_buf_rows = _routing_plan(ri, n_tok)
    buf = _dispatch(dest, tail_row, empty_block, h2)
    y = _experts(plan, buf, w_exp_gate[layer], w_exp_up[layer], w_exp_down[layer], n_buf_rows)
    out = _combine(dest, y, rw, x1, mod3, ln2_g[layer], ln2_b[layer], seq)
    return out.reshape(batch, seq, d)
```

```python
import functools

import jax
import jax.numpy as jnp
from jax import lax
from jax.experimental import pallas as pl
from jax.experimental.pallas import tpu as pltpu

F32 = jnp.float32
BF16 = jnp.bfloat16

HEAD_DIM = 128
GRID_W = 64
WINDOW = 128
A_BLOCK = 128
A_GROUP = 4
ROPE_THETA = 10000.0
B_HEADS = 4
GATE_RANK = 16
GATE_TAU = 16.0
GLA_CHUNK = 64
N_GROUPS = 4
EXPERTS_PER_GROUP = 8
N_EXPERTS = N_GROUPS * EXPERTS_PER_GROUP
TOP_K = 2
DEPTH = 1
ALPHA = (2.0 * DEPTH) ** 0.25
LN_EPS = 1e-6
LOG2_E = 1.4426950408889634
ATTN_EXP2_SCALE = HEAD_DIM ** -0.5 * LOG2_E

LANES = 128
SUBLANES = 8
MOD_ROWS = 8
VMEM_LIMIT = 56 * 1024 * 1024

DISPATCH_TILE = 1024
ADALN_STEPS = 16
COMBINE_TILE = 256
OUT_TILE = 512
SUB_ROWS = 256
PROJ_COLS = 1024
Q_BLOCKS = 8
GLA_STEP = 256
MOE_BLOCK = 256
HID_TILE = 512
W_PIECES = 4
DMA_UNROLL = 8


def _params(sem):
    return pltpu.CompilerParams(dimension_semantics=sem, vmem_limit_bytes=VMEM_LIMIT)


def _resident(shape):
    nd = len(shape)
    return pl.BlockSpec(shape, lambda *_: (0,) * nd, pipeline_mode=pl.Buffered(1))


def _ln(x):
    mu = jnp.mean(x, axis=-1, keepdims=True)
    xc = x - mu
    var = jnp.mean(xc * xc, axis=-1, keepdims=True)
    return xc * lax.rsqrt(var + LN_EPS)


def _silu(x):
    return x * jax.nn.sigmoid(x)


def _bdot(a, b):
    return jnp.dot(a, b, preferred_element_type=F32)


def _bdot_nt(a, b):
    return lax.dot_general(a, b, (((1,), (1,)), ((), ())), preferred_element_type=F32)


def _adaln_kernel(c_ref, w_ref, b_ref, win_ref, wout_ref, o_ref, win_o, wout_o):
    s = _silu(c_ref[...]).astype(BF16)
    o_ref[...] = _bdot(s, w_ref[...].astype(BF16)) + b_ref[...]
    win_o[...] = win_ref[...].astype(BF16)
    wout_o[...] = wout_ref[...].astype(BF16)


def _adaln(cc, w_ada, b_ada, w_in, w_out):
    d, n = w_ada.shape
    steps = ADALN_STEPS
    assert n % steps == 0 and w_in.shape[0] % steps == 0 and w_out.shape[0] % steps == 0
    tn, t_in, t_out = n // steps, w_in.shape[0] // steps, w_out.shape[0] // steps
    rows = lambda t, w: pl.BlockSpec((t, w), lambda j: (j, 0))
    return pl.pallas_call(
        _adaln_kernel,
        grid=(steps,),
        in_specs=[pl.BlockSpec((MOD_ROWS, d), lambda j: (0, 0)),
                  pl.BlockSpec((d, tn), lambda j: (0, j)),
                  pl.BlockSpec((1, tn), lambda j: (0, j)),
                  rows(t_in, w_in.shape[1]), rows(t_out, w_out.shape[1])],
        out_specs=[pl.BlockSpec((MOD_ROWS, tn), lambda j: (0, j)),
                   rows(t_in, w_in.shape[1]), rows(t_out, w_out.shape[1])],
        out_shape=[jax.ShapeDtypeStruct((MOD_ROWS, n), F32), jax.ShapeDtypeStruct(w_in.shape, BF16),
                   jax.ShapeDtypeStruct(w_out.shape, BF16)],
        compiler_params=_params(("arbitrary",)),
        name="adaln",
    )(cc, w_ada, b_ada.reshape(1, n), w_in, w_out)


def _proj_kernel(x_ref, mod_ref, cos_ref, sina_ref, sinb_ref, w_ref, wup, bg, *out_refs, groups, rope):
    for r0 in range(0, x_ref.shape[0], SUB_ROWS):
        _proj_rows(slice(r0, r0 + SUB_ROWS), x_ref, mod_ref, cos_ref, sina_ref, sinb_ref, w_ref, wup, bg,
                   out_refs, groups, rope)


def _proj_rows(rs, x_ref, mod_ref, cos_ref, sina_ref, sinb_ref, w_ref, wup, bg, out_refs, groups, rope):
    h = _ln(x_ref[rs, :]) * (1.0 + mod_ref[1:2, :]) + mod_ref[0:1, :]
    hb = h.astype(BF16)

    def rot(t):
        return (t * cos_ref[rs, :] + pltpu.roll(t, 96, 1) * sina_ref[rs, :]
                + pltpu.roll(t, 32, 1) * sinb_ref[rs, :])

    col = 0
    refs = iter(out_refs)
    for n, wanted, rotary, post in groups:
        if wanted:
            o_ref = next(refs)
            step = min(n, PROJ_COLS)
            for c0 in range(0, n, step):
                t = _bdot(hb, w_ref[:, col + c0:col + c0 + step])
                if rotary and rope:
                    for l0 in range(0, step, LANES):
                        r = rot(t[:, l0:l0 + LANES])
                        o_ref[rs, c0 + l0:c0 + l0 + LANES] = (r if post == 1.0 else r * post).astype(o_ref.dtype)
                else:
                    o_ref[rs, c0:c0 + step] = t.astype(o_ref.dtype)
        col += n
    laf_o, lab_o = refs
    gl = _bdot(hb, w_ref[:, col:]).astype(BF16)
    z = _bdot(gl, wup[...]) + bg[...]
    la = (jnp.minimum(z, 0.0) - jnp.log1p(jnp.exp(-jnp.abs(z)))) / GATE_TAU
    kw = laf_o.shape[1]
    laf_o[rs, :] = la[:, :kw]
    lab_o[rs, :] = la[:, kw:]


def _project(xf, mod3, mod_row, tables, w_in_b, splits, wup2, bg2, *, rope, seq, keys_values_only=False):
    r, d = xf.shape
    tm = OUT_TILE
    assert r % tm == 0 and seq % tm == 0 and tm % SUB_ROWS == 0
    tiles_per_seq = seq // tm
    mod_row_of_tile = lambda i: mod_row(i, tiles_per_seq)
    n_gate = splits[-1]
    assert sum(splits) == w_in_b.shape[1] and (sum(splits) - n_gate) % LANES == 0
    kv = keys_values_only
    spec = ((BF16, not kv, True, ATTN_EXP2_SCALE), (BF16, True, True, 1.0), (BF16, True, False, 1.0),
            (BF16, not kv, False, 1.0), (BF16, True, False, 1.0), (BF16, True, False, 1.0),
            (BF16, not kv, False, 1.0))
    groups = tuple((n, wanted, rotary, post) for n, (_, wanted, rotary, post) in zip(splits, spec))
    key_width = splits[3]
    outs = [(n, dt) for n, (dt, wanted, _, _) in zip(splits, spec) if wanted] + [(key_width, F32)] * 2
    cos_t, sina_t, sinb_t = tables
    row = lambda n: pl.BlockSpec((tm, n), lambda i: (i, 0))
    tab = pl.BlockSpec((tm, LANES), lambda i: (i % tiles_per_seq, 0))
    res = pl.pallas_call(
        functools.partial(_proj_kernel, groups=groups, rope=rope),
        grid=(r // tm,),
        in_specs=[row(d),
                  pl.BlockSpec((None, 6, d), lambda i: (mod_row_of_tile(i), 0, 0)),
                  tab, tab, tab,
                  _resident(w_in_b.shape), _resident(wup2.shape), _resident(bg2.shape)],
        out_specs=[row(n) for n, _ in outs],
        out_shape=[jax.ShapeDtypeStruct((r, n), dt) for n, dt in outs],
        compiler_params=_params(("parallel",)),
        name="in_proj_rope" if rope else "in_proj_ctx",
    )(xf, mod3, cos_t, sina_t, sinb_t, w_in_b, wup2, bg2)
    res = iter(res)
    return tuple(next(res) if wanted else None for _, wanted, _, _ in spec) + tuple(res)


def _attn_kernel(sink_ref, q_ref, *refs):
    n_kv = Q_BLOCKS + 2
    k_refs, v_refs = refs[:n_kv], refs[n_kv:2 * n_kv]
    kx_ref, vx_ref, o_ref = refs[2 * n_kv:]
    n = pl.program_id(1)
    last = pl.num_programs(1) - 1
    blk = A_BLOCK
    rows = A_GROUP * blk
    n_ctx = kx_ref.shape[0]
    qi = lax.broadcasted_iota(jnp.int32, (rows, blk), 0) % blk
    kj = lax.broadcasted_iota(jnp.int32, (rows, blk), 1)
    for u in range(Q_BLOCKS):
        ok_prev = (kj >= qi) & (n > 0) if u == 0 else (kj >= qi)
        ok_next = (kj <= qi) & (n < last) if u == Q_BLOCKS - 1 else (kj <= qi)
        qs = slice(u * blk, (u + 1) * blk)
        kp_ref, kc_ref, kn_ref = k_refs[u:u + 3]
        vp_ref, vc_ref, vn_ref = v_refs[u:u + 3]
        for hk in range(kc_ref.shape[1] // HEAD_DIM):
            ks = slice(hk * HEAD_DIM, (hk + 1) * HEAD_DIM)
            q4 = jnp.concatenate(
                [q_ref[qs, (hk * A_GROUP + g) * HEAD_DIM:(hk * A_GROUP + g + 1) * HEAD_DIM]
                 for g in range(A_GROUP)], axis=0)
            s_ctx = _bdot_nt(q4, kx_ref[:, ks])
            s_prev = jnp.where(ok_prev, _bdot_nt(q4, kp_ref[:, ks]), -jnp.inf)
            s_cur = _bdot_nt(q4, kc_ref[:, ks])
            s_next = jnp.where(ok_next, _bdot_nt(q4, kn_ref[:, ks]), -jnp.inf)
            sink = jnp.concatenate(
                [jnp.full((blk, 1), sink_ref[hk * A_GROUP + g] * LOG2_E, F32) for g in range(A_GROUP)], axis=0)
            m_lanes = jnp.maximum(jnp.maximum(s_prev, s_cur), s_next)
            for l0 in range(0, n_ctx, LANES):
                m_lanes = jnp.maximum(m_lanes, s_ctx[:, l0:l0 + LANES])
            m = jnp.maximum(jnp.max(m_lanes, axis=-1, keepdims=True), sink)
            e_ctx = jnp.exp2(s_ctx - m)
            e_prev = jnp.exp2(s_prev - m)
            e_cur = jnp.exp2(s_cur - m)
            e_next = jnp.exp2(s_next - m)
            e_lanes = e_prev + e_cur + e_next
            for l0 in range(0, n_ctx, LANES):
                e_lanes = e_lanes + e_ctx[:, l0:l0 + LANES]
            den = jnp.exp2(sink - m) + jnp.sum(e_lanes, axis=-1, keepdims=True)
            o = (_bdot(e_ctx.astype(BF16), vx_ref[:, ks])
                 + (_bdot(e_prev.astype(BF16), vp_ref[:, ks])
                    + _bdot(e_cur.astype(BF16), vc_ref[:, ks])
                    + _bdot(e_next.astype(BF16), vn_ref[:, ks]))) * (1.0 / den)
            for g in range(A_GROUP):
                hq = hk * A_GROUP + g
                o_ref[qs, hq * HEAD_DIM:(hq + 1) * HEAD_DIM] = o[g * blk:(g + 1) * blk, :].astype(o_ref.dtype)


def _attention(sink, qa, ka, va, ka_c, va_c, batch, seq, n_ctx):
    blk = A_BLOCK
    nb = seq // blk
    assert nb % Q_BLOCKS == 0
    steps = nb // Q_BLOCKS
    aw = qa.shape[1]
    kvw = ka.shape[1]
    kv = lambda j: pl.BlockSpec(
        (blk, kvw), lambda b, n: (b * nb + jnp.clip(n * Q_BLOCKS - 1 + j, 0, nb - 1), 0))
    kv_specs = [kv(j) for j in range(Q_BLOCKS + 2)]
    qo = pl.BlockSpec((Q_BLOCKS * blk, aw), lambda b, n: (b * steps + n, 0))
    ctx = pl.BlockSpec((n_ctx, kvw), lambda b, n: (b, 0))
    return pl.pallas_call(
        _attn_kernel,
        grid=(batch, steps),
        in_specs=[pl.BlockSpec(memory_space=pltpu.SMEM), qo] + kv_specs + kv_specs + [ctx, ctx],
        out_specs=qo,
        out_shape=jax.ShapeDtypeStruct((batch * seq, aw), BF16),
        compiler_params=_params(("parallel", "parallel")),
        name="window_gqa",
    )(sink, qa, *([ka] * (Q_BLOCKS + 2)), *([va] * (Q_BLOCKS + 2)), ka_c, va_c)


def _chunk_cumsum(g, *, reverse):
    rows = g.shape[0]
    p = lax.broadcasted_iota(jnp.int32, g.shape, 0) % GLA_CHUNK
    s = 1
    while s < GLA_CHUNK:
        if reverse:
            g = g + jnp.where(p < GLA_CHUNK - s, pltpu.roll(g, rows - s, 0), 0.0)
        else:
            g = g + jnp.where(p >= s, pltpu.roll(g, s, 0), 0.0)
        s *= 2
    return g


def _per_chunk_row(x, i):
    c = GLA_CHUNK
    return jnp.concatenate(
        [jnp.broadcast_to(x[j * c + i:j * c + i + 1, :], (c, x.shape[1])) for j in range(x.shape[0] // c)], axis=0)


def _gla_block(q, k, v, cum, state_t, mask, *, reverse, need_o):
    c = GLA_CHUNK
    rows, dk = k.shape
    n_chunks = rows // c
    k = k.astype(F32)
    i_last = 0 if reverse else c - 1
    i_mid = c // 2 if reverse else c // 2 - 1
    b_last = _per_chunk_row(cum, i_last)
    kdec = (k * jnp.exp(b_last - cum)).astype(BF16)
    if need_o:
        b_mid = _per_chunk_row(cum, i_mid)
        qc = q.astype(F32) * dk ** -0.5
        qm = (qc * jnp.exp(cum - b_mid)).astype(BF16)
        km = (k * jnp.exp(b_mid - cum)).astype(BF16)
        a = jnp.where(mask, _bdot_nt(qm, km), 0.0)
        o_intra = _bdot(a.astype(BF16), v)
        qe = (qc * jnp.exp(cum)).astype(BF16)
    o_inter = [None] * n_chunks
    for j in (reversed(range(n_chunks)) if reverse else range(n_chunks)):
        rs = slice(j * c, (j + 1) * c)
        if need_o:
            o_inter[j] = _bdot_nt(qe[rs, :], state_t.astype(BF16))
        kv_t = lax.dot_general(v[rs, :], kdec[rs, :], (((0,), (0,)), ((), ())), preferred_element_type=F32)
        state_t = state_t * jnp.exp(cum[j * c + i_last:j * c + i_last + 1, :]) + kv_t
    o = o_intra + jnp.concatenate(o_inter, axis=0) if need_o else None
    return o, state_t


def _gla_kernel(*refs, reverse, final):
    if final:
        (kx_ref, vx_ref, gx_ref, q_ref, k_ref, v_ref, g_ref, ob_ref, r_ref, nw_ref, o_ref, st_ref) = refs
    else:
        (kx_ref, vx_ref, gx_ref, q_ref, k_ref, v_ref, g_ref, o_ref, st_ref) = refs
    t = pl.program_id(1)
    n_heads = st_ref.shape[0]
    dv, dk = st_ref.shape[1:]

    @pl.when(t == 0)
    def _context():
        cum = _chunk_cumsum(gx_ref[...], reverse=reverse)
        for h in range(n_heads):
            ks, vs = slice(h * dk, (h + 1) * dk), slice(h * dv, (h + 1) * dv)
            _, st = _gla_block(None, kx_ref[:, ks], vx_ref[:, vs], cum[:, ks], jnp.zeros((dv, dk), F32), None,
                               reverse=reverse, need_o=False)
            st_ref[h] = st

    @pl.when(t > 0)
    def _latent():
        rows = k_ref.shape[0]
        cum = _chunk_cumsum(g_ref[...], reverse=reverse)
        r = lax.broadcasted_iota(jnp.int32, (rows, rows), 0)
        s = lax.broadcasted_iota(jnp.int32, (rows, rows), 1)
        causal = (s >= r) if reverse else (s <= r)
        mask = causal & ((r // GLA_CHUNK) == (s // GLA_CHUNK))
        for h in range(n_heads):
            ks, vs = slice(h * dk, (h + 1) * dk), slice(h * dv, (h + 1) * dv)
            o, st = _gla_block(q_ref[:, ks], k_ref[:, ks], v_ref[:, vs], cum[:, ks], st_ref[h], mask,
                               reverse=reverse, need_o=True)
            st_ref[h] = st
            if final:
                o = o + ob_ref[:, vs].astype(F32)
                o = o * lax.rsqrt(jnp.mean(o * o, axis=-1, keepdims=True) + LN_EPS)
                o = o * nw_ref[:, vs]
                o = o * _silu(r_ref[:, vs].astype(F32))
            o_ref[:, vs] = o.astype(o_ref.dtype)


def _gla(kb_c, vb_c, la_c, qb, kb, vb, la, batch, seq, n_ctx, *, reverse, extra=None):
    assert n_ctx % GLA_CHUNK == 0 and GLA_STEP % GLA_CHUNK == 0
    tb = GLA_STEP
    nt = seq // tb
    kw, vw = qb.shape[1], vb.shape[1]
    final = extra is not None

    def lat(b, t):
        i = jnp.maximum(t - 1, 0)
        if reverse:
            i = nt - 1 - i
        return (b * nt + i, 0)

    cx = lambda b, t: (b, 0)
    in_specs = [pl.BlockSpec((n_ctx, kw), cx), pl.BlockSpec((n_ctx, vw), cx), pl.BlockSpec((n_ctx, kw), cx),
                pl.BlockSpec((tb, kw), lat), pl.BlockSpec((tb, kw), lat),
                pl.BlockSpec((tb, vw), lat), pl.BlockSpec((tb, kw), lat)]
    args = [kb_c, vb_c, la_c, qb, kb, vb, la]
    if final:
        o_other, rb, norm_w = extra
        in_specs += [pl.BlockSpec((tb, vw), lat), pl.BlockSpec((tb, vw), lat), _resident((1, vw))]
        args += [o_other, rb, norm_w]
    return pl.pallas_call(
        functools.partial(_gla_kernel, reverse=reverse, final=final),
        grid=(batch, nt + 1),
        in_specs=in_specs,
        out_specs=pl.BlockSpec((tb, vw), lat),
        out_shape=jax.ShapeDtypeStruct((batch * seq, vw), BF16),
        scratch_shapes=[pltpu.VMEM((B_HEADS, vw // B_HEADS, kw // B_HEADS), F32)],
        compiler_params=_params(("parallel", "arbitrary")),
        name="gla_fwd_out" if final else "gla_bwd",
    )(*args)


def _outproj_kernel(oa_ref, ob_ref, x_ref, mod_ref, wt_ref, wb_ref, lng_ref, lnb_ref, wr_ref, br_ref,
                    x1_o, h2_o, rw_o, ri_o):
    for r0 in range(0, x_ref.shape[0], SUB_ROWS):
        _outproj_rows(slice(r0, r0 + SUB_ROWS), oa_ref, ob_ref, x_ref, mod_ref, wt_ref, wb_ref, lng_ref, lnb_ref,
                      wr_ref, br_ref, x1_o, h2_o, rw_o, ri_o)


def _outproj_rows(rs, oa_ref, ob_ref, x_ref, mod_ref, wt_ref, wb_ref, lng_ref, lnb_ref, wr_ref, br_ref,
                  x1_o, h2_o, rw_o, ri_o):
    y = _bdot(oa_ref[rs, :], wt_ref[...]) + _bdot(ob_ref[rs, :], wb_ref[...])
    x1 = _ln(ALPHA * x_ref[rs, :] + mod_ref[2:3, :] * y) * lng_ref[...] + lnb_ref[...]
    x1_o[rs, :] = x1
    h2 = _ln(x1) * (1.0 + mod_ref[4:5, :]) + mod_ref[3:4, :]
    h2_o[rs, :] = h2
    lg = _bdot(h2.astype(BF16), wr_ref[...]) + br_ref[...]
    lane = lax.broadcasted_iota(jnp.int32, lg.shape, 1)
    lanef = lane.astype(F32)
    big = float(LANES)
    is_g = (lane >= N_EXPERTS) & (lane < N_EXPERTS + N_GROUPS)
    gl = jnp.where(is_g, lg, -jnp.inf)
    gmax = jnp.max(gl, axis=-1, keepdims=True)
    pg_top = 1.0 / jnp.sum(jnp.exp(gl - gmax), axis=-1, keepdims=True)
    grp = jnp.min(jnp.where(gl == gmax, lanef, big), axis=-1, keepdims=True) - N_EXPERTS
    in_grp = (lane < N_EXPERTS) & ((lane // EXPERTS_PER_GROUP).astype(F32) == grp)
    el = jnp.where(in_grp, lg, -jnp.inf)
    m1 = jnp.max(el, axis=-1, keepdims=True)
    i1 = jnp.min(jnp.where(el == m1, lanef, big), axis=-1, keepdims=True)
    el2 = jnp.where(lanef == i1, -jnp.inf, el)
    m2 = jnp.max(el2, axis=-1, keepdims=True)
    i2 = jnp.min(jnp.where(el2 == m2, lanef, big), axis=-1, keepdims=True)
    e2 = jnp.exp(m2 - m1)
    w1 = pg_top / (1.0 + e2)
    w2 = pg_top * e2 / (1.0 + e2)
    rw_o[rs, :] = jnp.where(lane == 0, w1, jnp.where(lane == 1, w2, 0.0))
    ri_o[rs, :] = jnp.where(lane == 0, i1, jnp.where(lane == 1, i2, 0.0)).astype(jnp.int32)


def _outproj(out_a, out_b, xf, mod3, w_top, w_bot, ln_g, ln_b, wr, br, seq):
    r, d = xf.shape
    tm = OUT_TILE
    assert seq % tm == 0 and tm % SUB_ROWS == 0
    tiles_per_seq = seq // tm
    row = lambda n: pl.BlockSpec((tm, n), lambda i: (i, 0))
    return pl.pallas_call(
        _outproj_kernel,
        grid=(r // tm,),
        in_specs=[row(out_a.shape[1]), row(out_b.shape[1]), row(d),
                  pl.BlockSpec((None, 6, d), lambda i: (i // tiles_per_seq, 0, 0)),
                  _resident(w_top.shape), _resident(w_bot.shape),
                  _resident((1, d)), _resident((1, d)), _resident(wr.shape), _resident(br.shape)],
        out_specs=[row(d), row(d), row(LANES), row(LANES)],
        out_shape=[jax.ShapeDtypeStruct((r, d), F32), jax.ShapeDtypeStruct((r, d), F32),
                   jax.ShapeDtypeStruct((r, LANES), F32), jax.ShapeDtypeStruct((r, LANES), jnp.int32)],
        compiler_params=_params(("parallel",)),
        name="out_proj_router",
    )(out_a, out_b, xf, mod3, w_top, w_bot, ln_g.reshape(1, d), ln_b.reshape(1, d), wr, br)


def _row_copy(src_ref, src_row, dst_ref, dst_row, sem):
    return pltpu.make_async_copy(src_ref.at[pl.ds(src_row, 1)], dst_ref.at[pl.ds(dst_row, 1)], sem)


def _dispatch_kernel(dest_ref, tail_ref, empty_ref, h_ref, buf_ref, zero_ref, hbuf, sem, zsem, lsem, rsem):
    tm = hbuf.shape[1]
    step = pl.program_id(0)

    def zero_block(b):
        row = pl.multiple_of(b * MOE_BLOCK, MOE_BLOCK)
        return pltpu.make_async_copy(zero_ref, buf_ref.at[pl.ds(row, MOE_BLOCK)], zsem)

    def for_empty_blocks(fn):
        def body(b, carry):
            @pl.when(empty_ref[b] != 0)
            def _():
                fn(b)
            return carry
        lax.fori_loop(0, empty_ref.shape[0], body, 0)

    @pl.when(step == 0)
    def _zero_fill():
        zero_ref[...] = jnp.zeros(zero_ref.shape, zero_ref.dtype)
        for e in range(N_EXPERTS):
            start = pl.multiple_of(tail_ref[e] // SUBLANES * SUBLANES, SUBLANES)
            pltpu.make_async_copy(zero_ref, buf_ref.at[pl.ds(start, MOE_BLOCK)], sem).start()
        for e in range(N_EXPERTS):
            pltpu.make_async_copy(zero_ref, buf_ref.at[pl.ds(0, MOE_BLOCK)], sem).wait()
        for_empty_blocks(lambda b: zero_block(b).start())

    @pl.when(step == pl.num_programs(0) - 1)
    def _zero_done():
        for_empty_blocks(lambda b: zero_block(b).wait())

    n_slots = hbuf.shape[0]
    slot = step % n_slots

    def load(tile, sl):
        row = pl.multiple_of(tile * tm, tm)
        return pltpu.make_async_copy(h_ref.at[pl.ds(row, tm)], hbuf.at[sl], lsem.at[sl])

    @pl.when(step == 0)
    def _():
        load(0, 0).start()

    @pl.when(step + 1 < pl.num_programs(0))
    def _():
        load(step + 1, (step + 1) % n_slots).start()

    load(step, slot).wait()

    def issue(i, carry):
        for k in range(TOP_K):
            _row_copy(hbuf.at[slot], i, buf_ref, dest_ref[0, 0, TOP_K * i + k], rsem.at[slot]).start()
        return carry

    lax.fori_loop(0, tm, issue, 0, unroll=DMA_UNROLL)

    def wait_tile(sl):
        for _ in range(TOP_K):
            pltpu.make_async_copy(hbuf.at[sl], buf_ref.at[pl.ds(0, tm)], rsem.at[sl]).wait()

    pl.when(step > 0)(lambda: wait_tile((step + n_slots - 1) % n_slots))
    pl.when(step == pl.num_programs(0) - 1)(lambda: wait_tile(slot))


def _dispatch(dest, tail_row, empty_block, h2):
    r, d = h2.shape
    tm = DISPATCH_TILE
    assert r % tm == 0
    dest3 = dest.reshape(r // tm, 1, TOP_K * tm)
    n_rows = empty_block.shape[0] * MOE_BLOCK
    return pl.pallas_call(
        _dispatch_kernel,
        grid=(r // tm,),
        in_specs=[pl.BlockSpec((1, 1, TOP_K * tm), lambda i: (i, 0, 0), memory_space=pltpu.SMEM),
                  pl.BlockSpec(memory_space=pltpu.SMEM),
                  pl.BlockSpec(memory_space=pltpu.SMEM),
                  pl.BlockSpec(memory_space=pl.ANY)],
        out_specs=pl.BlockSpec(memory_space=pl.ANY),
        out_shape=jax.ShapeDtypeStruct((n_rows, d), h2.dtype),
        scratch_shapes=[pltpu.VMEM((MOE_BLOCK, d), h2.dtype), pltpu.VMEM((3, tm, d), h2.dtype),
                        pltpu.SemaphoreType.DMA, pltpu.SemaphoreType.DMA, pltpu.SemaphoreType.DMA((3,)),
                        pltpu.SemaphoreType.DMA((3,))],
        compiler_params=_params(("arbitrary",)),
        name="moe_dispatch",
    )(dest3, tail_row, empty_block, h2)


def _expert_kernel(bq_ref, lo_ref, hi_ref, el_ref, nq_ref, ub_ref, x_ref, wg_hbm, wu_hbm, wd_hbm, o_ref,
                   wgb, wub, wdb, sg, su, sd, sems):
    b = pl.program_id(0)
    q = bq_ref[b]
    slot = q % 2
    mats = ((wg_hbm, sg, wgb), (wu_hbm, su, wub), (wd_hbm, sd, wdb))
    n_pieces = W_PIECES * len(mats)

    def piece_copy(m, qt, t):
        hbm, stage, _ = mats[m]
        pr = stage.shape[1]
        return pltpu.make_async_copy(hbm.at[el_ref[qt], pl.ds(t * pr, pr), :], stage.at[t % 2],
                                     sems.at[m, t % 2])

    def process(qt, lo, hi):
        for p in range(n_pieces):
            m, t = p % len(mats), p // len(mats)

            @pl.when((lo <= p) & (p < hi))
            def _():
                _, stage, resident = mats[m]
                pr = stage.shape[1]
                piece_copy(m, qt, t).wait()
                resident[qt % 2, pl.ds(t * pr, pr), :] = stage[t % 2].astype(BF16)
                q2 = qt + (t + 2) // W_PIECES

                @pl.when(q2 < nq_ref[0])
                def _():
                    piece_copy(m, q2, (t + 2) % W_PIECES).start()

    @pl.when(b == 0)
    def _first_expert():
        for m in range(len(mats)):
            for t in range(2):
                piece_copy(m, 0, t).start()
        process(0, 0, n_pieces)

    @pl.when(b < ub_ref[0])
    def _compute():
        xb = x_ref[...].astype(BF16)
        hid = wgb.shape[2]
        acts = []
        for h0 in range(0, hid, HID_TILE):
            hs = pl.ds(h0, HID_TILE)
            acts.append((_silu(_bdot(xb, wgb[slot, :, hs])) * _bdot(xb, wub[slot, :, hs])).astype(BF16))
        o_ref[...] = _bdot(jnp.concatenate(acts, axis=1), wdb[slot])

    @pl.when(b >= ub_ref[0])
    def _unused():
        o_ref[...] = jnp.zeros(o_ref.shape, o_ref.dtype)

    process(q + 1, lo_ref[b], hi_ref[b])


def _experts(plan, buf, wg, wu, wd, n_out_rows):
    d = buf.shape[1]
    hid = wg.shape[2]
    assert d % W_PIECES == 0 and hid % W_PIECES == 0 and W_PIECES % 2 == 0
    blk = lambda f: pl.BlockSpec((MOE_BLOCK, d), f)
    return pl.pallas_call(
        _expert_kernel,
        grid_spec=pltpu.PrefetchScalarGridSpec(
            num_scalar_prefetch=len(plan),
            grid=(n_out_rows // MOE_BLOCK,),
            in_specs=[blk(lambda b, bq, lo, hi, el, nq, ub: (jnp.minimum(b, ub[0] - 1), 0)),
                      pl.BlockSpec(memory_space=pl.ANY), pl.BlockSpec(memory_space=pl.ANY),
                      pl.BlockSpec(memory_space=pl.ANY)],
            out_specs=blk(lambda b, bq, lo, hi, el, nq, ub: (b, 0)),
            scratch_shapes=[pltpu.VMEM((2, d, hid), BF16), pltpu.VMEM((2, d, hid), BF16),
                            pltpu.VMEM((2, hid, d), BF16),
                            pltpu.VMEM((2, d // W_PIECES, hid), F32), pltpu.VMEM((2, d // W_PIECES, hid), F32),
                            pltpu.VMEM((2, hid // W_PIECES, d), F32),
                            pltpu.SemaphoreType.DMA((3, 2))],
        ),
        out_shape=jax.ShapeDtypeStruct((n_out_rows, d), F32),
        compiler_params=_params(("arbitrary",)),
        name="moe_experts",
    )(*plan, buf, wg, wu, wd)


def _combine_kernel(dest_ref, dnext_ref, y_ref, rw_ref, x1_ref, mod_ref, lng_ref, lnb_ref, o_ref,
                    g_even, g_odd, sem):
    tm = x1_ref.shape[0]
    i = pl.program_id(0)
    n_groups = tm // DMA_UNROLL
    has_next = i + 1 < pl.num_programs(0)

    def gather(d_ref, g_ref, sl):
        def body(j, carry):
            for r in range(DMA_UNROLL):
                row = j * DMA_UNROLL + r
                for k in range(TOP_K):
                    _row_copy(y_ref, d_ref[0, 0, TOP_K * row + k], g_ref.at[k], row, sem.at[sl]).start()
            return carry
        lax.fori_loop(0, n_groups, body, 0)

    @pl.when(i == 0)
    def _():
        gather(dest_ref, g_even, 0)

    def run(sl, g_cur, g_next, prefetch):
        if prefetch:
            gather(dnext_ref, g_next, 1 - sl)
        for k in range(TOP_K):
            pltpu.make_async_copy(y_ref.at[pl.ds(0, tm)], g_cur.at[k], sem.at[sl]).wait()
        f = rw_ref[:, 0:1] * g_cur[0] + rw_ref[:, 1:2] * g_cur[1]
        o_ref[...] = _ln(ALPHA * x1_ref[...] + mod_ref[5:6, :] * f) * lng_ref[...] + lnb_ref[...]

    for sl, g_cur, g_next in ((0, g_even, g_odd), (1, g_odd, g_even)):
        for prefetch in (True, False):
            pl.when((i % 2 == sl) & (has_next == prefetch))(functools.partial(run, sl, g_cur, g_next, prefetch))


def _combine(dest, y, rw, x1, mod3, ln_g, ln_b, seq):
    r, d = x1.shape
    tm = COMBINE_TILE
    assert seq % tm == 0
    tiles_per_seq = seq // tm
    dest3 = dest.reshape(r // tm, 1, TOP_K * tm)
    n = r // tm
    row = lambda w: pl.BlockSpec((tm, w), lambda i: (i, 0))
    return pl.pallas_call(
        _combine_kernel,
        grid=(n,),
        in_specs=[pl.BlockSpec((1, 1, TOP_K * tm), lambda i: (i, 0, 0), memory_space=pltpu.SMEM),
                  pl.BlockSpec((1, 1, TOP_K * tm), lambda i: (jnp.minimum(i + 1, n - 1), 0, 0),
                               memory_space=pltpu.SMEM),
                  pl.BlockSpec(memory_space=pl.ANY),
                  row(LANES), row(d),
                  pl.BlockSpec((None, 6, d), lambda i: (i // tiles_per_seq, 0, 0)),
                  _resident((1, d)), _resident((1, d))],
        out_specs=row(d),
        out_shape=jax.ShapeDtypeStruct((r, d), F32),
        scratch_shapes=[pltpu.VMEM((TOP_K, tm, d), F32), pltpu.VMEM((TOP_K, tm, d), F32),
                        pltpu.SemaphoreType.DMA((2,))],
        compiler_params=_params(("arbitrary",)),
        name="moe_combine_ln",
    )(dest3, dest3, y, rw, x1, mod3, ln_g.reshape(1, d), ln_b.reshape(1, d))


def _routing_plan(ri, n_tokens):
    eid = ri[:, :TOP_K].reshape(-1)
    m = n_tokens * TOP_K
    onehot = (eid[:, None] == jnp.arange(N_EXPERTS, dtype=jnp.int32)[None, :]).astype(jnp.int32)
    csum = jnp.cumsum(onehot, axis=0)
    counts = csum[-1]
    rank = jnp.sum(csum * onehot, axis=1) - 1
    padded = (counts + MOE_BLOCK - 1) // MOE_BLOCK * MOE_BLOCK
    pad_end = jnp.cumsum(padded)
    pad_start = pad_end - padded
    dest = jnp.sum(onehot * pad_start[None, :], axis=1) + rank
    tail = pad_start + counts
    n_blocks = m // MOE_BLOCK + N_EXPERTS
    used_blocks = pad_end[-1] // MOE_BLOCK
    empty = jnp.arange(n_blocks + 1, dtype=jnp.int32) >= used_blocks

    n_pieces = 3 * W_PIECES
    has = counts > 0
    ordinal = jnp.cumsum(has.astype(jnp.int32)) - 1
    n_ord = ordinal[-1] + 1
    e_ids = jnp.arange(N_EXPERTS, dtype=jnp.int32)
    ord_expert = jnp.sum(jnp.where(has[None, :] & (ordinal[None, :] == e_ids[:, None]), e_ids[None, :], 0), axis=1)
    bid = jnp.arange(n_blocks, dtype=jnp.int32)
    brow = bid[:, None] * MOE_BLOCK
    in_e = ((pad_start[None, :] <= brow) & (brow < pad_end[None, :])).astype(jnp.int32)
    pick = lambda v: jnp.sum(in_e * v[None, :], axis=1)
    used = bid < used_blocks
    blk_q = jnp.where(used, pick(ordinal), n_ord - 1)
    i_in_e = bid - pick(pad_start) // MOE_BLOCK
    k_e = jnp.maximum(pick(padded) // MOE_BLOCK, 1)
    brings = used & (blk_q + 1 < n_ord)
    lo = jnp.where(brings, n_pieces * i_in_e // k_e, 0)
    hi = jnp.where(brings, n_pieces * (i_in_e + 1) // k_e, 0)
    i32 = lambda a: a.astype(jnp.int32)
    plan = (i32(blk_q), i32(lo), i32(hi), i32(ord_expert), i32(n_ord.reshape(1)), i32(used_blocks.reshape(1)))
    return i32(dest), i32(tail), i32(empty), plan, n_blocks * MOE_BLOCK


def _rope_tables(seq):
    n_freq = HEAD_DIM // 4
    inv_freq = ROPE_THETA ** (-jnp.arange(n_freq, dtype=F32) / n_freq)
    rows = seq // GRID_W
    ar = jnp.arange(rows, dtype=F32)[:, None] * inv_freq
    ac = jnp.arange(GRID_W, dtype=F32)[:, None] * inv_freq
    by_row = lambda t: jnp.broadcast_to(t[:, None, :], (rows, GRID_W, n_freq)).reshape(seq, n_freq)
    by_col = lambda t: jnp.broadcast_to(t[None, :, :], (rows, GRID_W, n_freq)).reshape(seq, n_freq)
    cos_r, sin_r, cos_c, sin_c = by_row(jnp.cos(ar)), by_row(jnp.sin(ar)), by_col(jnp.cos(ac)), by_col(jnp.sin(ac))
    zero = jnp.zeros_like(cos_r)
    cos_t = jnp.concatenate([cos_r, cos_r, cos_c, cos_c], axis=1)
    sin_a = jnp.concatenate([-sin_r, zero, -sin_c, zero], axis=1)
    sin_b = jnp.concatenate([zero, sin_r, zero, sin_c], axis=1)
    return cos_t, sin_a, sin_b


def kernel(x, c, ctx, c_ctx, w_ada, b_ada, w_in, w_gate_up, b_gate, attn_sink, gla_norm_w, w_out, ln1_g, ln1_b, w_router_group, b_router_group, w_router_expert, b_router_expert, w_exp_gate, w_exp_up, w_exp_down, ln2_g, ln2_b):
    batch, seq, d = x.shape
    n_ctx = ctx.shape[1]
    assert w_ada.shape[0] == DEPTH and batch < MOD_ROWS
    assert seq % GLA_STEP == 0 and n_ctx == GLA_STEP
    n_tok = batch * seq
    a_width = d // 2
    kv_width = a_width // A_GROUP
    b_width = d - a_width
    key_width = b_width // 2
    layer = 0

    cc = jnp.concatenate([c, c_ctx[None, :], jnp.zeros((MOD_ROWS - batch - 1, d), F32)], axis=0)
    mod, w_in_b, w_out_b = _adaln(cc, w_ada[layer], b_ada[layer], w_in[layer], w_out[layer])
    mod3 = mod.reshape(MOD_ROWS, 6, d)

    splits = (a_width, kv_width, kv_width, key_width, key_width, b_width, b_width, 2 * GATE_RANK)
    zero_up = jnp.zeros((GATE_RANK, key_width), F32)
    wup2 = jnp.concatenate([jnp.concatenate([w_gate_up[layer, 0], zero_up], axis=1),
                            jnp.concatenate([zero_up, w_gate_up[layer, 1]], axis=1)], axis=0).astype(BF16)
    bg2 = b_gate[layer].reshape(1, 2 * key_width)
    tables = _rope_tables(seq)

    xf = x.reshape(n_tok, d)
    qa, ka, va, qb, kb, vb, rb, la_f, la_b = _project(
        xf, mod3, lambda i, per_seq: i // per_seq, tables, w_in_b, splits, wup2, bg2, rope=True, seq=seq)
    _, ka_c, va_c, _, kb_c, vb_c, _, lac_f, lac_b = _project(
        ctx.reshape(batch * n_ctx, d), mod3, lambda i, per_seq: batch, tables, w_in_b, splits, wup2, bg2,
        rope=False, seq=seq, keys_values_only=True)

    out_a = _attention(attn_sink[layer], qa, ka, va, ka_c, va_c, batch, seq, n_ctx)
    o_b = _gla(kb_c, vb_c, lac_b, qb, kb, vb, la_b, batch, seq, n_ctx, reverse=True)
    out_b = _gla(kb_c, vb_c, lac_f, qb, kb, vb, la_f, batch, seq, n_ctx, reverse=False,
                 extra=(o_b, rb, gla_norm_w[layer].reshape(1, b_width)))

    wr = jnp.concatenate([w_router_expert[layer], w_router_group[layer],
                          jnp.zeros((d, LANES - N_EXPERTS - N_GROUPS), F32)], axis=1).astype(BF16)
    br = jnp.concatenate([b_router_expert[layer], b_router_group[layer],
                          jnp.zeros((LANES - N_EXPERTS - N_GROUPS,), F32)]).reshape(1, LANES)
    x1, h2, rw, ri = _outproj(out_a, out_b, xf, mod3, w_out_b[:a_width], w_out_b[a_width:],
                              ln1_g[layer], ln1_b[layer], wr, br, seq)

    dest, tail_row, empty_block, plan, n_buf_rows = _routing_plan(ri, n_tok)
    buf = _dispatch(dest, tail_row, empty_block, h2)
    y = _experts(plan, buf, w_exp_gate[layer], w_exp_up[layer], w_exp_down[layer], n_buf_rows)
    out = _combine(dest, y, rw, x1, mod3, ln2_g[layer], ln2_b[layer], seq)
    return out.reshape(batch, seq, d)
```

```python
import functools

import jax
import jax.numpy as jnp
from jax import lax
from jax.experimental import pallas as pl
from jax.experimental.pallas import tpu as pltpu

F32 = jnp.float32
BF16 = jnp.bfloat16

HEAD_DIM = 128
GRID_W = 64
WINDOW = 128
A_BLOCK = 128
A_GROUP = 4
ROPE_THETA = 10000.0
B_HEADS = 4
GATE_RANK = 16
GATE_TAU = 16.0
GLA_CHUNK = 64
N_GROUPS = 4
EXPERTS_PER_GROUP = 8
N_EXPERTS = N_GROUPS * EXPERTS_PER_GROUP
TOP_K = 2
DEPTH = 1
ALPHA = (2.0 * DEPTH) ** 0.25
LN_EPS = 1e-6
LOG2_E = 1.4426950408889634
ATTN_EXP2_SCALE = HEAD_DIM ** -0.5 * LOG2_E

LANES = 128
SUBLANES = 8
MOD_ROWS = 8
VMEM_LIMIT = 56 * 1024 * 1024

DISPATCH_TILE = 1024
COMBINE_TILE = 256
OUT_TILE = 512
SUB_ROWS = 256
PROJ_COLS = 1024
Q_BLOCKS = 8
GLA_STEP = 256
MOE_BLOCK = 256
HID_TILE = 512
W_PIECES = 4
DMA_UNROLL = 8


def _params(sem):
    return pltpu.CompilerParams(dimension_semantics=sem, vmem_limit_bytes=VMEM_LIMIT)


def _resident(shape):
    nd = len(shape)
    return pl.BlockSpec(shape, lambda *_: (0,) * nd, pipeline_mode=pl.Buffered(1))


def _ln(x):
    mu = jnp.mean(x, axis=-1, keepdims=True)
    xc = x - mu
    var = jnp.mean(xc * xc, axis=-1, keepdims=True)
    return xc * lax.rsqrt(var + LN_EPS)


def _silu(x):
    return x * jax.nn.sigmoid(x)


def _bdot(a, b):
    return jnp.dot(a, b, preferred_element_type=F32)


def _bdot_nt(a, b):
    return lax.dot_general(a, b, (((1,), (1,)), ((), ())), preferred_element_type=F32)


def _adaln_kernel(c_ref, w_ref, b_ref, o_ref):
    s = _silu(c_ref[...]).astype(BF16)
    o_ref[...] = _bdot(s, w_ref[...].astype(BF16)) + b_ref[...]


def _adaln(cc, w_ada, b_ada):
    d, n = w_ada.shape
    tn = 1024
    return pl.pallas_call(
        _adaln_kernel,
        grid=(n // tn,),
        in_specs=[pl.BlockSpec((MOD_ROWS, d), lambda j: (0, 0)),
                  pl.BlockSpec((d, tn), lambda j: (0, j)),
                  pl.BlockSpec((1, tn), lambda j: (0, j))],
        out_specs=pl.BlockSpec((MOD_ROWS, tn), lambda j: (0, j)),
        out_shape=jax.ShapeDtypeStruct((MOD_ROWS, n), F32),
        compiler_params=_params(("arbitrary",)),
        name="adaln",
    )(cc, w_ada, b_ada.reshape(1, n))


def _proj_kernel(x_ref, mod_ref, cos_ref, sina_ref, sinb_ref, w_ref, wup, bg, *out_refs, groups, rope):
    for r0 in range(0, x_ref.shape[0], SUB_ROWS):
        _proj_rows(slice(r0, r0 + SUB_ROWS), x_ref, mod_ref, cos_ref, sina_ref, sinb_ref, w_ref, wup, bg,
                   out_refs, groups, rope)


def _proj_rows(rs, x_ref, mod_ref, cos_ref, sina_ref, sinb_ref, w_ref, wup, bg, out_refs, groups, rope):
    h = _ln(x_ref[rs, :]) * (1.0 + mod_ref[1:2, :]) + mod_ref[0:1, :]
    hb = h.astype(BF16)

    def rot(t):
        return (t * cos_ref[rs, :] + pltpu.roll(t, 96, 1) * sina_ref[rs, :]
                + pltpu.roll(t, 32, 1) * sinb_ref[rs, :])

    col = 0
    refs = iter(out_refs)
    for n, wanted, rotary, post in groups:
        if wanted:
            o_ref = next(refs)
            step = min(n, PROJ_COLS)
            for c0 in range(0, n, step):
                t = _bdot(hb, w_ref[:, col + c0:col + c0 + step])
                if rotary and rope:
                    for l0 in range(0, step, LANES):
                        r = rot(t[:, l0:l0 + LANES])
                        o_ref[rs, c0 + l0:c0 + l0 + LANES] = (r if post == 1.0 else r * post).astype(o_ref.dtype)
                else:
                    o_ref[rs, c0:c0 + step] = t.astype(o_ref.dtype)
        col += n
    laf_o, lab_o = refs
    gl = _bdot(hb, w_ref[:, col:]).astype(BF16)
    z = _bdot(gl, wup[...]) + bg[...]
    la = (jnp.minimum(z, 0.0) - jnp.log1p(jnp.exp(-jnp.abs(z)))) / GATE_TAU
    kw = laf_o.shape[1]
    laf_o[rs, :] = la[:, :kw]
    lab_o[rs, :] = la[:, kw:]


def _project(xf, mod3, mod_row, tables, w_in_b, splits, wup2, bg2, *, rope, seq, keys_values_only=False):
    r, d = xf.shape
    tm = OUT_TILE
    assert r % tm == 0 and seq % tm == 0 and tm % SUB_ROWS == 0
    tiles_per_seq = seq // tm
    mod_row_of_tile = lambda i: mod_row(i, tiles_per_seq)
    n_gate = splits[-1]
    assert sum(splits) == w_in_b.shape[1] and (sum(splits) - n_gate) % LANES == 0
    kv = keys_values_only
    spec = ((BF16, not kv, True, ATTN_EXP2_SCALE), (BF16, True, True, 1.0), (BF16, True, False, 1.0),
            (BF16, not kv, False, 1.0), (BF16, True, False, 1.0), (BF16, True, False, 1.0),
            (BF16, not kv, False, 1.0))
    groups = tuple((n, wanted, rotary, post) for n, (_, wanted, rotary, post) in zip(splits, spec))
    key_width = splits[3]
    outs = [(n, dt) for n, (dt, wanted, _, _) in zip(splits, spec) if wanted] + [(key_width, F32)] * 2
    cos_t, sina_t, sinb_t = tables
    row = lambda n: pl.BlockSpec((tm, n), lambda i: (i, 0))
    tab = pl.BlockSpec((tm, LANES), lambda i: (i % tiles_per_seq, 0))
    res = pl.pallas_call(
        functools.partial(_proj_kernel, groups=groups, rope=rope),
        grid=(r // tm,),
        in_specs=[row(d),
                  pl.BlockSpec((None, 6, d), lambda i: (mod_row_of_tile(i), 0, 0)),
                  tab, tab, tab,
                  _resident(w_in_b.shape), _resident(wup2.shape), _resident(bg2.shape)],
        out_specs=[row(n) for n, _ in outs],
        out_shape=[jax.ShapeDtypeStruct((r, n), dt) for n, dt in outs],
        compiler_params=_params(("parallel",)),
        name="in_proj_rope" if rope else "in_proj_ctx",
    )(xf, mod3, cos_t, sina_t, sinb_t, w_in_b, wup2, bg2)
    res = iter(res)
    return tuple(next(res) if wanted else None for _, wanted, _, _ in spec) + tuple(res)


def _attn_kernel(sink_ref, q_ref, *refs):
    n_kv = Q_BLOCKS + 2
    k_refs, v_refs = refs[:n_kv], refs[n_kv:2 * n_kv]
    kx_ref, vx_ref, o_ref = refs[2 * n_kv:]
    n = pl.program_id(1)
    last = pl.num_programs(1) - 1
    blk = A_BLOCK
    rows = A_GROUP * blk
    n_ctx = kx_ref.shape[0]
    qi = lax.broadcasted_iota(jnp.int32, (rows, blk), 0) % blk
    kj = lax.broadcasted_iota(jnp.int32, (rows, blk), 1)
    for u in range(Q_BLOCKS):
        ok_prev = (kj >= qi) & (n > 0) if u == 0 else (kj >= qi)
        ok_next = (kj <= qi) & (n < last) if u == Q_BLOCKS - 1 else (kj <= qi)
        qs = slice(u * blk, (u + 1) * blk)
        kp_ref, kc_ref, kn_ref = k_refs[u:u + 3]
        vp_ref, vc_ref, vn_ref = v_refs[u:u + 3]
        for hk in range(kc_ref.shape[1] // HEAD_DIM):
            ks = slice(hk * HEAD_DIM, (hk + 1) * HEAD_DIM)
            q4 = jnp.concatenate(
                [q_ref[qs, (hk * A_GROUP + g) * HEAD_DIM:(hk * A_GROUP + g + 1) * HEAD_DIM]
                 for g in range(A_GROUP)], axis=0)
            s_ctx = _bdot_nt(q4, kx_ref[:, ks])
            s_prev = jnp.where(ok_prev, _bdot_nt(q4, kp_ref[:, ks]), -jnp.inf)
            s_cur = _bdot_nt(q4, kc_ref[:, ks])
            s_next = jnp.where(ok_next, _bdot_nt(q4, kn_ref[:, ks]), -jnp.inf)
            sink = jnp.concatenate(
                [jnp.full((blk, 1), sink_ref[hk * A_GROUP + g] * LOG2_E, F32) for g in range(A_GROUP)], axis=0)
            m_lanes = jnp.maximum(jnp.maximum(s_prev, s_cur), s_next)
            for l0 in range(0, n_ctx, LANES):
                m_lanes = jnp.maximum(m_lanes, s_ctx[:, l0:l0 + LANES])
            m = jnp.maximum(jnp.max(m_lanes, axis=-1, keepdims=True), sink)
            e_ctx = jnp.exp2(s_ctx - m)
            e_prev = jnp.exp2(s_prev - m)
            e_cur = jnp.exp2(s_cur - m)
            e_next = jnp.exp2(s_next - m)
            e_lanes = e_prev + e_cur + e_next
            for l0 in range(0, n_ctx, LANES):
                e_lanes = e_lanes + e_ctx[:, l0:l0 + LANES]
            den = jnp.exp2(sink - m) + jnp.sum(e_lanes, axis=-1, keepdims=True)
            o = (_bdot(e_ctx.astype(BF16), vx_ref[:, ks])
                 + (_bdot(e_prev.astype(BF16), vp_ref[:, ks])
                    + _bdot(e_cur.astype(BF16), vc_ref[:, ks])
                    + _bdot(e_next.astype(BF16), vn_ref[:, ks]))) * (1.0 / den)
            for g in range(A_GROUP):
                hq = hk * A_GROUP + g
                o_ref[qs, hq * HEAD_DIM:(hq + 1) * HEAD_DIM] = o[g * blk:(g + 1) * blk, :].astype(o_ref.dtype)


def _attention(sink, qa, ka, va, ka_c, va_c, batch, seq, n_ctx):
    blk = A_BLOCK
    nb = seq // blk
    assert nb % Q_BLOCKS == 0
    steps = nb // Q_BLOCKS
    aw = qa.shape[1]
    kvw = ka.shape[1]
    kv = lambda j: pl.BlockSpec(
        (blk, kvw), lambda b, n: (b * nb + jnp.clip(n * Q_BLOCKS - 1 + j, 0, nb - 1), 0))
    kv_specs = [kv(j) for j in range(Q_BLOCKS + 2)]
    qo = pl.BlockSpec((Q_BLOCKS * blk, aw), lambda b, n: (b * steps + n, 0))
    ctx = pl.BlockSpec((n_ctx, kvw), lambda b, n: (b, 0))
    return pl.pallas_call(
        _attn_kernel,
        grid=(batch, steps),
        in_specs=[pl.BlockSpec(memory_space=pltpu.SMEM), qo] + kv_specs + kv_specs + [ctx, ctx],
        out_specs=qo,
        out_shape=jax.ShapeDtypeStruct((batch * seq, aw), BF16),
        compiler_params=_params(("parallel", "parallel")),
        name="window_gqa",
    )(sink, qa, *([ka] * (Q_BLOCKS + 2)), *([va] * (Q_BLOCKS + 2)), ka_c, va_c)


def _chunk_cumsum(g, *, reverse):
    rows = g.shape[0]
    p = lax.broadcasted_iota(jnp.int32, g.shape, 0) % GLA_CHUNK
    s = 1
    while s < GLA_CHUNK:
        if reverse:
            g = g + jnp.where(p < GLA_CHUNK - s, pltpu.roll(g, rows - s, 0), 0.0)
        else:
            g = g + jnp.where(p >= s, pltpu.roll(g, s, 0), 0.0)
        s *= 2
    return g


def _per_chunk_row(x, i):
    c = GLA_CHUNK
    return jnp.concatenate(
        [jnp.broadcast_to(x[j * c + i:j * c + i + 1, :], (c, x.shape[1])) for j in range(x.shape[0] // c)], axis=0)


def _gla_block(q, k, v, cum, state_t, mask, *, reverse, need_o):
    c = GLA_CHUNK
    rows, dk = k.shape
    n_chunks = rows // c
    k = k.astype(F32)
    i_last = 0 if reverse else c - 1
    i_mid = c // 2 if reverse else c // 2 - 1
    b_last = _per_chunk_row(cum, i_last)
    kdec = (k * jnp.exp(b_last - cum)).astype(BF16)
    if need_o:
        b_mid = _per_chunk_row(cum, i_mid)
        qc = q.astype(F32) * dk ** -0.5
        qm = (qc * jnp.exp(cum - b_mid)).astype(BF16)
        km = (k * jnp.exp(b_mid - cum)).astype(BF16)
        a = jnp.where(mask, _bdot_nt(qm, km), 0.0)
        o_intra = _bdot(a.astype(BF16), v)
        qe = (qc * jnp.exp(cum)).astype(BF16)
    o_inter = [None] * n_chunks
    for j in (reversed(range(n_chunks)) if reverse else range(n_chunks)):
        rs = slice(j * c, (j + 1) * c)
        if need_o:
            o_inter[j] = _bdot_nt(qe[rs, :], state_t.astype(BF16))
        kv_t = lax.dot_general(v[rs, :], kdec[rs, :], (((0,), (0,)), ((), ())), preferred_element_type=F32)
        state_t = state_t * jnp.exp(cum[j * c + i_last:j * c + i_last + 1, :]) + kv_t
    o = o_intra + jnp.concatenate(o_inter, axis=0) if need_o else None
    return o, state_t


def _gla_kernel(*refs, reverse, final):
    if final:
        (kx_ref, vx_ref, gx_ref, q_ref, k_ref, v_ref, g_ref, ob_ref, r_ref, nw_ref, o_ref, st_ref) = refs
    else:
        (kx_ref, vx_ref, gx_ref, q_ref, k_ref, v_ref, g_ref, o_ref, st_ref) = refs
    t = pl.program_id(1)
    n_heads = st_ref.shape[0]
    dv, dk = st_ref.shape[1:]

    @pl.when(t == 0)
    def _context():
        cum = _chunk_cumsum(gx_ref[...], reverse=reverse)
        for h in range(n_heads):
            ks, vs = slice(h * dk, (h + 1) * dk), slice(h * dv, (h + 1) * dv)
            _, st = _gla_block(None, kx_ref[:, ks], vx_ref[:, vs], cum[:, ks], jnp.zeros((dv, dk), F32), None,
                               reverse=reverse, need_o=False)
            st_ref[h] = st

    @pl.when(t > 0)
    def _latent():
        rows = k_ref.shape[0]
        cum = _chunk_cumsum(g_ref[...], reverse=reverse)
        r = lax.broadcasted_iota(jnp.int32, (rows, rows), 0)
        s = lax.broadcasted_iota(jnp.int32, (rows, rows), 1)
        causal = (s >= r) if reverse else (s <= r)
        mask = causal & ((r // GLA_CHUNK) == (s // GLA_CHUNK))
        for h in range(n_heads):
            ks, vs = slice(h * dk, (h + 1) * dk), slice(h * dv, (h + 1) * dv)
            o, st = _gla_block(q_ref[:, ks], k_ref[:, ks], v_ref[:, vs], cum[:, ks], st_ref[h], mask,
                               reverse=reverse, need_o=True)
            st_ref[h] = st
            if final:
                o = o + ob_ref[:, vs].astype(F32)
                o = o * lax.rsqrt(jnp.mean(o * o, axis=-1, keepdims=True) + LN_EPS)
                o = o * nw_ref[:, vs]
                o = o * _silu(r_ref[:, vs].astype(F32))
            o_ref[:, vs] = o.astype(o_ref.dtype)


def _gla(kb_c, vb_c, la_c, qb, kb, vb, la, batch, seq, n_ctx, *, reverse, extra=None):
    assert n_ctx % GLA_CHUNK == 0 and GLA_STEP % GLA_CHUNK == 0
    tb = GLA_STEP
    nt = seq // tb
    kw, vw = qb.shape[1], vb.shape[1]
    final = extra is not None

    def lat(b, t):
        i = jnp.maximum(t - 1, 0)
        if reverse:
            i = nt - 1 - i
        return (b * nt + i, 0)

    cx = lambda b, t: (b, 0)
    in_specs = [pl.BlockSpec((n_ctx, kw), cx), pl.BlockSpec((n_ctx, vw), cx), pl.BlockSpec((n_ctx, kw), cx),
                pl.BlockSpec((tb, kw), lat), pl.BlockSpec((tb, kw), lat),
                pl.BlockSpec((tb, vw), lat), pl.BlockSpec((tb, kw), lat)]
    args = [kb_c, vb_c, la_c, qb, kb, vb, la]
    if final:
        o_other, rb, norm_w = extra
        in_specs += [pl.BlockSpec((tb, vw), lat), pl.BlockSpec((tb, vw), lat), _resident((1, vw))]
        args += [o_other, rb, norm_w]
    return pl.pallas_call(
        functools.partial(_gla_kernel, reverse=reverse, final=final),
        grid=(batch, nt + 1),
        in_specs=in_specs,
        out_specs=pl.BlockSpec((tb, vw), lat),
        out_shape=jax.ShapeDtypeStruct((batch * seq, vw), BF16),
        scratch_shapes=[pltpu.VMEM((B_HEADS, vw // B_HEADS, kw // B_HEADS), F32)],
        compiler_params=_params(("parallel", "arbitrary")),
        name="gla_fwd_out" if final else "gla_bwd",
    )(*args)


def _outproj_kernel(oa_ref, ob_ref, x_ref, mod_ref, wt_ref, wb_ref, lng_ref, lnb_ref, wr_ref, br_ref,
                    x1_o, h2_o, rw_o, ri_o):
    for r0 in range(0, x_ref.shape[0], SUB_ROWS):
        _outproj_rows(slice(r0, r0 + SUB_ROWS), oa_ref, ob_ref, x_ref, mod_ref, wt_ref, wb_ref, lng_ref, lnb_ref,
                      wr_ref, br_ref, x1_o, h2_o, rw_o, ri_o)


def _outproj_rows(rs, oa_ref, ob_ref, x_ref, mod_ref, wt_ref, wb_ref, lng_ref, lnb_ref, wr_ref, br_ref,
                  x1_o, h2_o, rw_o, ri_o):
    y = _bdot(oa_ref[rs, :], wt_ref[...]) + _bdot(ob_ref[rs, :], wb_ref[...])
    x1 = _ln(ALPHA * x_ref[rs, :] + mod_ref[2:3, :] * y) * lng_ref[...] + lnb_ref[...]
    x1_o[rs, :] = x1
    h2 = _ln(x1) * (1.0 + mod_ref[4:5, :]) + mod_ref[3:4, :]
    h2_o[rs, :] = h2
    lg = _bdot(h2.astype(BF16), wr_ref[...]) + br_ref[...]
    lane = lax.broadcasted_iota(jnp.int32, lg.shape, 1)
    lanef = lane.astype(F32)
    big = float(LANES)
    is_g = (lane >= N_EXPERTS) & (lane < N_EXPERTS + N_GROUPS)
    gl = jnp.where(is_g, lg, -jnp.inf)
    gmax = jnp.max(gl, axis=-1, keepdims=True)
    pg_top = 1.0 / jnp.sum(jnp.exp(gl - gmax), axis=-1, keepdims=True)
    grp = jnp.min(jnp.where(gl == gmax, lanef, big), axis=-1, keepdims=True) - N_EXPERTS
    in_grp = (lane < N_EXPERTS) & ((lane // EXPERTS_PER_GROUP).astype(F32) == grp)
    el = jnp.where(in_grp, lg, -jnp.inf)
    m1 = jnp.max(el, axis=-1, keepdims=True)
    i1 = jnp.min(jnp.where(el == m1, lanef, big), axis=-1, keepdims=True)
    el2 = jnp.where(lanef == i1, -jnp.inf, el)
    m2 = jnp.max(el2, axis=-1, keepdims=True)
    i2 = jnp.min(jnp.where(el2 == m2, lanef, big), axis=-1, keepdims=True)
    e2 = jnp.exp(m2 - m1)
    w1 = pg_top / (1.0 + e2)
    w2 = pg_top * e2 / (1.0 + e2)
    rw_o[rs, :] = jnp.where(lane == 0, w1, jnp.where(lane == 1, w2, 0.0))
    ri_o[rs, :] = jnp.where(lane == 0, i1, jnp.where(lane == 1, i2, 0.0)).astype(jnp.int32)


def _outproj(out_a, out_b, xf, mod3, w_top, w_bot, ln_g, ln_b, wr, br, seq):
    r, d = xf.shape
    tm = OUT_TILE
    assert seq % tm == 0 and tm % SUB_ROWS == 0
    tiles_per_seq = seq // tm
    row = lambda n: pl.BlockSpec((tm, n), lambda i: (i, 0))
    return pl.pallas_call(
        _outproj_kernel,
        grid=(r // tm,),
        in_specs=[row(out_a.shape[1]), row(out_b.shape[1]), row(d),
                  pl.BlockSpec((None, 6, d), lambda i: (i // tiles_per_seq, 0, 0)),
                  _resident(w_top.shape), _resident(w_bot.shape),
                  _resident((1, d)), _resident((1, d)), _resident(wr.shape), _resident(br.shape)],
        out_specs=[row(d), row(d), row(LANES), row(LANES)],
        out_shape=[jax.ShapeDtypeStruct((r, d), F32), jax.ShapeDtypeStruct((r, d), F32),
                   jax.ShapeDtypeStruct((r, LANES), F32), jax.ShapeDtypeStruct((r, LANES), jnp.int32)],
        compiler_params=_params(("parallel",)),
        name="out_proj_router",
    )(out_a, out_b, xf, mod3, w_top, w_bot, ln_g.reshape(1, d), ln_b.reshape(1, d), wr, br)


def _row_copy(src_ref, src_row, dst_ref, dst_row, sem):
    return pltpu.make_async_copy(src_ref.at[pl.ds(src_row, 1)], dst_ref.at[pl.ds(dst_row, 1)], sem)


def _dispatch_kernel(dest_ref, tail_ref, empty_ref, h_ref, buf_ref, zero_ref, hbuf, sem, zsem, lsem, rsem):
    tm = hbuf.shape[1]
    step = pl.program_id(0)

    def zero_block(b):
        row = pl.multiple_of(b * MOE_BLOCK, MOE_BLOCK)
        return pltpu.make_async_copy(zero_ref, buf_ref.at[pl.ds(row, MOE_BLOCK)], zsem)

    def for_empty_blocks(fn):
        def body(b, carry):
            @pl.when(empty_ref[b] != 0)
            def _():
                fn(b)
            return carry
        lax.fori_loop(0, empty_ref.shape[0], body, 0)

    @pl.when(step == 0)
    def _zero_fill():
        zero_ref[...] = jnp.zeros(zero_ref.shape, zero_ref.dtype)
        for e in range(N_EXPERTS):
            start = pl.multiple_of(tail_ref[e] // SUBLANES * SUBLANES, SUBLANES)
            pltpu.make_async_copy(zero_ref, buf_ref.at[pl.ds(start, MOE_BLOCK)], sem).start()
        for e in range(N_EXPERTS):
            pltpu.make_async_copy(zero_ref, buf_ref.at[pl.ds(0, MOE_BLOCK)], sem).wait()
        for_empty_blocks(lambda b: zero_block(b).start())

    @pl.when(step == pl.num_programs(0) - 1)
    def _zero_done():
        for_empty_blocks(lambda b: zero_block(b).wait())

    n_slots = hbuf.shape[0]
    slot = step % n_slots

    def load(tile, sl):
        row = pl.multiple_of(tile * tm, tm)
        return pltpu.make_async_copy(h_ref.at[pl.ds(row, tm)], hbuf.at[sl], lsem.at[sl])

    @pl.when(step == 0)
    def _():
        load(0, 0).start()

    @pl.when(step + 1 < pl.num_programs(0))
    def _():
        load(step + 1, (step + 1) % n_slots).start()

    load(step, slot).wait()

    def issue(i, carry):
        for k in range(TOP_K):
            _row_copy(hbuf.at[slot], i, buf_ref, dest_ref[0, 0, TOP_K * i + k], rsem.at[slot]).start(priority=k % 2)
        return carry

    lax.fori_loop(0, tm, issue, 0, unroll=DMA_UNROLL)

    def wait_tile(sl):
        for _ in range(TOP_K):
            pltpu.make_async_copy(hbuf.at[sl], buf_ref.at[pl.ds(0, tm)], rsem.at[sl]).wait()

    pl.when(step > 0)(lambda: wait_tile((step + n_slots - 1) % n_slots))
    pl.when(step == pl.num_programs(0) - 1)(lambda: wait_tile(slot))


def _dispatch(dest, tail_row, empty_block, h2):
    r, d = h2.shape
    tm = DISPATCH_TILE
    assert r % tm == 0
    dest3 = dest.reshape(r // tm, 1, TOP_K * tm)
    n_rows = empty_block.shape[0] * MOE_BLOCK
    return pl.pallas_call(
        _dispatch_kernel,
        grid=(r // tm,),
        in_specs=[pl.BlockSpec((1, 1, TOP_K * tm), lambda i: (i, 0, 0), memory_space=pltpu.SMEM),
                  pl.BlockSpec(memory_space=pltpu.SMEM),
                  pl.BlockSpec(memory_space=pltpu.SMEM),
                  pl.BlockSpec(memory_space=pl.ANY)],
        out_specs=pl.BlockSpec(memory_space=pl.ANY),
        out_shape=jax.ShapeDtypeStruct((n_rows, d), h2.dtype),
        scratch_shapes=[pltpu.VMEM((MOE_BLOCK, d), h2.dtype), pltpu.VMEM((3, tm, d), h2.dtype),
                        pltpu.SemaphoreType.DMA, pltpu.SemaphoreType.DMA, pltpu.SemaphoreType.DMA((3,)),
                        pltpu.SemaphoreType.DMA((3,))],
        compiler_params=_params(("arbitrary",)),
        name="moe_dispatch",
    )(dest3, tail_row, empty_block, h2)


def _expert_kernel(bq_ref, lo_ref, hi_ref, el_ref, nq_ref, ub_ref, x_ref, wg_hbm, wu_hbm, wd_hbm, o_ref,
                   wgb, wub, wdb, sg, su, sd, sems):
    b = pl.program_id(0)
    q = bq_ref[b]
    slot = q % 2
    mats = ((wg_hbm, sg, wgb), (wu_hbm, su, wub), (wd_hbm, sd, wdb))
    n_pieces = W_PIECES * len(mats)

    def piece_copy(m, qt, t):
        hbm, stage, _ = mats[m]
        pr = stage.shape[1]
        return pltpu.make_async_copy(hbm.at[el_ref[qt], pl.ds(t * pr, pr), :], stage.at[t % 2],
                                     sems.at[m, t % 2])

    def process(qt, lo, hi):
        for p in range(n_pieces):
            m, t = p % len(mats), p // len(mats)

            @pl.when((lo <= p) & (p < hi))
            def _():
                _, stage, resident = mats[m]
                pr = stage.shape[1]
                piece_copy(m, qt, t).wait()
                resident[qt % 2, pl.ds(t * pr, pr), :] = stage[t % 2].astype(BF16)
                q2 = qt + (t + 2) // W_PIECES

                @pl.when(q2 < nq_ref[0])
                def _():
                    piece_copy(m, q2, (t + 2) % W_PIECES).start()

    @pl.when(b == 0)
    def _first_expert():
        for m in range(len(mats)):
            for t in range(2):
                piece_copy(m, 0, t).start()
        process(0, 0, n_pieces)

    @pl.when(b < ub_ref[0])
    def _compute():
        xb = x_ref[...].astype(BF16)
        hid = wgb.shape[2]
        acts = []
        for h0 in range(0, hid, HID_TILE):
            hs = pl.ds(h0, HID_TILE)
            acts.append((_silu(_bdot(xb, wgb[slot, :, hs])) * _bdot(xb, wub[slot, :, hs])).astype(BF16))
        o_ref[...] = _bdot(jnp.concatenate(acts, axis=1), wdb[slot])

    @pl.when(b >= ub_ref[0])
    def _unused():
        o_ref[...] = jnp.zeros(o_ref.shape, o_ref.dtype)

    process(q + 1, lo_ref[b], hi_ref[b])


def _experts(plan, buf, wg, wu, wd, n_out_rows):
    d = buf.shape[1]
    hid = wg.shape[2]
    assert d % W_PIECES == 0 and hid % W_PIECES == 0 and W_PIECES % 2 == 0
    blk = lambda f: pl.BlockSpec((MOE_BLOCK, d), f)
    return pl.pallas_call(
        _expert_kernel,
        grid_spec=pltpu.PrefetchScalarGridSpec(
            num_scalar_prefetch=len(plan),
            grid=(n_out_rows // MOE_BLOCK,),
            in_specs=[blk(lambda b, bq, lo, hi, el, nq, ub: (jnp.minimum(b, ub[0] - 1), 0)),
                      pl.BlockSpec(memory_space=pl.ANY), pl.BlockSpec(memory_space=pl.ANY),
                      pl.BlockSpec(memory_space=pl.ANY)],
            out_specs=blk(lambda b, bq, lo, hi, el, nq, ub: (b, 0)),
            scratch_shapes=[pltpu.VMEM((2, d, hid), BF16), pltpu.VMEM((2, d, hid), BF16),
                            pltpu.VMEM((2, hid, d), BF16),
                            pltpu.VMEM((2, d // W_PIECES, hid), F32), pltpu.VMEM((2, d // W_PIECES, hid), F32),
                            pltpu.VMEM((2, hid // W_PIECES, d), F32),
                            pltpu.SemaphoreType.DMA((3, 2))],
        ),
        out_shape=jax.ShapeDtypeStruct((n_out_rows, d), F32),
        compiler_params=_params(("arbitrary",)),
        name="moe_experts",
    )(*plan, buf, wg, wu, wd)


def _combine_kernel(dest_ref, dnext_ref, y_ref, rw_ref, x1_ref, mod_ref, lng_ref, lnb_ref, o_ref,
                    g_even, g_odd, sem):
    tm = x1_ref.shape[0]
    i = pl.program_id(0)
    n_groups = tm // DMA_UNROLL
    has_next = i + 1 < pl.num_programs(0)

    def gather(d_ref, g_ref, sl):
        def body(j, carry):
            for r in range(DMA_UNROLL):
                row = j * DMA_UNROLL + r
                for k in range(TOP_K):
                    _row_copy(y_ref, d_ref[0, 0, TOP_K * row + k], g_ref.at[k], row, sem.at[sl]).start(priority=k % 2)
            return carry
        lax.fori_loop(0, n_groups, body, 0)

    @pl.when(i == 0)
    def _():
        gather(dest_ref, g_even, 0)

    def run(sl, g_cur, g_next, prefetch):
        if prefetch:
            gather(dnext_ref, g_next, 1 - sl)
        for k in range(TOP_K):
            pltpu.make_async_copy(y_ref.at[pl.ds(0, tm)], g_cur.at[k], sem.at[sl]).wait()
        f = rw_ref[:, 0:1] * g_cur[0] + rw_ref[:, 1:2] * g_cur[1]
        o_ref[...] = _ln(ALPHA * x1_ref[...] + mod_ref[5:6, :] * f) * lng_ref[...] + lnb_ref[...]

    for sl, g_cur, g_next in ((0, g_even, g_odd), (1, g_odd, g_even)):
        for prefetch in (True, False):
            pl.when((i % 2 == sl) & (has_next == prefetch))(functools.partial(run, sl, g_cur, g_next, prefetch))


def _combine(dest, y, rw, x1, mod3, ln_g, ln_b, seq):
    r, d = x1.shape
    tm = COMBINE_TILE
    assert seq % tm == 0
    tiles_per_seq = seq // tm
    dest3 = dest.reshape(r // tm, 1, TOP_K * tm)
    n = r // tm
    row = lambda w: pl.BlockSpec((tm, w), lambda i: (i, 0))
    return pl.pallas_call(
        _combine_kernel,
        grid=(n,),
        in_specs=[pl.BlockSpec((1, 1, TOP_K * tm), lambda i: (i, 0, 0), memory_space=pltpu.SMEM),
                  pl.BlockSpec((1, 1, TOP_K * tm), lambda i: (jnp.minimum(i + 1, n - 1), 0, 0),
                               memory_space=pltpu.SMEM),
                  pl.BlockSpec(memory_space=pl.ANY),
                  row(LANES), row(d),
                  pl.BlockSpec((None, 6, d), lambda i: (i // tiles_per_seq, 0, 0)),
                  _resident((1, d)), _resident((1, d))],
        out_specs=row(d),
        out_shape=jax.ShapeDtypeStruct((r, d), F32),
        scratch_shapes=[pltpu.VMEM((TOP_K, tm, d), F32), pltpu.VMEM((TOP_K, tm, d), F32),
                        pltpu.SemaphoreType.DMA((2,))],
        compiler_params=_params(("arbitrary",)),
        name="moe_combine_ln",
    )(dest3, dest3, y, rw, x1, mod3, ln_g.reshape(1, d), ln_b.reshape(1, d))


def _routing_plan(ri, n_tokens):
    eid = ri[:, :TOP_K].reshape(-1)
    m = n_tokens * TOP_K
    onehot = (eid[:, None] == jnp.arange(N_EXPERTS, dtype=jnp.int32)[None, :]).astype(jnp.int32)
    csum = jnp.cumsum(onehot, axis=0)
    counts = csum[-1]
    rank = jnp.sum(csum * onehot, axis=1) - 1
    padded = (counts + MOE_BLOCK - 1) // MOE_BLOCK * MOE_BLOCK
    pad_end = jnp.cumsum(padded)
    pad_start = pad_end - padded
    dest = jnp.sum(onehot * pad_start[None, :], axis=1) + rank
    tail = pad_start + counts
    n_blocks = m // MOE_BLOCK + N_EXPERTS
    used_blocks = pad_end[-1] // MOE_BLOCK
    empty = jnp.arange(n_blocks + 1, dtype=jnp.int32) >= used_blocks

    n_pieces = 3 * W_PIECES
    has = counts > 0
    ordinal = jnp.cumsum(has.astype(jnp.int32)) - 1
    n_ord = ordinal[-1] + 1
    e_ids = jnp.arange(N_EXPERTS, dtype=jnp.int32)
    ord_expert = jnp.sum(jnp.where(has[None, :] & (ordinal[None, :] == e_ids[:, None]), e_ids[None, :], 0), axis=1)
    bid = jnp.arange(n_blocks, dtype=jnp.int32)
    brow = bid[:, None] * MOE_BLOCK
    in_e = ((pad_start[None, :] <= brow) & (brow < pad_end[None, :])).astype(jnp.int32)
    pick = lambda v: jnp.sum(in_e * v[None, :], axis=1)
    used = bid < used_blocks
    blk_q = jnp.where(used, pick(ordinal), n_ord - 1)
    i_in_e = bid - pick(pad_start) // MOE_BLOCK
    k_e = jnp.maximum(pick(padded) // MOE_BLOCK, 1)
    brings = used & (blk_q + 1 < n_ord)
    lo = jnp.where(brings, n_pieces * i_in_e // k_e, 0)
    hi = jnp.where(brings, n_pieces * (i_in_e + 1) // k_e, 0)
    i32 = lambda a: a.astype(jnp.int32)
    plan = (i32(blk_q), i32(lo), i32(hi), i32(ord_expert), i32(n_ord.reshape(1)), i32(used_blocks.reshape(1)))
    return i32(dest), i32(tail), i32(empty), plan, n_blocks * MOE_BLOCK


def _rope_tables(seq):
    n_freq = HEAD_DIM // 4
    inv_freq = ROPE_THETA ** (-jnp.arange(n_freq, dtype=F32) / n_freq)
    rows = seq // GRID_W
    ar = jnp.arange(rows, dtype=F32)[:, None] * inv_freq
    ac = jnp.arange(GRID_W, dtype=F32)[:, None] * inv_freq
    by_row = lambda t: jnp.broadcast_to(t[:, None, :], (rows, GRID_W, n_freq)).reshape(seq, n_freq)
    by_col = lambda t: jnp.broadcast_to(t[None, :, :], (rows, GRID_W, n_freq)).reshape(seq, n_freq)
    cos_r, sin_r, cos_c, sin_c = by_row(jnp.cos(ar)), by_row(jnp.sin(ar)), by_col(jnp.cos(ac)), by_col(jnp.sin(ac))
    zero = jnp.zeros_like(cos_r)
    cos_t = jnp.concatenate([cos_r, cos_r, cos_c, cos_c], axis=1)
    sin_a = jnp.concatenate([-sin_r, zero, -sin_c, zero], axis=1)
    sin_b = jnp.concatenate([zero, sin_r, zero, sin_c], axis=1)
    return cos_t, sin_a, sin_b


def kernel(x, c, ctx, c_ctx, w_ada, b_ada, w_in, w_gate_up, b_gate, attn_sink, gla_norm_w, w_out, ln1_g, ln1_b, w_router_group, b_router_group, w_router_expert, b_router_expert, w_exp_gate, w_exp_up, w_exp_down, ln2_g, ln2_b):
    batch, seq, d = x.shape
    n_ctx = ctx.shape[1]
    assert w_ada.shape[0] == DEPTH and batch < MOD_ROWS
    assert seq % GLA_STEP == 0 and n_ctx == GLA_STEP
    n_tok = batch * seq
    a_width = d // 2
    kv_width = a_width // A_GROUP
    b_width = d - a_width
    key_width = b_width // 2
    layer = 0

    cc = jnp.concatenate([c, c_ctx[None, :], jnp.zeros((MOD_ROWS - batch - 1, d), F32)], axis=0)
    mod3 = _adaln(cc, w_ada[layer], b_ada[layer]).reshape(MOD_ROWS, 6, d)

    splits = (a_width, kv_width, kv_width, key_width, key_width, b_width, b_width, 2 * GATE_RANK)
    w_in_b = w_in[layer].astype(BF16)
    zero_up = jnp.zeros((GATE_RANK, key_width), F32)
    wup2 = jnp.concatenate([jnp.concatenate([w_gate_up[layer, 0], zero_up], axis=1),
                            jnp.concatenate([zero_up, w_gate_up[layer, 1]], axis=1)], axis=0).astype(BF16)
    bg2 = b_gate[layer].reshape(1, 2 * key_width)
    tables = _rope_tables(seq)

    xf = x.reshape(n_tok, d)
    qa, ka, va, qb, kb, vb, rb, la_f, la_b = _project(
        xf, mod3, lambda i, per_seq: i // per_seq, tables, w_in_b, splits, wup2, bg2, rope=True, seq=seq)
    _, ka_c, va_c, _, kb_c, vb_c, _, lac_f, lac_b = _project(
        ctx.reshape(batch * n_ctx, d), mod3, lambda i, per_seq: batch, tables, w_in_b, splits, wup2, bg2,
        rope=False, seq=seq, keys_values_only=True)

    out_a = _attention(attn_sink[layer], qa, ka, va, ka_c, va_c, batch, seq, n_ctx)
    o_b = _gla(kb_c, vb_c, lac_b, qb, kb, vb, la_b, batch, seq, n_ctx, reverse=True)
    out_b = _gla(kb_c, vb_c, lac_f, qb, kb, vb, la_f, batch, seq, n_ctx, reverse=False,
                 extra=(o_b, rb, gla_norm_w[layer].reshape(1, b_width)))

    w_out_b = w_out[layer].astype(BF16)
    wr = jnp.concatenate([w_router_expert[layer], w_router_group[layer],
                          jnp.zeros((d, LANES - N_EXPERTS - N_GROUPS), F32)], axis=1).astype(BF16)
    br = jnp.concatenate([b_router_expert[layer], b_router_group[layer],
                          jnp.zeros((LANES - N_EXPERTS - N_GROUPS,), F32)]).reshape(1, LANES)
    x1, h2, rw, ri = _outproj(out_a, out_b, xf, mod3, w_out_b[:a_width], w_out_b[a_width:],
                              ln1_g[layer], ln1_b[layer], wr, br, seq)

    dest, tail_row, empty_block, plan, n_buf_rows = _routing_plan(ri, n_tok)
    buf = _dispatch(dest, tail_row, empty_block, h2)
    y = _experts(plan, buf, w_exp_gate[layer], w_exp_up[layer], w_exp_down[layer], n_buf_rows)
    out = _combine(dest, y, rw, x1, mod3, ln2_g[layer], ln2_b[layer], seq)
    return out.reshape(batch, seq, d)
```
